```python
import math
import jax, jax.numpy as jnp
from jax import lax
import numpy as np

D_MODEL = 2048
BATCH = 4
SEQ = 2048
DEPTH = 1
DEC_BATCH = 128
DEC_SEQ = 4
PAST_LEN = 2048
PAGE_SIZE = 128

D_MIX = D_MODEL
D_ATT = D_MIX // 2
D_SSD = D_MIX - D_ATT
ATT_DK = 64
ATT_DV = 2 * ATT_DK
N_ATT_HEADS = D_ATT // ATT_DV
ROT_DIM = ATT_DK // 4
ROPE_THETA = 500000.0
Q_BLOCK = 128
SSD_HEADDIM = 64
SSD_HEADS = D_SSD // SSD_HEADDIM
SSD_GROUPS = 2
SSD_STATE = 128
SSD_CONV = 4
SSD_CHUNK = 128
CONV_DIM = D_SSD + 2 * SSD_GROUPS * SSD_STATE
Q_DIM = N_ATT_HEADS * 2 * ATT_DK
IN_PROJ_DIM = 2 * Q_DIM + D_ATT + D_SSD + CONV_DIM + SSD_HEADS
D_FF = 5632
FFN_CONV = 3
EPS = 1e-6

kernel_name = "hymba_diffattn_mamba2_convffn_step"


def rmsnorm(x, w):
    xf = x.astype(jnp.float32)
    y = xf * lax.rsqrt(jnp.mean(xf * xf, axis=-1, keepdims=True) + EPS)
    return (y * w.astype(jnp.float32)).astype(x.dtype)


def partial_rope(x, pos):
    half = ROT_DIM // 2
    inv = ROPE_THETA ** (-jnp.arange(half, dtype=jnp.float32) / half)
    ang = pos.astype(jnp.float32)[:, None] * inv[None, :]
    cos = jnp.cos(ang)[None, :, None, None, :].astype(x.dtype)
    sin = jnp.sin(ang)[None, :, None, None, :].astype(x.dtype)
    x1 = x[..., :half]
    x2 = x[..., half:ROT_DIM]
    return jnp.concatenate([x1 * cos - x2 * sin, x2 * cos + x1 * sin, x[..., ROT_DIM:]], axis=-1)


def causal_dwconv(x, prev, w, b):
    width = w.shape[0]
    L = x.shape[1]
    xp = jnp.concatenate([prev.astype(x.dtype), x], axis=1)
    out = b
    for j in range(width):
        out = out + xp[:, j:j + L] * w[j]
    return out, xp[:, L:]


def diff_attend(q, k, v, mask, lam):
    s = jnp.einsum("bqhcd,bkhcd->bhcqk", q, k).astype(jnp.float32) * (ATT_DK ** -0.5)
    s = jnp.where(mask[None, None, None], s, -jnp.inf)
    p = jax.nn.softmax(s, axis=-1)
    a = p[:, :, 0] - lam * p[:, :, 1]
    return jnp.einsum("bhqk,bkhd->bqhd", a.astype(v.dtype), v)


def attend_prompt(q, k, v, lam):
    b, s = q.shape[0], q.shape[1]
    nb = s // Q_BLOCK
    qb = q.reshape(b, nb, Q_BLOCK, N_ATT_HEADS, 2, ATT_DK).swapaxes(0, 1)
    kpos = jnp.arange(s)

    def block(args):
        i, qi = args
        qpos = i * Q_BLOCK + jnp.arange(Q_BLOCK)
        return diff_attend(qi, k, v, kpos[None, :] <= qpos[:, None], lam)

    out = lax.map(block, (jnp.arange(nb), qb))
    return out.swapaxes(0, 1).reshape(b, s, N_ATT_HEADS, ATT_DV)


def ssd_scan(x, dt, A, Bm, Cm, h0):
    b, l = x.shape[0], x.shape[1]
    q = SSD_CHUNK if l % SSD_CHUNK == 0 else l
    c = l // q
    R = SSD_HEADS // SSD_GROUPS
    f32 = jnp.float32
    xdt = (x.astype(f32) * dt[..., None]).reshape(b, c, q, SSD_GROUPS, R, SSD_HEADDIM)
    a = (dt * A).reshape(b, c, q, SSD_GROUPS, R)
    Bc = Bm.astype(f32).reshape(b, c, q, SSD_GROUPS, SSD_STATE)
    Cc = Cm.astype(f32).reshape(b, c, q, SSD_GROUPS, SSD_STATE)
    a_cs = jnp.cumsum(a, axis=2)
    seg = a_cs[:, :, :, None] - a_cs[:, :, None, :]
    tri = jnp.tril(jnp.ones((q, q), dtype=bool))[None, None, :, :, None, None]
    Lmat = jnp.exp(jnp.where(tri, seg, -jnp.inf))
    cb = jnp.einsum("bclgn,bcsgn->bclsg", Cc, Bc)
    y_diag = jnp.einsum("bclsg,bclsgr,bcsgrp->bclgrp", cb, Lmat, xdt)
    decay = jnp.exp(a_cs[:, :, -1:] - a_cs)
    chunk_states = jnp.einsum("bcsgn,bcsgr,bcsgrp->bcgrpn", Bc, decay, xdt)
    chunk_decay = jnp.exp(a_cs[:, :, -1])

    def step(h, inp):
        st, dec = inp
        return h * dec[..., None, None] + st, h

    h_init = h0.astype(f32).reshape(b, SSD_GROUPS, R, SSD_HEADDIM, SSD_STATE)
    h_final, h_prev = lax.scan(step, h_init, (chunk_states.swapaxes(0, 1), chunk_decay.swapaxes(0, 1)))
    h_prev = h_prev.swapaxes(0, 1)
    y_off = jnp.einsum("bclgn,bcgrpn,bclgr->bclgrp", Cc, h_prev, jnp.exp(a_cs))
    y = (y_diag + y_off).reshape(b, l, SSD_HEADS, SSD_HEADDIM)
    return y, h_final.reshape(b, SSD_HEADS, SSD_HEADDIM, SSD_STATE)


def hybrid_layer(x, positions, past_k, past_v, conv_ssd_prev, ssm_prev, conv_ffn_prev, lam_init,
                 norm_mix_w, w_in, lambda_q1, lambda_k1, lambda_q2, lambda_k2, subln_w,
                 conv_ssd_w, conv_ssd_b, dt_bias, a_log, d_skip, norm_ssd_w, w_out,
                 norm_ffn_w, w_gate, w_up, conv_ffn_w, conv_ffn_b, w_down):
    b, s = x.shape[0], x.shape[1]
    h = rmsnorm(x, norm_mix_w)
    proj = h @ w_in
    o1 = Q_DIM
    o2 = o1 + Q_DIM
    o3 = o2 + D_ATT
    o4 = o3 + D_SSD
    o5 = o4 + CONV_DIM
    q = partial_rope(proj[..., :o1].reshape(b, s, N_ATT_HEADS, 2, ATT_DK), positions)
    k = partial_rope(proj[..., o1:o2].reshape(b, s, N_ATT_HEADS, 2, ATT_DK), positions)
    v = proj[..., o2:o3].reshape(b, s, N_ATT_HEADS, ATT_DV)
    z = proj[..., o3:o4]
    xbc = proj[..., o4:o5]
    dt_raw = proj[..., o5:]

    f32 = jnp.float32
    lam = (jnp.exp(jnp.sum(lambda_q1.astype(f32) * lambda_k1.astype(f32)))
           - jnp.exp(jnp.sum(lambda_q2.astype(f32) * lambda_k2.astype(f32))) + lam_init)
    if past_k is None:
        att = attend_prompt(q, k, v, lam)
    else:
        past_len = past_k.shape[1]
        k_all = jnp.concatenate([past_k.astype(k.dtype), k], axis=1)
        v_all = jnp.concatenate([past_v.astype(v.dtype), v], axis=1)
        kpos = jnp.arange(past_len + s)
        mask = kpos[None, :] <= positions[:, None]
        att = diff_attend(q, k_all, v_all, mask, lam)
    att = (rmsnorm(att, subln_w) * (1.0 - lam_init)).reshape(b, s, D_ATT)

    xbc_c, new_conv_ssd = causal_dwconv(xbc, conv_ssd_prev, conv_ssd_w, conv_ssd_b)
    xbc_c = jax.nn.silu(xbc_c)
    xs = xbc_c[..., :D_SSD].reshape(b, s, SSD_HEADS, SSD_HEADDIM)
    Bm = xbc_c[..., D_SSD:D_SSD + SSD_GROUPS * SSD_STATE].reshape(b, s, SSD_GROUPS, SSD_STATE)
    Cm = xbc_c[..., D_SSD + SSD_GROUPS * SSD_STATE:].reshape(b, s, SSD_GROUPS, SSD_STATE)
    dt = jax.nn.softplus(dt_raw.astype(f32) + dt_bias.astype(f32))
    A = -jnp.exp(a_log.astype(f32))
    y, ssm_new = ssd_scan(xs, dt, A, Bm, Cm, ssm_prev)
    y = (y + d_skip.astype(f32)[:, None] * xs.astype(f32)).astype(x.dtype).reshape(b, s, D_SSD)
    y = y * jax.nn.silu(z)
    y = rmsnorm(y.reshape(b, s, SSD_GROUPS, D_SSD // SSD_GROUPS),
                norm_ssd_w.reshape(SSD_GROUPS, D_SSD // SSD_GROUPS)).reshape(b, s, D_SSD)

    x = x + jnp.concatenate([att, y], axis=-1) @ w_out

    hf = rmsnorm(x, norm_ffn_w)
    g = hf @ w_gate
    u = hf @ w_up
    g_c, new_conv_ffn = causal_dwconv(g, conv_ffn_prev, conv_ffn_w, conv_ffn_b)
    x = x + (jax.nn.silu(g_c) * u) @ w_down
    return x, k, v, ssm_new.astype(ssm_prev.dtype), new_conv_ssd, new_conv_ffn


def setup_inputs(seed: int = 0) -> dict:
    key = jax.random.key(seed)
    ks = jax.random.split(key, 32)
    f32 = jnp.float32
    n_pages = PAST_LEN // PAGE_SIZE
    used = DEC_BATCH * n_pages
    n_phys = used + max(1, used // 4)
    nrm = lambda k, shape, scale: jax.random.normal(k, shape, f32) * scale
    page_table = jax.random.permutation(ks[7], n_phys)[:used].reshape(DEC_BATCH, n_pages).astype(jnp.int32)
    dt0 = jnp.exp(jax.random.uniform(ks[13], (DEPTH, SSD_HEADS), f32) * (math.log(0.1) - math.log(0.001)) + math.log(0.001))
    return {
        "x_prompt": nrm(ks[0], (BATCH, SEQ, D_MODEL), 1.0),
        "x_sample": nrm(ks[1], (DEC_BATCH, DEC_SEQ, D_MODEL), 1.0),
        "cache_k": nrm(ks[2], (DEPTH, n_phys, PAGE_SIZE, N_ATT_HEADS, 2, ATT_DK), 1.0),
        "cache_v": nrm(ks[3], (DEPTH, n_phys, PAGE_SIZE, N_ATT_HEADS, ATT_DV), 1.0),
        "state_ssm": nrm(ks[4], (DEPTH, DEC_BATCH, SSD_HEADS, SSD_HEADDIM, SSD_STATE), 0.1),
        "state_conv_ssd": nrm(ks[5], (DEPTH, DEC_BATCH, SSD_CONV - 1, CONV_DIM), 1.0),
        "state_conv_ffn": nrm(ks[6], (DEPTH, DEC_BATCH, FFN_CONV - 1, D_FF), 1.0),
        "page_table": page_table,
        "norm_mix_w": 1.0 + nrm(ks[8], (DEPTH, D_MODEL), 0.02),
        "w_in": nrm(ks[9], (DEPTH, D_MODEL, IN_PROJ_DIM), D_MODEL ** -0.5),
        "lambda_q1": nrm(ks[10], (DEPTH, ATT_DK), 0.1),
        "lambda_k1": nrm(ks[11], (DEPTH, ATT_DK), 0.1),
        "lambda_q2": nrm(ks[12], (DEPTH, ATT_DK), 0.1),
        "lambda_k2": nrm(ks[14], (DEPTH, ATT_DK), 0.1),
        "subln_w": 1.0 + nrm(ks[15], (DEPTH, ATT_DV), 0.02),
        "conv_ssd_w": nrm(ks[16], (DEPTH, SSD_CONV, CONV_DIM), SSD_CONV ** -0.5),
        "conv_ssd_b": nrm(ks[17], (DEPTH, CONV_DIM), 0.01),
        "dt_bias": dt0 + jnp.log(-jnp.expm1(-dt0)),
        "a_log": jnp.log(jax.random.uniform(ks[18], (DEPTH, SSD_HEADS), f32, 1.0, 16.0)),
        "d_skip": 1.0 + nrm(ks[19], (DEPTH, SSD_HEADS), 0.02),
        "norm_ssd_w": 1.0 + nrm(ks[20], (DEPTH, D_SSD), 0.02),
        "w_out": nrm(ks[21], (DEPTH, D_MIX, D_MODEL), D_MIX ** -0.5),
        "norm_ffn_w": 1.0 + nrm(ks[22], (DEPTH, D_MODEL), 0.02),
        "w_gate": nrm(ks[23], (DEPTH, D_MODEL, D_FF), D_MODEL ** -0.5),
        "w_up": nrm(ks[24], (DEPTH, D_MODEL, D_FF), D_MODEL ** -0.5),
        "conv_ffn_w": nrm(ks[25], (DEPTH, FFN_CONV, D_FF), FFN_CONV ** -0.5),
        "conv_ffn_b": nrm(ks[26], (DEPTH, D_FF), 0.01),
        "w_down": nrm(ks[27], (DEPTH, D_FF, D_MODEL), D_FF ** -0.5),
        "norm_final_w": 1.0 + nrm(ks[28], (D_MODEL,), 0.02),
    }


def reference(x_prompt, x_sample, cache_k, cache_v, state_ssm, state_conv_ssd, state_conv_ffn, page_table,
              norm_mix_w, w_in, lambda_q1, lambda_k1, lambda_q2, lambda_k2, subln_w,
              conv_ssd_w, conv_ssd_b, dt_bias, a_log, d_skip, norm_ssd_w, w_out,
              norm_ffn_w, w_gate, w_up, conv_ffn_w, conv_ffn_b, w_down, norm_final_w):
    bp, sp = x_prompt.shape[0], x_prompt.shape[1]
    db, ds = x_sample.shape[0], x_sample.shape[1]
    n_pages = page_table.shape[1]
    past_len = n_pages * PAGE_SIZE
    pos_p = jnp.arange(sp)
    pos_s = past_len + jnp.arange(ds)
    xp, xs = x_prompt, x_sample
    kp_l, vp_l, sp_l, cp_l, fp_l = [], [], [], [], []
    ks_l, vs_l, ss_l, cs_l, fs_l = [], [], [], [], []
    for i in range(DEPTH):
        lam_init = 0.8 - 0.6 * math.exp(-0.3 * i)
        params = (norm_mix_w[i], w_in[i], lambda_q1[i], lambda_k1[i], lambda_q2[i], lambda_k2[i], subln_w[i],
                  conv_ssd_w[i], conv_ssd_b[i], dt_bias[i], a_log[i], d_skip[i], norm_ssd_w[i], w_out[i],
                  norm_ffn_w[i], w_gate[i], w_up[i], conv_ffn_w[i], conv_ffn_b[i], w_down[i])
        xp, kp, vp, ssp, cvp, ffp = hybrid_layer(
            xp, pos_p, None, None,
            jnp.zeros((bp, SSD_CONV - 1, CONV_DIM), xp.dtype),
            jnp.zeros((bp, SSD_HEADS, SSD_HEADDIM, SSD_STATE), xp.dtype),
            jnp.zeros((bp, FFN_CONV - 1, D_FF), xp.dtype),
            lam_init, *params)
        past_k = cache_k[i, page_table].reshape(db, past_len, N_ATT_HEADS, 2, ATT_DK)
        past_v = cache_v[i, page_table].reshape(db, past_len, N_ATT_HEADS, ATT_DV)
        xs, ksm, vsm, sss, cvs, ffs = hybrid_layer(
            xs, pos_s, past_k, past_v, state_conv_ssd[i], state_ssm[i], state_conv_ffn[i],
            lam_init, *params)
        kp_l.append(kp); vp_l.append(vp); sp_l.append(ssp); cp_l.append(cvp); fp_l.append(ffp)
        ks_l.append(ksm); vs_l.append(vsm); ss_l.append(sss); cs_l.append(cvs); fs_l.append(ffs)
    y_prompt = rmsnorm(xp, norm_final_w)
    y_sample = rmsnorm(xs, norm_final_w)
    return (y_prompt, y_sample,
            jnp.stack(kp_l), jnp.stack(vp_l), jnp.stack(sp_l), jnp.stack(cp_l), jnp.stack(fp_l),
            jnp.stack(ks_l), jnp.stack(vs_l), jnp.stack(ss_l), jnp.stack(cs_l), jnp.stack(fs_l))
```

```python
import functools
import math

import jax
import jax.numpy as jnp
from jax import lax
from jax.experimental import pallas as pl
from jax.experimental.pallas import tpu as pltpu

F32 = jnp.float32
BF16 = jnp.bfloat16

D_MODEL = 2048
ATT_DK = 64
ATT_DV = 128
N_ATT_HEADS = 8
ROT_DIM = 16
ROPE_THETA = 500000.0
Q_DIM = N_ATT_HEADS * 2 * ATT_DK
D_ATT = N_ATT_HEADS * ATT_DV
D_SSD = 1024
SSD_HEADDIM = 64
SSD_HEADS = 16
SSD_GROUPS = 2
SSD_STATE = 128
SSD_CONV = 4
SSD_CHUNK = 128
CONV_DIM = D_SSD + 2 * SSD_GROUPS * SSD_STATE
D_FF = 5632
FFN_CONV = 3
EPS = 1e-6
PAGE_SIZE = 128

LANES = 128
SUBLANES = 8
VMEM_LIMIT_BYTES = 56 * 1024 * 1024
DT_PAD = LANES
IN_PROJ_PAD = 2 * Q_DIM + D_ATT + D_SSD + CONV_DIM + DT_PAD
PROJ_CHUNK = 512
NEG_INF = float("-inf")


def _cparams(sem):
    return pltpu.CompilerParams(dimension_semantics=sem, vmem_limit_bytes=VMEM_LIMIT_BYTES)


def _sigmoid(x):
    return 1.0 / (1.0 + jnp.exp(-x))


def _rms_rows(x, w):
    return x * lax.rsqrt(jnp.mean(x * x, axis=-1, keepdims=True) + EPS) * w


def _inproj_kernel(x_ref, nw_ref, w_ref, cos_ref, sa_ref, sb_ref,
                   q_ref, k_ref, v_ref, z_ref, xbc_ref, dt_ref):
    xn = _rms_rows(x_ref[...], nw_ref[...]).astype(BF16)
    cos, sa, sb = cos_ref[...], sa_ref[...], sb_ref[...]

    def rope(p):
        up = pltpu.roll(p, LANES - ROT_DIM // 2, axis=1)
        dn = pltpu.roll(p, ROT_DIM // 2, axis=1)
        return p * cos + up * sa + dn * sb

    col = 0
    for ref, width, rot in ((q_ref, Q_DIM, True), (k_ref, Q_DIM, True), (v_ref, D_ATT, False),
                            (z_ref, D_SSD, False), (xbc_ref, CONV_DIM, False), (dt_ref, DT_PAD, False)):
        for c0 in range(0, width, PROJ_CHUNK):
            cw = min(PROJ_CHUNK, width - c0)
            p = jnp.dot(xn, w_ref[:, col + c0:col + c0 + cw], preferred_element_type=F32)
            if rot:
                for s in range(0, cw, LANES):
                    ref[:, c0 + s:c0 + s + LANES] = rope(p[:, s:s + LANES])
            else:
                ref[:, c0:c0 + cw] = p
        col += width


def _in_proj(x2d, nw, w_pad, cos_t, sa_t, sb_t):
    m = x2d.shape[0]
    tm = min(256, m)
    row = lambda i: (i, 0)
    const = lambda i: (0, 0)
    widths = (Q_DIM, Q_DIM, D_ATT, D_SSD, CONV_DIM, DT_PAD)
    return pl.pallas_call(
        _inproj_kernel,
        grid=(m // tm,),
        in_specs=[pl.BlockSpec((tm, D_MODEL), row),
                  pl.BlockSpec((1, D_MODEL), const),
                  pl.BlockSpec((D_MODEL, IN_PROJ_PAD), const, pipeline_mode=pl.Buffered(1)),
                  pl.BlockSpec((tm, LANES), row), pl.BlockSpec((tm, LANES), row), pl.BlockSpec((tm, LANES), row)],
        out_specs=[pl.BlockSpec((tm, w), row) for w in widths],
        out_shape=[jax.ShapeDtypeStruct((m, w), F32) for w in widths],
        compiler_params=_cparams(("arbitrary",)),
        name="in_proj",
    )(x2d, nw, w_pad, cos_t, sa_t, sb_t)


def _rope_tables(pos):
    half = ROT_DIM // 2
    inv = ROPE_THETA ** (-jnp.arange(half, dtype=F32) / half)
    ang = pos.astype(F32)[:, None] * inv[None, :]
    cos, sin = jnp.cos(ang), jnp.sin(ang)
    rows = pos.shape[0]
    ones = jnp.ones((rows, ATT_DK - ROT_DIM), F32)
    zeros = jnp.zeros((rows, ATT_DK - ROT_DIM), F32)
    zh = jnp.zeros((rows, half), F32)
    cos64 = jnp.concatenate([cos, cos, ones], axis=1)
    sa64 = jnp.concatenate([-sin, zh, zeros], axis=1)
    sb64 = jnp.concatenate([zh, sin, zeros], axis=1)
    two = lambda t: jnp.concatenate([t, t], axis=1)
    return two(cos64), two(sa64), two(sb64)


def _lambda_full(lam_ref, lam_init):
    l = lam_ref[...]
    s1 = jnp.sum(l[0:1] * l[1:2], axis=1, keepdims=True)
    s2 = jnp.sum(l[2:3] * l[3:4], axis=1, keepdims=True)
    return jnp.exp(s1) - jnp.exp(s2) + lam_init


def _attn_prompt_kernel(lam_ref, subw_ref, q_ref, k_ref, v_ref, o_ref, *, tq, tk, lam_init):
    i = pl.program_id(2)
    lam = _lambda_full(lam_ref, lam_init)
    q = q_ref[0] * (ATT_DK ** -0.5)
    lane = lax.broadcasted_iota(jnp.int32, (tq, LANES), 1)
    qq = jnp.concatenate([jnp.where(lane < ATT_DK, q, 0.0), jnp.where(lane >= ATT_DK, q, 0.0)], axis=0).astype(BF16)
    r = lax.broadcasted_iota(jnp.int32, (2 * tq, tk), 0)
    qpos = i * tq + jnp.where(r >= tq, r - tq, r)
    kcol = lax.broadcasted_iota(jnp.int32, (2 * tq, tk), 1)

    def body(j, carry):
        m, l, acc = carry
        start = pl.multiple_of(j * tk, tk)
        kt = k_ref[0, pl.ds(start, tk), :].astype(BF16)
        vt = v_ref[0, pl.ds(start, tk), :].astype(BF16)
        s = lax.dot_general(qq, kt, (((1,), (1,)), ((), ())), preferred_element_type=F32)
        s = jnp.where(j * tk + kcol <= qpos, s, NEG_INF)
        m_new = jnp.maximum(m, jnp.max(s, axis=1, keepdims=True))
        alpha = jnp.exp(m - m_new)
        p = jnp.exp(s - m_new)
        l = alpha * l + jnp.sum(p, axis=1, keepdims=True)
        acc = alpha * acc + jnp.dot(p.astype(BF16), vt, preferred_element_type=F32)
        return m_new, l, acc

    init = (jnp.full((2 * tq, 1), NEG_INF, F32), jnp.zeros((2 * tq, 1), F32), jnp.zeros((2 * tq, ATT_DV), F32))
    n_kv = (i * tq + tq + tk - 1) // tk
    _, l, acc = lax.fori_loop(0, n_kv, body, init)
    o = acc / l
    att = o[:tq] - lam * o[tq:]
    o_ref[0] = _rms_rows(att, subw_ref[...]) * (1.0 - lam_init)


def _attn_prompt(lam_vecs, subw, q, k, v, lam_init):
    b, s, _ = q.shape
    tq = min(256, s)
    tk = tq
    kern = functools.partial(_attn_prompt_kernel, tq=tq, tk=tk, lam_init=lam_init)
    return pl.pallas_call(
        kern,
        grid=(b, N_ATT_HEADS, s // tq),
        in_specs=[pl.BlockSpec((4, ATT_DK), lambda bb, h, i: (0, 0)),
                  pl.BlockSpec((1, ATT_DV), lambda bb, h, i: (0, 0)),
                  pl.BlockSpec((1, tq, LANES), lambda bb, h, i: (bb, i, h)),
                  pl.BlockSpec((1, s, LANES), lambda bb, h, i: (bb, 0, h)),
                  pl.BlockSpec((1, s, LANES), lambda bb, h, i: (bb, 0, h))],
        out_specs=pl.BlockSpec((1, tq, LANES), lambda bb, h, i: (bb, i, h)),
        out_shape=jax.ShapeDtypeStruct((b, s, D_ATT), F32),
        compiler_params=_cparams(("arbitrary", "arbitrary", "arbitrary")),
        name="attn_prompt",
    )(lam_vecs, subw, q, k, v)


def _attn_sample_kernel(pt_ref, lam_ref, subw_ref, q_ref, kn_ref, vn_ref, *rest, n_pages, t_new, lam_init):
    kpages = rest[:n_pages]
    vpages = rest[n_pages:2 * n_pages]
    o_ref, kbf, vbf = rest[2 * n_pages:]
    past = n_pages * PAGE_SIZE
    n_keys = past + PAGE_SIZE
    for j in range(n_pages):
        kbf[:, j * PAGE_SIZE:(j + 1) * PAGE_SIZE] = kpages[j][0].astype(BF16)
        vbf[j * PAGE_SIZE:(j + 1) * PAGE_SIZE, :] = vpages[j][0].astype(BF16)
    zpad = jnp.zeros((PAGE_SIZE - t_new, Q_DIM), F32)
    k_tail = jnp.concatenate([kn_ref[0], zpad], axis=0).astype(BF16)
    vbf[past:n_keys, :] = jnp.concatenate([vn_ref[0], zpad], axis=0).astype(BF16)

    lam = _lambda_full(lam_ref, lam_init)
    q = q_ref[0] * (ATT_DK ** -0.5)
    nr = t_new * N_ATT_HEADS
    qrep = jnp.concatenate([jnp.broadcast_to(q[t:t + 1], (N_ATT_HEADS, Q_DIM)) for t in range(t_new)], axis=0)
    row = lax.broadcasted_iota(jnp.int32, (nr, Q_DIM), 0)
    lane = lax.broadcasted_iota(jnp.int32, (nr, Q_DIM), 1)
    head = row % N_ATT_HEADS
    grp = lane // ATT_DK
    qq = jnp.concatenate([jnp.where(grp == 2 * head, qrep, 0.0), jnp.where(grp == 2 * head + 1, qrep, 0.0)],
                         axis=0).astype(BF16)
    s_past = jnp.dot(qq, kbf[...], preferred_element_type=F32)
    s_tail = lax.dot_general(qq, k_tail, (((1,), (1,)), ((), ())), preferred_element_type=F32)
    s = jnp.concatenate([s_past, s_tail], axis=1)
    srow = lax.broadcasted_iota(jnp.int32, (2 * nr, n_keys), 0)
    scol = lax.broadcasted_iota(jnp.int32, (2 * nr, n_keys), 1)
    tq = (srow // N_ATT_HEADS) % t_new
    s = jnp.where(scol <= past + tq, s, NEG_INF)
    m = jnp.max(s, axis=1, keepdims=True)
    p = jnp.exp(s - m)
    pn = p / jnp.sum(p, axis=1, keepdims=True)
    a = pn[:nr] - lam * pn[nr:]
    o = jnp.dot(a.astype(BF16), vbf[...], preferred_element_type=F32)
    o = jnp.where(lane // ATT_DV == head, o, 0.0)
    ms = jnp.sum(o * o, axis=1, keepdims=True) * (1.0 / ATT_DV)
    y = o * lax.rsqrt(ms + EPS)
    att = jnp.concatenate([jnp.sum(y[t * N_ATT_HEADS:(t + 1) * N_ATT_HEADS], axis=0, keepdims=True)
                           for t in range(t_new)], axis=0)
    o_ref[0] = att * subw_ref[...] * (1.0 - lam_init)


def _attn_sample(page_table, lam_vecs, subw_tiled, q, kn, vn, cache_k, cache_v, lam_init):
    db, t_new, _ = q.shape
    n_pages = page_table.shape[1]
    n_keys = (n_pages + 1) * PAGE_SIZE
    kern = functools.partial(_attn_sample_kernel, n_pages=n_pages, t_new=t_new, lam_init=lam_init)
    tok = pl.BlockSpec((1, t_new, Q_DIM), lambda b, pt: (b, 0, 0))
    page_maps = [functools.partial(lambda b, pt, j: (pt[b, j], 0, 0), j=j) for j in range(n_pages)]
    kpage_specs = [pl.BlockSpec((1, Q_DIM, PAGE_SIZE), pm) for pm in page_maps]
    vpage_specs = [pl.BlockSpec((1, PAGE_SIZE, D_ATT), pm) for pm in page_maps]
    grid_spec = pltpu.PrefetchScalarGridSpec(
        num_scalar_prefetch=1,
        grid=(db,),
        in_specs=[pl.BlockSpec((4, ATT_DK), lambda b, pt: (0, 0)),
                  pl.BlockSpec((1, D_ATT), lambda b, pt: (0, 0)),
                  tok, tok, tok] + kpage_specs + vpage_specs,
        out_specs=tok,
        scratch_shapes=[pltpu.VMEM((Q_DIM, n_keys - PAGE_SIZE), BF16), pltpu.VMEM((n_keys, D_ATT), BF16)],
    )
    return pl.pallas_call(
        kern,
        grid_spec=grid_spec,
        out_shape=jax.ShapeDtypeStruct((db, t_new, D_ATT), F32),
        compiler_params=_cparams(("arbitrary",)),
        name="attn_sample",
    )(page_table, lam_vecs, subw_tiled, q, kn, vn, *([cache_k] * n_pages), *([cache_v] * n_pages))


def _expand_heads(v, rows):
    return jnp.concatenate([jnp.broadcast_to(v[:, h:h + 1], (rows, SSD_HEADDIM)) for h in range(SSD_HEADS)], axis=1)


def _ssd_kernel(xbc_ref, prev_ref, dt_ref, z_ref, h0_ref, cw_ref, cb_ref, dtb_ref, alog_ref, dskip_ref, nw_ref,
                y_ref, hout_ref, xprev, *, rows, valid_rows, carry):
    c = pl.program_id(1)

    @pl.when(c == 0)
    def _():
        hout_ref[...] = h0_ref[...]
        xprev[...] = prev_ref[0]

    xr = xbc_ref[0]
    pv = xprev[...]
    row8 = lax.broadcasted_iota(jnp.int32, (SUBLANES, CONV_DIM), 0)
    conv = cb_ref[...] + cw_ref[SSD_CONV - 1:SSD_CONV] * xr
    for k in range(1, SSD_CONV):
        sh = pltpu.roll(xr, k, axis=0)
        top = jnp.where(row8 < k, pltpu.roll(pv, k, axis=0), sh[:SUBLANES])
        sh = top if rows == SUBLANES else jnp.concatenate([top, sh[SUBLANES:]], axis=0)
        conv = conv + cw_ref[SSD_CONV - 1 - k:SSD_CONV - k] * sh
    if carry:
        xprev[...] = xr[rows - SUBLANES:]
    xc = conv * _sigmoid(conv)
    xs = xc[:, :D_SSD]

    dtv = dt_ref[0] + dtb_ref[...]
    dtv = jnp.maximum(dtv, 0.0) + jnp.log1p(jnp.exp(-jnp.abs(dtv)))
    if valid_rows < rows:
        rvalid = lax.broadcasted_iota(jnp.int32, (rows, LANES), 0) < valid_rows
        dtv = jnp.where(rvalid, dtv, 0.0)
    a = dtv * (-jnp.exp(alog_ref[...]))
    ri = lax.broadcasted_iota(jnp.int32, (rows, rows), 0)
    ci = lax.broadcasted_iota(jnp.int32, (rows, rows), 1)
    tril = ri >= ci
    a_cs = jnp.dot(tril.astype(F32), a, preferred_element_type=F32, precision=lax.Precision.HIGHEST)
    total = a_cs[rows - 1:rows]
    e_cs = jnp.exp(a_cs)
    e_tot = jnp.exp(total)
    xdt = xs * _expand_heads(dtv, rows)
    xdtd = xdt * _expand_heads(jnp.exp(total - a_cs), rows)
    e_exp = _expand_heads(e_cs, rows)
    eye = ri == ci

    y_diag_parts, y_off_parts = [], []
    heads_per_group = SSD_HEADS // SSD_GROUPS
    gw = heads_per_group * SSD_HEADDIM
    for g in range(SSD_GROUPS):
        bg = xc[:, D_SSD + g * SSD_STATE:D_SSD + (g + 1) * SSD_STATE].astype(BF16)
        cg = xc[:, D_SSD + (SSD_GROUPS + g) * SSD_STATE:D_SSD + (SSD_GROUPS + g + 1) * SSD_STATE].astype(BF16)
        hg = hout_ref[0, g * heads_per_group:(g + 1) * heads_per_group].reshape(gw, SSD_STATE)
        y_off = lax.dot_general(cg, hg.astype(BF16), (((1,), (1,)), ((), ())), preferred_element_type=F32)
        st = lax.dot_general(xdtd[:, g * gw:(g + 1) * gw].astype(BF16), bg, (((0,), (0,)), ((), ())),
                             preferred_element_type=F32)
        cbm = lax.dot_general(cg, bg, (((1,), (1,)), ((), ())), preferred_element_type=F32)
        for r in range(heads_per_group):
            h = g * heads_per_group + r
            col = a_cs[:, h:h + 1]
            rowv = jnp.sum(jnp.where(eye, col, 0.0), axis=0, keepdims=True)
            lmat = jnp.where(tril, jnp.exp(col - rowv), 0.0)
            mm = (cbm * lmat).astype(BF16)
            y_diag_parts.append(jnp.dot(mm, xdt[:, h * SSD_HEADDIM:(h + 1) * SSD_HEADDIM].astype(BF16),
                                        preferred_element_type=F32))
            hh = hout_ref[0, h]
            hout_ref[0, h] = hh * e_tot[:, h:h + 1] + st[r * SSD_HEADDIM:(r + 1) * SSD_HEADDIM]
        y_off_parts.append(y_off)
    y = (jnp.concatenate(y_diag_parts, axis=1) + jnp.concatenate(y_off_parts, axis=1) * e_exp
         + dskip_ref[...] * xs)
    zz = z_ref[0]
    y = y * (zz * _sigmoid(zz))
    half = D_SSD // SSD_GROUPS
    nw = nw_ref[...]
    y_ref[0] = jnp.concatenate([_rms_rows(y[:, g * half:(g + 1) * half], nw[:, g * half:(g + 1) * half])
                                for g in range(SSD_GROUPS)], axis=1)


def _ssd(xbc, prev8, dt, z, h0, cw, cb, dtb, alog, dskip_e, nw, *, rows, valid_rows):
    b, s, _ = xbc.shape
    n_chunks = s // rows
    kern = functools.partial(_ssd_kernel, rows=rows, valid_rows=valid_rows, carry=n_chunks > 1)
    tokmap = lambda bb, c: (bb, c, 0)
    seqmap3 = lambda bb, c: (bb, 0, 0)
    seqmap4 = lambda bb, c: (bb, 0, 0, 0)
    const = lambda bb, c: (0, 0)
    state_spec = pl.BlockSpec((1, SSD_HEADS, SSD_HEADDIM, SSD_STATE), seqmap4)
    return pl.pallas_call(
        kern,
        grid=(b, n_chunks),
        in_specs=[pl.BlockSpec((1, rows, CONV_DIM), tokmap),
                  pl.BlockSpec((1, SUBLANES, CONV_DIM), seqmap3),
                  pl.BlockSpec((1, rows, DT_PAD), tokmap),
                  pl.BlockSpec((1, rows, D_SSD), tokmap),
                  state_spec,
                  pl.BlockSpec((SSD_CONV, CONV_DIM), const),
                  pl.BlockSpec((1, CONV_DIM), const),
                  pl.BlockSpec((1, DT_PAD), const),
                  pl.BlockSpec((1, DT_PAD), const),
                  pl.BlockSpec((1, D_SSD), const),
                  pl.BlockSpec((1, D_SSD), const)],
        out_specs=[pl.BlockSpec((1, rows, D_SSD), tokmap), state_spec],
        out_shape=[jax.ShapeDtypeStruct((b, s, D_SSD), F32),
                   jax.ShapeDtypeStruct((b, SSD_HEADS, SSD_HEADDIM, SSD_STATE), F32)],
        scratch_shapes=[pltpu.VMEM((SUBLANES, CONV_DIM), F32)],
        compiler_params=_cparams(("arbitrary", "arbitrary")),
        name="ssd",
    )(xbc, prev8, dt, z, h0, cw, cb, dtb, alog, dskip_e, nw)


def _outproj_kernel(x_ref, att_ref, y_ref, wa_ref, wy_ref, nw_ref, x1_ref, hf_ref):
    mix = jnp.dot(att_ref[...].astype(BF16), wa_ref[...], preferred_element_type=F32)
    mix = mix + jnp.dot(y_ref[...].astype(BF16), wy_ref[...], preferred_element_type=F32)
    x1 = x_ref[...] + mix
    x1_ref[...] = x1
    hf_ref[...] = _rms_rows(x1, nw_ref[...]).astype(BF16)


def _out_proj(x2d, att, y, wa, wy, nw):
    m = x2d.shape[0]
    tm = min(512, m)
    row = lambda i: (i, 0)
    const = lambda i: (0, 0)
    return pl.pallas_call(
        _outproj_kernel,
        grid=(m // tm,),
        in_specs=[pl.BlockSpec((tm, D_MODEL), row), pl.BlockSpec((tm, D_ATT), row), pl.BlockSpec((tm, D_SSD), row),
                  pl.BlockSpec((D_ATT, D_MODEL), const, pipeline_mode=pl.Buffered(1)),
                  pl.BlockSpec((D_SSD, D_MODEL), const, pipeline_mode=pl.Buffered(1)),
                  pl.BlockSpec((1, D_MODEL), const)],
        out_specs=[pl.BlockSpec((tm, D_MODEL), row), pl.BlockSpec((tm, D_MODEL), row)],
        out_shape=[jax.ShapeDtypeStruct((m, D_MODEL), F32), jax.ShapeDtypeStruct((m, D_MODEL), BF16)],
        compiler_params=_cparams(("arbitrary",)),
        name="out_proj",
    )(x2d, att, y, wa, wy, nw)


def _ffn_kernel(x1_ref, hf_ref, wg_ref, wu_ref, wd_ref, cw_ref, cb_ref, nfw_ref, *rest,
                tm, seq_len, tiles_per_seq):
    if seq_len >= tm:
        y_ref, gt_ref, acc, gprev = rest
    else:
        p1_ref, p2_ref, y_ref, g_ref, acc = rest
    i = pl.program_id(0)
    f = pl.program_id(1)
    nf = pl.num_programs(1)

    @pl.when(f == 0)
    def _():
        acc[...] = jnp.zeros_like(acc)

    hf = hf_ref[...]
    g = jnp.dot(hf, wg_ref[...], preferred_element_type=F32)
    u = jnp.dot(hf, wu_ref[...], preferred_element_type=F32)
    tf = g.shape[1]
    if seq_len >= tm:
        gp = gprev[f]
        gp = jnp.where(i % tiles_per_seq == 0, 0.0, gp)
        row8 = lax.broadcasted_iota(jnp.int32, (SUBLANES, tf), 0)
        shifted = []
        for k in range(1, FFN_CONV):
            sh = pltpu.roll(g, k, axis=0)
            top = jnp.where(row8 < k, pltpu.roll(gp, k, axis=0), sh[:SUBLANES])
            shifted.append(jnp.concatenate([top, sh[SUBLANES:]], axis=0))
        g1, g2 = shifted
        gprev[f] = g[tm - SUBLANES:]
        gt_ref[0] = g[tm - SUBLANES:]
    else:
        pos = lax.broadcasted_iota(jnp.int32, (tm, tf), 0) % seq_len
        g1 = jnp.where(pos >= 1, pltpu.roll(g, 1, axis=0), 0.0) + p1_ref[...]
        g2 = jnp.where(pos >= 2, pltpu.roll(g, 2, axis=0), 0.0) + p2_ref[...]
        g_ref[...] = g
    gc = cb_ref[...] + cw_ref[0:1] * g2 + cw_ref[1:2] * g1 + cw_ref[2:3] * g
    act = (gc * _sigmoid(gc) * u).astype(BF16)
    acc[...] += jnp.dot(act, wd_ref[...], preferred_element_type=F32)

    @pl.when(f == nf - 1)
    def _():
        y_ref[...] = _rms_rows(x1_ref[...] + acc[...], nfw_ref[...])


def _ffn(x1, hf, wg, wu, wd, cw, cb, nfw, p1, p2, *, seq_len):
    m = x1.shape[0]
    tm = min(512, m)
    tf = 512
    nf = D_FF // tf
    prompt_mode = seq_len >= tm
    tiles_per_seq = max(seq_len // tm, 1)
    kern = functools.partial(_ffn_kernel, tm=tm, seq_len=seq_len, tiles_per_seq=tiles_per_seq)
    row = lambda i, f: (i, 0)
    in_specs = [pl.BlockSpec((tm, D_MODEL), row), pl.BlockSpec((tm, D_MODEL), row),
                pl.BlockSpec((D_MODEL, tf), lambda i, f: (0, f)), pl.BlockSpec((D_MODEL, tf), lambda i, f: (0, f)),
                pl.BlockSpec((tf, D_MODEL), lambda i, f: (f, 0)),
                pl.BlockSpec((FFN_CONV, tf), lambda i, f: (0, f)), pl.BlockSpec((1, tf), lambda i, f: (0, f)),
                pl.BlockSpec((1, D_MODEL), lambda i, f: (0, 0))]
    args = [x1, hf, wg, wu, wd, cw, cb, nfw]
    scratch = [pltpu.VMEM((tm, D_MODEL), F32)]
    if prompt_mode:
        out_specs = [pl.BlockSpec((tm, D_MODEL), row), pl.BlockSpec((1, SUBLANES, tf), lambda i, f: (i, 0, f))]
        out_shape = [jax.ShapeDtypeStruct((m, D_MODEL), F32), jax.ShapeDtypeStruct((m // tm, SUBLANES, D_FF), F32)]
        scratch.append(pltpu.VMEM((nf, SUBLANES, tf), F32))
    else:
        in_specs += [pl.BlockSpec((tm, tf), lambda i, f: (i, f)), pl.BlockSpec((tm, tf), lambda i, f: (i, f))]
        args += [p1, p2]
        out_specs = [pl.BlockSpec((tm, D_MODEL), row), pl.BlockSpec((tm, tf), lambda i, f: (i, f))]
        out_shape = [jax.ShapeDtypeStruct((m, D_MODEL), F32), jax.ShapeDtypeStruct((m, D_FF), F32)]
    return pl.pallas_call(
        kern,
        grid=(m // tm, nf),
        in_specs=in_specs,
        out_specs=out_specs,
        out_shape=out_shape,
        scratch_shapes=scratch,
        compiler_params=_cparams(("arbitrary", "arbitrary")),
        name="ffn",
    )(*args)


def _pad_lanes(v, n):
    return jnp.pad(v.reshape(1, -1), ((0, 0), (0, n - v.shape[-1])))


def _layer(xp, xs, past, lam_init, p):
    (cache_k, cache_v, state_ssm, state_conv_ssd, state_conv_ffn, page_table) = past
    bp, sp, _ = xp.shape
    db, ds, _ = xs.shape
    n_pages = page_table.shape[1]
    past_len = n_pages * PAGE_SIZE

    w_in = jnp.pad(p["w_in"].astype(BF16), ((0, 0), (0, IN_PROJ_PAD - p["w_in"].shape[1])))
    nmw = p["norm_mix_w"].reshape(1, D_MODEL)
    lam_vecs = jnp.stack([p["lambda_q1"], p["lambda_k1"], p["lambda_q2"], p["lambda_k2"]])
    subw = p["subln_w"].reshape(1, ATT_DV)
    subw_tiled = jnp.tile(subw, (1, N_ATT_HEADS))
    cw_ssd = p["conv_ssd_w"]
    cb_ssd = p["conv_ssd_b"].reshape(1, CONV_DIM)
    dtb = _pad_lanes(p["dt_bias"], DT_PAD)
    alog = _pad_lanes(p["a_log"], DT_PAD)
    dskip_e = jnp.repeat(p["d_skip"], SSD_HEADDIM).reshape(1, D_SSD)
    nsw = p["norm_ssd_w"].reshape(1, D_SSD)
    w_out = p["w_out"].astype(BF16)
    wa, wy = w_out[:D_ATT], w_out[D_ATT:]
    nfw = p["norm_ffn_w"].reshape(1, D_MODEL)
    wg, wu, wd = p["w_gate"].astype(BF16), p["w_up"].astype(BF16), p["w_down"].astype(BF16)
    cw_ffn = p["conv_ffn_w"]
    cb_ffn = p["conv_ffn_b"].reshape(1, D_FF)
    nfin = p["norm_final_w"].reshape(1, D_MODEL)

    pos_p = jnp.tile(jnp.arange(sp), bp)
    q, k, v, z, xbc, dt = _in_proj(xp.reshape(bp * sp, D_MODEL), nmw, w_in, *_rope_tables(pos_p))
    r3 = lambda t: t.reshape(bp, sp, t.shape[-1])
    att = _attn_prompt(lam_vecs, subw, r3(q), r3(k), r3(v), lam_init)
    rows = SSD_CHUNK if sp % SSD_CHUNK == 0 else sp
    y, ssm_p = _ssd(r3(xbc), jnp.zeros((bp, SUBLANES, CONV_DIM), F32), r3(dt), r3(z),
                    jnp.zeros((bp, SSD_HEADS, SSD_HEADDIM, SSD_STATE), F32),
                    cw_ssd, cb_ssd, dtb, alog, dskip_e, nsw, rows=rows, valid_rows=rows)
    x1, hf = _out_proj(xp.reshape(bp * sp, D_MODEL), att.reshape(bp * sp, D_ATT), y.reshape(bp * sp, D_SSD), wa, wy, nfw)
    yp, gtail = _ffn(x1, hf, wg, wu, wd, cw_ffn, cb_ffn, nfin, None, None, seq_len=sp)
    tiles_per_seq = gtail.shape[0] // bp
    conv_ffn_p = gtail.reshape(bp, tiles_per_seq, SUBLANES, D_FF)[:, -1, SUBLANES - (FFN_CONV - 1):]
    prompt_out = (yp.reshape(bp, sp, D_MODEL),
                  k.reshape(bp, sp, N_ATT_HEADS, 2, ATT_DK), v.reshape(bp, sp, N_ATT_HEADS, ATT_DV),
                  ssm_p, r3(xbc)[:, sp - (SSD_CONV - 1):], conv_ffn_p)

    pos_s = jnp.tile(past_len + jnp.arange(ds), db)
    q, k, v, z, xbc, dt = _in_proj(xs.reshape(db * ds, D_MODEL), nmw, w_in, *_rope_tables(pos_s))
    r3s = lambda t: t.reshape(db, ds, t.shape[-1])
    n_phys = cache_k.shape[0]
    cache_kt = jnp.transpose(cache_k, (0, 2, 3, 4, 1)).reshape(n_phys, Q_DIM, PAGE_SIZE)
    att = _attn_sample(page_table, lam_vecs, subw_tiled, r3s(q), r3s(k), r3s(v),
                       cache_kt, cache_v.reshape(n_phys, PAGE_SIZE, D_ATT), lam_init)
    pad8 = lambda t: jnp.pad(t, ((0, 0), (0, SUBLANES - ds), (0, 0)))
    prev8 = jnp.pad(state_conv_ssd, ((0, 0), (SUBLANES - (SSD_CONV - 1), 0), (0, 0)))
    y8, ssm_s = _ssd(pad8(r3s(xbc)), prev8, pad8(r3s(dt)), pad8(r3s(z)), state_ssm,
                     cw_ssd, cb_ssd, dtb, alog, dskip_e, nsw, rows=SUBLANES, valid_rows=ds)
    y = y8[:, :ds]
    x1, hf = _out_proj(xs.reshape(db * ds, D_MODEL), att.reshape(db * ds, D_ATT), y.reshape(db * ds, D_SSD), wa, wy, nfw)
    zrow = jnp.zeros((db, 1, D_FF), F32)
    st0, st1 = state_conv_ffn[:, 0:1], state_conv_ffn[:, 1:2]
    p1 = jnp.concatenate([st1] + [zrow] * (ds - 1), axis=1).reshape(db * ds, D_FF)
    p2 = jnp.concatenate([st0, st1] + [zrow] * (ds - 2), axis=1).reshape(db * ds, D_FF)
    ys, g_s = _ffn(x1, hf, wg, wu, wd, cw_ffn, cb_ffn, nfin, p1, p2, seq_len=ds)
    sample_out = (ys.reshape(db, ds, D_MODEL),
                  k.reshape(db, ds, N_ATT_HEADS, 2, ATT_DK), v.reshape(db, ds, N_ATT_HEADS, ATT_DV),
                  ssm_s, r3s(xbc)[:, ds - (SSD_CONV - 1):], g_s.reshape(db, ds, D_FF)[:, ds - (FFN_CONV - 1):])
    return prompt_out, sample_out


def kernel(x_prompt, x_sample, cache_k, cache_v, state_ssm, state_conv_ssd, state_conv_ffn, page_table, norm_mix_w, w_in, lambda_q1, lambda_k1, lambda_q2, lambda_k2, subln_w, conv_ssd_w, conv_ssd_b, dt_bias, a_log, d_skip, norm_ssd_w, w_out, norm_ffn_w, w_gate, w_up, conv_ffn_w, conv_ffn_b, w_down, norm_final_w):
    depth = w_in.shape[0]
    assert depth == 1, "the final RMSNorm is fused into the (single) layer's FFN kernel"
    lam_init = 0.8 - 0.6 * math.exp(-0.3 * 0)
    params = dict(norm_mix_w=norm_mix_w[0], w_in=w_in[0], lambda_q1=lambda_q1[0], lambda_k1=lambda_k1[0],
                  lambda_q2=lambda_q2[0], lambda_k2=lambda_k2[0], subln_w=subln_w[0], conv_ssd_w=conv_ssd_w[0],
                  conv_ssd_b=conv_ssd_b[0], dt_bias=dt_bias[0], a_log=a_log[0], d_skip=d_skip[0],
                  norm_ssd_w=norm_ssd_w[0], w_out=w_out[0], norm_ffn_w=norm_ffn_w[0], w_gate=w_gate[0],
                  w_up=w_up[0], conv_ffn_w=conv_ffn_w[0], conv_ffn_b=conv_ffn_b[0], w_down=w_down[0],
                  norm_final_w=norm_final_w)
    past = (cache_k[0], cache_v[0], state_ssm[0], state_conv_ssd[0], state_conv_ffn[0], page_table)
    (yp, kp, vp, sp_, cp, fp), (ys, ks, vs, ss, cs, fs) = _layer(x_prompt, x_sample, past, lam_init, params)
    lead = lambda t: t[None]
    return (yp, ys, lead(kp), lead(vp), lead(sp_), lead(cp), lead(fp),
            lead(ks), lead(vs), lead(ss), lead(cs), lead(fs))
```

```python
import functools
import math

import jax
import jax.numpy as jnp
from jax import lax
from jax.experimental import pallas as pl
from jax.experimental.pallas import tpu as pltpu

F32 = jnp.float32
BF16 = jnp.bfloat16

D_MODEL = 2048
ATT_DK = 64
ATT_DV = 128
N_ATT_HEADS = 8
ROT_DIM = 16
ROPE_THETA = 500000.0
Q_DIM = N_ATT_HEADS * 2 * ATT_DK
D_ATT = N_ATT_HEADS * ATT_DV
D_SSD = 1024
SSD_HEADDIM = 64
SSD_HEADS = 16
SSD_GROUPS = 2
SSD_STATE = 128
SSD_CONV = 4
SSD_CHUNK = 128
CONV_DIM = D_SSD + 2 * SSD_GROUPS * SSD_STATE
D_FF = 5632
FFN_CONV = 3
EPS = 1e-6
PAGE_SIZE = 128

LANES = 128
SUBLANES = 8
VMEM_LIMIT_BYTES = 56 * 1024 * 1024
IN_PROJ_DIM = 2 * Q_DIM + D_ATT + D_SSD + CONV_DIM + SSD_HEADS
PROJ_CHUNK = 512
NEG_INF = float("-inf")


def _cparams(sem):
    return pltpu.CompilerParams(dimension_semantics=sem, vmem_limit_bytes=VMEM_LIMIT_BYTES)


def _sigmoid(x):
    return 1.0 / (1.0 + jnp.exp(-x))


def _rms_rows(x, w):
    return x * lax.rsqrt(jnp.mean(x * x, axis=-1, keepdims=True) + EPS) * w


def _inproj_kernel(x_ref, nw_ref, w_ref, cos_ref, sa_ref, sb_ref,
                   q_ref, k_ref, v_ref, z_ref, xbc_ref, dt_ref):
    xn = _rms_rows(x_ref[...], nw_ref[...]).astype(BF16)
    cos, sa, sb = cos_ref[...], sa_ref[...], sb_ref[...]

    def rope(p):
        up = pltpu.roll(p, LANES - ROT_DIM // 2, axis=1)
        dn = pltpu.roll(p, ROT_DIM // 2, axis=1)
        return p * cos + up * sa + dn * sb

    col = 0
    for ref, width, rot in ((q_ref, Q_DIM, True), (k_ref, Q_DIM, True), (v_ref, D_ATT, False),
                            (z_ref, D_SSD, False), (xbc_ref, CONV_DIM, False), (dt_ref, SSD_HEADS, False)):
        for c0 in range(0, width, PROJ_CHUNK):
            cw = min(PROJ_CHUNK, width - c0)
            p = jnp.dot(xn, w_ref[:, col + c0:col + c0 + cw], preferred_element_type=F32)
            if rot:
                for s in range(0, cw, LANES):
                    ref[:, c0 + s:c0 + s + LANES] = rope(p[:, s:s + LANES])
            else:
                ref[:, c0:c0 + cw] = p
        col += width


def _in_proj_tile(m):
    return min(256, m)


def _in_proj(x2d, nw, w_bf, cos_t, sa_t, sb_t):
    m = x2d.shape[0]
    tm = _in_proj_tile(m)
    table_tiles = cos_t.shape[0] // tm
    row = lambda i: (i, 0)
    trow = lambda i: (i % table_tiles, 0)
    const = lambda i: (0, 0)
    widths = (Q_DIM, Q_DIM, D_ATT, D_SSD, CONV_DIM, SSD_HEADS)
    return pl.pallas_call(
        _inproj_kernel,
        grid=(m // tm,),
        in_specs=[pl.BlockSpec((tm, D_MODEL), row),
                  pl.BlockSpec((1, D_MODEL), const),
                  pl.BlockSpec((D_MODEL, IN_PROJ_DIM), const, pipeline_mode=pl.Buffered(1)),
                  pl.BlockSpec((tm, LANES), trow), pl.BlockSpec((tm, LANES), trow), pl.BlockSpec((tm, LANES), trow)],
        out_specs=[pl.BlockSpec((tm, w), row) for w in widths],
        out_shape=[jax.ShapeDtypeStruct((m, w), F32) for w in widths],
        compiler_params=_cparams(("arbitrary",)),
        name="in_proj",
    )(x2d, nw, w_bf, cos_t, sa_t, sb_t)


def _rope_tables(pos):
    half = ROT_DIM // 2
    inv = ROPE_THETA ** (-jnp.arange(half, dtype=F32) / half)
    ang = pos.astype(F32)[:, None] * inv[None, :]
    cos, sin = jnp.cos(ang), jnp.sin(ang)
    rows = pos.shape[0]
    ones = jnp.ones((rows, ATT_DK - ROT_DIM), F32)
    zeros = jnp.zeros((rows, ATT_DK - ROT_DIM), F32)
    zh = jnp.zeros((rows, half), F32)
    cos64 = jnp.concatenate([cos, cos, ones], axis=1)
    sa64 = jnp.concatenate([-sin, zh, zeros], axis=1)
    sb64 = jnp.concatenate([zh, sin, zeros], axis=1)
    two = lambda t: jnp.concatenate([t, t], axis=1)
    return two(cos64), two(sa64), two(sb64)


def _lambda_full(lam_ref, lam_init):
    l = lam_ref[...]
    s1 = jnp.sum(l[0:1] * l[1:2], axis=1, keepdims=True)
    s2 = jnp.sum(l[2:3] * l[3:4], axis=1, keepdims=True)
    return jnp.exp(s1) - jnp.exp(s2) + lam_init


def _attn_prompt_kernel(lam_ref, subw_ref, q_ref, k_ref, v_ref, o_ref, kbf, vt, s_scr, p_scr, *, tq, lam_init):
    s_len = q_ref.shape[1]
    lam = _lambda_full(lam_ref, lam_init)
    kbf[...] = k_ref[0].astype(BF16)
    for c in range(s_len // tq):
        vt[:, c * tq:(c + 1) * tq] = v_ref[0, c * tq:(c + 1) * tq, :].T.astype(BF16)
    lane = lax.broadcasted_iota(jnp.int32, (tq, LANES), 1)
    krow = lax.broadcasted_iota(jnp.int32, (tq, 2 * tq), 0)
    qcol = lax.broadcasted_iota(jnp.int32, (tq, 2 * tq), 1)
    diag = krow <= jnp.where(qcol >= tq, qcol - tq, qcol)
    contract_last = (((1,), (1,)), ((), ()))

    for qi in range(s_len // tq):
        q = q_ref[0, qi * tq:(qi + 1) * tq, :] * (ATT_DK ** -0.5)
        qq = jnp.concatenate([jnp.where(lane < ATT_DK, q, 0.0), jnp.where(lane >= ATT_DK, q, 0.0)],
                             axis=0).astype(BF16)
        m = None
        for c in range(qi + 1):
            s = lax.dot_general(kbf[c * tq:(c + 1) * tq, :], qq, contract_last, preferred_element_type=F32)
            if c == qi:
                s = jnp.where(diag, s, NEG_INF)
            s_scr[c * tq:(c + 1) * tq, :] = s
            cm = jnp.max(s, axis=0, keepdims=True)
            m = cm if m is None else jnp.maximum(m, cm)
        l = jnp.zeros((1, 2 * tq), F32)
        for c in range(qi + 1):
            p = jnp.exp(s_scr[c * tq:(c + 1) * tq, :] - m)
            l = l + jnp.sum(p, axis=0, keepdims=True)
            p_scr[c * tq:(c + 1) * tq, :] = p.astype(BF16)
        kv = (qi + 1) * tq
        o = jnp.dot(vt[:, :kv], p_scr[:kv, :], preferred_element_type=F32) / l
        att = o[:, :tq] - lam * o[:, tq:]
        ms = jnp.mean(att * att, axis=0, keepdims=True)
        y = att * lax.rsqrt(ms + EPS) * subw_ref[...] * (1.0 - lam_init)
        o_ref[0, qi * tq:(qi + 1) * tq, :] = y.T


def _attn_prompt(lam_vecs, subw_col, q, k, v, lam_init):
    b, s, _ = q.shape
    tq = min(256, s)
    kern = functools.partial(_attn_prompt_kernel, tq=tq, lam_init=lam_init)
    head = pl.BlockSpec((1, s, LANES), lambda bb, h: (bb, 0, h))
    return pl.pallas_call(
        kern,
        grid=(b, N_ATT_HEADS),
        in_specs=[pl.BlockSpec((4, ATT_DK), lambda bb, h: (0, 0)),
                  pl.BlockSpec((ATT_DV, 1), lambda bb, h: (0, 0)),
                  head, head, head],
        out_specs=head,
        out_shape=jax.ShapeDtypeStruct((b, s, D_ATT), F32),
        scratch_shapes=[pltpu.VMEM((s, LANES), BF16), pltpu.VMEM((ATT_DV, s), BF16),
                        pltpu.VMEM((s, 2 * tq), F32), pltpu.VMEM((s, 2 * tq), BF16)],
        compiler_params=_cparams(("arbitrary", "arbitrary")),
        name="attn_prompt",
    )(lam_vecs, subw_col, q, k, v)


def _attn_sample_kernel(pt_ref, lam_ref, subw_ref, q_ref, kn_ref, vn_ref, *rest, n_pages, t_new, lam_init):
    kpages = rest[:n_pages]
    vpages = rest[n_pages:2 * n_pages]
    o_ref, kbf, vbf = rest[2 * n_pages:]
    past = n_pages * PAGE_SIZE
    n_keys = past + PAGE_SIZE
    for j in range(n_pages):
        kbf[:, j * PAGE_SIZE:(j + 1) * PAGE_SIZE] = kpages[j][0].astype(BF16)
        for h in range(N_ATT_HEADS):
            vh = vpages[j][0, pl.ds(h, PAGE_SIZE, stride=N_ATT_HEADS), :]
            vbf[j * PAGE_SIZE:(j + 1) * PAGE_SIZE, h * ATT_DV:(h + 1) * ATT_DV] = vh.astype(BF16)
    zpad = jnp.zeros((PAGE_SIZE - t_new, Q_DIM), F32)
    k_tail = jnp.concatenate([kn_ref[0], zpad], axis=0).astype(BF16)
    vbf[past:n_keys, :] = jnp.concatenate([vn_ref[0], zpad], axis=0).astype(BF16)

    lam = _lambda_full(lam_ref, lam_init)
    q = q_ref[0] * (ATT_DK ** -0.5)
    nr = t_new * N_ATT_HEADS
    qrep = jnp.concatenate([jnp.broadcast_to(q[t:t + 1], (N_ATT_HEADS, Q_DIM)) for t in range(t_new)], axis=0)
    row = lax.broadcasted_iota(jnp.int32, (nr, Q_DIM), 0)
    lane = lax.broadcasted_iota(jnp.int32, (nr, Q_DIM), 1)
    head = row % N_ATT_HEADS
    grp = lane // ATT_DK
    qq = jnp.concatenate([jnp.where(grp == 2 * head, qrep, 0.0), jnp.where(grp == 2 * head + 1, qrep, 0.0)],
                         axis=0).astype(BF16)
    s_past = jnp.dot(qq, kbf[...], preferred_element_type=F32)
    s_tail = lax.dot_general(qq, k_tail, (((1,), (1,)), ((), ())), preferred_element_type=F32)
    s = jnp.concatenate([s_past, s_tail], axis=1)
    srow = lax.broadcasted_iota(jnp.int32, (2 * nr, n_keys), 0)
    scol = lax.broadcasted_iota(jnp.int32, (2 * nr, n_keys), 1)
    tq = (srow // N_ATT_HEADS) % t_new
    s = jnp.where(scol <= past + tq, s, NEG_INF)
    m = jnp.max(s, axis=1, keepdims=True)
    p = jnp.exp(s - m)
    pn = p / jnp.sum(p, axis=1, keepdims=True)
    a = pn[:nr] - lam * pn[nr:]
    o = jnp.dot(a.astype(BF16), vbf[...], preferred_element_type=F32)
    o = jnp.where(lane // ATT_DV == head, o, 0.0)
    ms = jnp.sum(o * o, axis=1, keepdims=True) * (1.0 / ATT_DV)
    y = o * lax.rsqrt(ms + EPS)
    att = jnp.concatenate([jnp.sum(y[t * N_ATT_HEADS:(t + 1) * N_ATT_HEADS], axis=0, keepdims=True)
                           for t in range(t_new)], axis=0)
    o_ref[0] = att * subw_ref[...] * (1.0 - lam_init)


def _attn_sample(page_table, lam_vecs, subw_tiled, q, kn, vn, cache_k, cache_v, lam_init):
    db, t_new, _ = q.shape
    n_pages = page_table.shape[1]
    n_keys = (n_pages + 1) * PAGE_SIZE
    kern = functools.partial(_attn_sample_kernel, n_pages=n_pages, t_new=t_new, lam_init=lam_init)
    tok = pl.BlockSpec((1, t_new, Q_DIM), lambda b, pt: (b, 0, 0))
    page_maps = [functools.partial(lambda b, pt, j: (pt[b, j], 0, 0), j=j) for j in range(n_pages)]
    kpage_specs = [pl.BlockSpec((1, Q_DIM, PAGE_SIZE), pm) for pm in page_maps]
    vpage_specs = [pl.BlockSpec((1, PAGE_SIZE * N_ATT_HEADS, ATT_DV), pm) for pm in page_maps]
    grid_spec = pltpu.PrefetchScalarGridSpec(
        num_scalar_prefetch=1,
        grid=(db,),
        in_specs=[pl.BlockSpec((4, ATT_DK), lambda b, pt: (0, 0)),
                  pl.BlockSpec((1, D_ATT), lambda b, pt: (0, 0)),
                  tok, tok, tok] + kpage_specs + vpage_specs,
        out_specs=tok,
        scratch_shapes=[pltpu.VMEM((Q_DIM, n_keys - PAGE_SIZE), BF16), pltpu.VMEM((n_keys, D_ATT), BF16)],
    )
    return pl.pallas_call(
        kern,
        grid_spec=grid_spec,
        out_shape=jax.ShapeDtypeStruct((db, t_new, D_ATT), F32),
        compiler_params=_cparams(("arbitrary",)),
        name="attn_sample",
    )(page_table, lam_vecs, subw_tiled, q, kn, vn, *([cache_k] * n_pages), *([cache_v] * n_pages))


def _expand_heads(v, rows):
    return jnp.concatenate([jnp.broadcast_to(v[:, h:h + 1], (rows, SSD_HEADDIM)) for h in range(SSD_HEADS)], axis=1)


def _ssd_kernel(xbc_ref, prev_ref, dt_ref, z_ref, h0_ref, cw_ref, cb_ref, dtb_ref, alog_ref, dskip_ref, nw_ref,
                y_ref, hout_ref, xprev, *, rows, valid_rows, carry):
    c = pl.program_id(1)

    @pl.when(c == 0)
    def _():
        hout_ref[...] = h0_ref[...]
        xprev[...] = prev_ref[0]

    xr = xbc_ref[0]
    pv = xprev[...]
    row8 = lax.broadcasted_iota(jnp.int32, (SUBLANES, CONV_DIM), 0)
    conv = cb_ref[...] + cw_ref[SSD_CONV - 1:SSD_CONV] * xr
    for k in range(1, SSD_CONV):
        sh = pltpu.roll(xr, k, axis=0)
        top = jnp.where(row8 < k, pltpu.roll(pv, k, axis=0), sh[:SUBLANES])
        sh = top if rows == SUBLANES else jnp.concatenate([top, sh[SUBLANES:]], axis=0)
        conv = conv + cw_ref[SSD_CONV - 1 - k:SSD_CONV - k] * sh
    if carry:
        xprev[...] = xr[rows - SUBLANES:]
    xc = conv * _sigmoid(conv)
    xs = xc[:, :D_SSD]

    dtv = dt_ref[0] + dtb_ref[...]
    dtv = jnp.maximum(dtv, 0.0) + jnp.log1p(jnp.exp(-jnp.abs(dtv)))
    if valid_rows < rows:
        rvalid = lax.broadcasted_iota(jnp.int32, (rows, SSD_HEADS), 0) < valid_rows
        dtv = jnp.where(rvalid, dtv, 0.0)
    a = dtv * (-jnp.exp(alog_ref[...]))
    ri = lax.broadcasted_iota(jnp.int32, (rows, rows), 0)
    ci = lax.broadcasted_iota(jnp.int32, (rows, rows), 1)
    tril = ri >= ci
    a_cs = jnp.dot(tril.astype(F32), a, preferred_element_type=F32, precision=lax.Precision.HIGHEST)
    total = a_cs[rows - 1:rows]
    e_cs = jnp.exp(a_cs)
    e_tot = jnp.exp(total)
    xdt = xs * _expand_heads(dtv, rows)
    xdtd = xdt * _expand_heads(jnp.exp(total - a_cs), rows)
    e_exp = _expand_heads(e_cs, rows)
    eye = ri == ci

    y_diag_parts, y_off_parts = [], []
    heads_per_group = SSD_HEADS // SSD_GROUPS
    gw = heads_per_group * SSD_HEADDIM
    for g in range(SSD_GROUPS):
        bg = xc[:, D_SSD + g * SSD_STATE:D_SSD + (g + 1) * SSD_STATE].astype(BF16)
        cg = xc[:, D_SSD + (SSD_GROUPS + g) * SSD_STATE:D_SSD + (SSD_GROUPS + g + 1) * SSD_STATE].astype(BF16)
        hg = hout_ref[0, g * heads_per_group:(g + 1) * heads_per_group].reshape(gw, SSD_STATE)
        y_off = lax.dot_general(cg, hg.astype(BF16), (((1,), (1,)), ((), ())), preferred_element_type=F32)
        st = lax.dot_general(xdtd[:, g * gw:(g + 1) * gw].astype(BF16), bg, (((0,), (0,)), ((), ())),
                             preferred_element_type=F32)
        cbm = lax.dot_general(cg, bg, (((1,), (1,)), ((), ())), preferred_element_type=F32)
        for r in range(heads_per_group):
            h = g * heads_per_group + r
            col = a_cs[:, h:h + 1]
            rowv = jnp.sum(jnp.where(eye, col, 0.0), axis=0, keepdims=True)
            lmat = jnp.where(tril, jnp.exp(col - rowv), 0.0)
            mm = (cbm * lmat).astype(BF16)
            y_diag_parts.append(jnp.dot(mm, xdt[:, h * SSD_HEADDIM:(h + 1) * SSD_HEADDIM].astype(BF16),
                                        preferred_element_type=F32))
            hh = hout_ref[0, h]
            hout_ref[0, h] = hh * e_tot[:, h:h + 1] + st[r * SSD_HEADDIM:(r + 1) * SSD_HEADDIM]
        y_off_parts.append(y_off)
    y = (jnp.concatenate(y_diag_parts, axis=1) + jnp.concatenate(y_off_parts, axis=1) * e_exp
         + _expand_heads(dskip_ref[...], 1) * xs)
    zz = z_ref[0]
    y = y * (zz * _sigmoid(zz))
    half = D_SSD // SSD_GROUPS
    nw = nw_ref[...]
    y_ref[0] = jnp.concatenate([_rms_rows(y[:, g * half:(g + 1) * half], nw[:, g * half:(g + 1) * half])
                                for g in range(SSD_GROUPS)], axis=1)


def _ssd(xbc, prev8, dt, z, h0, cw, cb, dtb, alog, dskip, nw, *, rows, valid_rows):
    b, s, _ = xbc.shape
    n_chunks = s // rows
    kern = functools.partial(_ssd_kernel, rows=rows, valid_rows=valid_rows, carry=n_chunks > 1)
    tokmap = lambda bb, c: (bb, c, 0)
    seqmap3 = lambda bb, c: (bb, 0, 0)
    seqmap4 = lambda bb, c: (bb, 0, 0, 0)
    const = lambda bb, c: (0, 0)
    state_spec = pl.BlockSpec((1, SSD_HEADS, SSD_HEADDIM, SSD_STATE), seqmap4)
    return pl.pallas_call(
        kern,
        grid=(b, n_chunks),
        in_specs=[pl.BlockSpec((1, rows, CONV_DIM), tokmap),
                  pl.BlockSpec((1, SUBLANES, CONV_DIM), seqmap3),
                  pl.BlockSpec((1, rows, SSD_HEADS), tokmap),
                  pl.BlockSpec((1, rows, D_SSD), tokmap),
                  state_spec,
                  pl.BlockSpec((SSD_CONV, CONV_DIM), const),
                  pl.BlockSpec((1, CONV_DIM), const),
                  pl.BlockSpec((1, SSD_HEADS), const),
                  pl.BlockSpec((1, SSD_HEADS), const),
                  pl.BlockSpec((1, SSD_HEADS), const),
                  pl.BlockSpec((1, D_SSD), const)],
        out_specs=[pl.BlockSpec((1, rows, D_SSD), tokmap), state_spec],
        out_shape=[jax.ShapeDtypeStruct((b, s, D_SSD), F32),
                   jax.ShapeDtypeStruct((b, SSD_HEADS, SSD_HEADDIM, SSD_STATE), F32)],
        scratch_shapes=[pltpu.VMEM((SUBLANES, CONV_DIM), F32)],
        compiler_params=_cparams(("arbitrary", "arbitrary")),
        name="ssd",
    )(xbc, prev8, dt, z, h0, cw, cb, dtb, alog, dskip, nw)


def _outproj_kernel(x_ref, att_ref, y_ref, wa_ref, wy_ref, nw_ref, x1_ref, hf_ref):
    mix = jnp.dot(att_ref[...].astype(BF16), wa_ref[...], preferred_element_type=F32)
    mix = mix + jnp.dot(y_ref[...].astype(BF16), wy_ref[...], preferred_element_type=F32)
    x1 = x_ref[...] + mix
    x1_ref[...] = x1
    hf_ref[...] = _rms_rows(x1, nw_ref[...]).astype(BF16)


def _out_proj(x2d, att, y, wa, wy, nw):
    m = x2d.shape[0]
    tm = min(512, m)
    row = lambda i: (i, 0)
    const = lambda i: (0, 0)
    return pl.pallas_call(
        _outproj_kernel,
        grid=(m // tm,),
        in_specs=[pl.BlockSpec((tm, D_MODEL), row), pl.BlockSpec((tm, D_ATT), row), pl.BlockSpec((tm, D_SSD), row),
                  pl.BlockSpec((D_ATT, D_MODEL), const, pipeline_mode=pl.Buffered(1)),
                  pl.BlockSpec((D_SSD, D_MODEL), const, pipeline_mode=pl.Buffered(1)),
                  pl.BlockSpec((1, D_MODEL), const)],
        out_specs=[pl.BlockSpec((tm, D_MODEL), row), pl.BlockSpec((tm, D_MODEL), row)],
        out_shape=[jax.ShapeDtypeStruct((m, D_MODEL), F32), jax.ShapeDtypeStruct((m, D_MODEL), BF16)],
        compiler_params=_cparams(("arbitrary",)),
        name="out_proj",
    )(x2d, att, y, wa, wy, nw)


def _ffn_kernel(x1_ref, hf_ref, wg_ref, wu_ref, wd_ref, cw_ref, cb_ref, nfw_ref, *rest,
                tm, seq_len, tiles_per_seq):
    if seq_len >= tm:
        y_ref, gt_ref, acc, gprev = rest
    else:
        p1_ref, p2_ref, y_ref, g_ref, acc = rest
    i = pl.program_id(0)
    f = pl.program_id(1)
    nf = pl.num_programs(1)

    @pl.when(f == 0)
    def _():
        acc[...] = jnp.zeros_like(acc)

    hf = hf_ref[...]
    g = jnp.dot(hf, wg_ref[...], preferred_element_type=F32)
    u = jnp.dot(hf, wu_ref[...], preferred_element_type=F32)
    tf = g.shape[1]
    if seq_len >= tm:
        gp = gprev[f]
        gp = jnp.where(i % tiles_per_seq == 0, 0.0, gp)
        row8 = lax.broadcasted_iota(jnp.int32, (SUBLANES, tf), 0)
        shifted = []
        for k in range(1, FFN_CONV):
            sh = pltpu.roll(g, k, axis=0)
            top = jnp.where(row8 < k, pltpu.roll(gp, k, axis=0), sh[:SUBLANES])
            shifted.append(jnp.concatenate([top, sh[SUBLANES:]], axis=0))
        g1, g2 = shifted
        gprev[f] = g[tm - SUBLANES:]
        gt_ref[0] = g[tm - SUBLANES:]
    else:
        pos = lax.broadcasted_iota(jnp.int32, (tm, tf), 0) % seq_len
        g1 = jnp.where(pos >= 1, pltpu.roll(g, 1, axis=0), 0.0) + p1_ref[...]
        g2 = jnp.where(pos >= 2, pltpu.roll(g, 2, axis=0), 0.0) + p2_ref[...]
        g_ref[...] = g
    gc = cb_ref[...] + cw_ref[0:1] * g2 + cw_ref[1:2] * g1 + cw_ref[2:3] * g
    act = (gc * _sigmoid(gc) * u).astype(BF16)
    acc[...] += jnp.dot(act, wd_ref[...], preferred_element_type=F32)

    @pl.when(f == nf - 1)
    def _():
        y_ref[...] = _rms_rows(x1_ref[...] + acc[...], nfw_ref[...])


def _ffn(x1, hf, wg, wu, wd, cw, cb, nfw, p1, p2, *, seq_len):
    m = x1.shape[0]
    tm = min(512, m)
    tf = 512
    nf = D_FF // tf
    prompt_mode = seq_len >= tm
    tiles_per_seq = max(seq_len // tm, 1)
    kern = functools.partial(_ffn_kernel, tm=tm, seq_len=seq_len, tiles_per_seq=tiles_per_seq)
    row = lambda i, f: (i, 0)
    in_specs = [pl.BlockSpec((tm, D_MODEL), row), pl.BlockSpec((tm, D_MODEL), row),
                pl.BlockSpec((D_MODEL, tf), lambda i, f: (0, f)), pl.BlockSpec((D_MODEL, tf), lambda i, f: (0, f)),
                pl.BlockSpec((tf, D_MODEL), lambda i, f: (f, 0)),
                pl.BlockSpec((FFN_CONV, tf), lambda i, f: (0, f)), pl.BlockSpec((1, tf), lambda i, f: (0, f)),
                pl.BlockSpec((1, D_MODEL), lambda i, f: (0, 0))]
    args = [x1, hf, wg, wu, wd, cw, cb, nfw]
    scratch = [pltpu.VMEM((tm, D_MODEL), F32)]
    if prompt_mode:
        out_specs = [pl.BlockSpec((tm, D_MODEL), row), pl.BlockSpec((1, SUBLANES, tf), lambda i, f: (i, 0, f))]
        out_shape = [jax.ShapeDtypeStruct((m, D_MODEL), F32), jax.ShapeDtypeStruct((m // tm, SUBLANES, D_FF), F32)]
        scratch.append(pltpu.VMEM((nf, SUBLANES, tf), F32))
    else:
        in_specs += [pl.BlockSpec((tm, tf), lambda i, f: (i, f)), pl.BlockSpec((tm, tf), lambda i, f: (i, f))]
        args += [p1, p2]
        out_specs = [pl.BlockSpec((tm, D_MODEL), row), pl.BlockSpec((tm, tf), lambda i, f: (i, f))]
        out_shape = [jax.ShapeDtypeStruct((m, D_MODEL), F32), jax.ShapeDtypeStruct((m, D_FF), F32)]
    return pl.pallas_call(
        kern,
        grid=(m // tm, nf),
        in_specs=in_specs,
        out_specs=out_specs,
        out_shape=out_shape,
        scratch_shapes=scratch,
        compiler_params=_cparams(("arbitrary", "arbitrary")),
        name="ffn",
    )(*args)


def _layer(xp, xs, past, lam_init, p):
    (cache_k, cache_v, state_ssm, state_conv_ssd, state_conv_ffn, page_table) = past
    bp, sp, _ = xp.shape
    db, ds, _ = xs.shape
    n_pages = page_table.shape[1]
    past_len = n_pages * PAGE_SIZE

    w_in = p["w_in"].astype(BF16)
    nmw = p["norm_mix_w"].reshape(1, D_MODEL)
    lam_vecs = jnp.stack([p["lambda_q1"], p["lambda_k1"], p["lambda_q2"], p["lambda_k2"]])
    subw = p["subln_w"].reshape(1, ATT_DV)
    subw_tiled = jnp.tile(subw, (1, N_ATT_HEADS))
    cw_ssd = p["conv_ssd_w"]
    cb_ssd = p["conv_ssd_b"].reshape(1, CONV_DIM)
    dtb = p["dt_bias"].reshape(1, SSD_HEADS)
    alog = p["a_log"].reshape(1, SSD_HEADS)
    dskip = p["d_skip"].reshape(1, SSD_HEADS)
    nsw = p["norm_ssd_w"].reshape(1, D_SSD)
    w_out = p["w_out"].astype(BF16)
    wa, wy = w_out[:D_ATT], w_out[D_ATT:]
    nfw = p["norm_ffn_w"].reshape(1, D_MODEL)
    wg, wu, wd = p["w_gate"].astype(BF16), p["w_up"].astype(BF16), p["w_down"].astype(BF16)
    cw_ffn = p["conv_ffn_w"]
    cb_ffn = p["conv_ffn_b"].reshape(1, D_FF)
    nfin = p["norm_final_w"].reshape(1, D_MODEL)

    q, k, v, z, xbc, dt = _in_proj(xp.reshape(bp * sp, D_MODEL), nmw, w_in, *_rope_tables(jnp.arange(sp)))
    r3 = lambda t: t.reshape(bp, sp, t.shape[-1])
    att = _attn_prompt(lam_vecs, subw.reshape(ATT_DV, 1), r3(q), r3(k), r3(v), lam_init)
    rows = SSD_CHUNK if sp % SSD_CHUNK == 0 else sp
    y, ssm_p = _ssd(r3(xbc), jnp.zeros((bp, SUBLANES, CONV_DIM), F32), r3(dt), r3(z),
                    jnp.zeros((bp, SSD_HEADS, SSD_HEADDIM, SSD_STATE), F32),
                    cw_ssd, cb_ssd, dtb, alog, dskip, nsw, rows=rows, valid_rows=rows)
    x1, hf = _out_proj(xp.reshape(bp * sp, D_MODEL), att.reshape(bp * sp, D_ATT), y.reshape(bp * sp, D_SSD), wa, wy, nfw)
    yp, gtail = _ffn(x1, hf, wg, wu, wd, cw_ffn, cb_ffn, nfin, None, None, seq_len=sp)
    tiles_per_seq = gtail.shape[0] // bp
    conv_ffn_p = gtail.reshape(bp, tiles_per_seq, SUBLANES, D_FF)[:, -1, SUBLANES - (FFN_CONV - 1):]
    prompt_out = (yp.reshape(bp, sp, D_MODEL),
                  k.reshape(bp, sp, N_ATT_HEADS, 2, ATT_DK), v.reshape(bp, sp, N_ATT_HEADS, ATT_DV),
                  ssm_p, r3(xbc)[:, sp - (SSD_CONV - 1):], conv_ffn_p)

    pos_s = jnp.tile(past_len + jnp.arange(ds), _in_proj_tile(db * ds) // ds)
    q, k, v, z, xbc, dt = _in_proj(xs.reshape(db * ds, D_MODEL), nmw, w_in, *_rope_tables(pos_s))
    r3s = lambda t: t.reshape(db, ds, t.shape[-1])
    n_phys = cache_k.shape[0]
    cache_kt = jnp.transpose(cache_k, (0, 2, 3, 4, 1)).reshape(n_phys, Q_DIM, PAGE_SIZE)
    att = _attn_sample(page_table, lam_vecs, subw_tiled, r3s(q), r3s(k), r3s(v),
                       cache_kt, cache_v.reshape(n_phys, PAGE_SIZE * N_ATT_HEADS, ATT_DV), lam_init)
    pad8 = lambda t: jnp.pad(t, ((0, 0), (0, SUBLANES - ds), (0, 0)))
    prev8 = jnp.pad(state_conv_ssd, ((0, 0), (SUBLANES - (SSD_CONV - 1), 0), (0, 0)))
    y8, ssm_s = _ssd(pad8(r3s(xbc)), prev8, pad8(r3s(dt)), pad8(r3s(z)), state_ssm,
                     cw_ssd, cb_ssd, dtb, alog, dskip, nsw, rows=SUBLANES, valid_rows=ds)
    y = y8[:, :ds]
    x1, hf = _out_proj(xs.reshape(db * ds, D_MODEL), att.reshape(db * ds, D_ATT), y.reshape(db * ds, D_SSD), wa, wy, nfw)
    zrow = jnp.zeros((db, 1, D_FF), F32)
    st0, st1 = state_conv_ffn[:, 0:1], state_conv_ffn[:, 1:2]
    p1 = jnp.concatenate([st1] + [zrow] * (ds - 1), axis=1).reshape(db * ds, D_FF)
    p2 = jnp.concatenate([st0, st1] + [zrow] * (ds - 2), axis=1).reshape(db * ds, D_FF)
    ys, g_s = _ffn(x1, hf, wg, wu, wd, cw_ffn, cb_ffn, nfin, p1, p2, seq_len=ds)
    sample_out = (ys.reshape(db, ds, D_MODEL),
                  k.reshape(db, ds, N_ATT_HEADS, 2, ATT_DK), v.reshape(db, ds, N_ATT_HEADS, ATT_DV),
                  ssm_s, r3s(xbc)[:, ds - (SSD_CONV - 1):], g_s.reshape(db, ds, D_FF)[:, ds - (FFN_CONV - 1):])
    return prompt_out, sample_out


def kernel(x_prompt, x_sample, cache_k, cache_v, state_ssm, state_conv_ssd, state_conv_ffn, page_table, norm_mix_w, w_in, lambda_q1, lambda_k1, lambda_q2, lambda_k2, subln_w, conv_ssd_w, conv_ssd_b, dt_bias, a_log, d_skip, norm_ssd_w, w_out, norm_ffn_w, w_gate, w_up, conv_ffn_w, conv_ffn_b, w_down, norm_final_w):
    depth = w_in.shape[0]
    assert depth == 1, "the final RMSNorm is fused into the (single) layer's FFN kernel"
    lam_init = 0.8 - 0.6 * math.exp(-0.3 * 0)
    params = dict(norm_mix_w=norm_mix_w[0], w_in=w_in[0], lambda_q1=lambda_q1[0], lambda_k1=lambda_k1[0],
                  lambda_q2=lambda_q2[0], lambda_k2=lambda_k2[0], subln_w=subln_w[0], conv_ssd_w=conv_ssd_w[0],
                  conv_ssd_b=conv_ssd_b[0], dt_bias=dt_bias[0], a_log=a_log[0], d_skip=d_skip[0],
                  norm_ssd_w=norm_ssd_w[0], w_out=w_out[0], norm_ffn_w=norm_ffn_w[0], w_gate=w_gate[0],
                  w_up=w_up[0], conv_ffn_w=conv_ffn_w[0], conv_ffn_b=conv_ffn_b[0], w_down=w_down[0],
                  norm_final_w=norm_final_w)
    past = (cache_k[0], cache_v[0], state_ssm[0], state_conv_ssd[0], state_conv_ffn[0], page_table)
    (yp, kp, vp, sp_, cp, fp), (ys, ks, vs, ss, cs, fs) = _layer(x_prompt, x_sample, past, lam_init, params)
    lead = lambda t: t[None]
    return (yp, ys, lead(kp), lead(vp), lead(sp_), lead(cp), lead(fp),
            lead(ks), lead(vs), lead(ss), lead(cs), lead(fs))
```

```python
import functools
import math

import jax
import jax.numpy as jnp
from jax import lax
from jax.experimental import pallas as pl
from jax.experimental.pallas import tpu as pltpu

F32 = jnp.float32
BF16 = jnp.bfloat16

D_MODEL = 2048
ATT_DK = 64
ATT_DV = 128
N_ATT_HEADS = 8
ROT_DIM = 16
ROPE_THETA = 500000.0
Q_DIM = N_ATT_HEADS * 2 * ATT_DK
D_ATT = N_ATT_HEADS * ATT_DV
D_SSD = 1024
SSD_HEADDIM = 64
SSD_HEADS = 16
SSD_GROUPS = 2
SSD_STATE = 128
SSD_CONV = 4
SSD_CHUNK = 128
CONV_DIM = D_SSD + 2 * SSD_GROUPS * SSD_STATE
D_FF = 5632
FFN_CONV = 3
EPS = 1e-6
PAGE_SIZE = 128

LANES = 128
SUBLANES = 8
VMEM_LIMIT_BYTES = 56 * 1024 * 1024
IN_PROJ_DIM = 2 * Q_DIM + D_ATT + D_SSD + CONV_DIM + SSD_HEADS
PROJ_CHUNK = 512
SSD_SAMPLE_GROUP = 8
NEG_INF = float("-inf")


def _cparams(sem):
    return pltpu.CompilerParams(dimension_semantics=sem, vmem_limit_bytes=VMEM_LIMIT_BYTES)


def _sigmoid(x):
    return 1.0 / (1.0 + jnp.exp(-x))


def _rms_rows(x, w):
    return x * lax.rsqrt(jnp.mean(x * x, axis=-1, keepdims=True) + EPS) * w


def _inproj_kernel(x_ref, nw_ref, w_ref, cos_ref, sa_ref, sb_ref,
                   q_ref, k_ref, v_ref, z_ref, xbc_ref, dt_ref, kt_ref=None):
    xn = _rms_rows(x_ref[...], nw_ref[...]).astype(BF16)
    cos, sa, sb = cos_ref[...], sa_ref[...], sb_ref[...]

    def rope(p):
        up = pltpu.roll(p, LANES - ROT_DIM // 2, axis=1)
        dn = pltpu.roll(p, ROT_DIM // 2, axis=1)
        return p * cos + up * sa + dn * sb

    col = 0
    for ref, width, rot in ((q_ref, Q_DIM, True), (k_ref, Q_DIM, True), (v_ref, D_ATT, False),
                            (z_ref, D_SSD, False), (xbc_ref, CONV_DIM, False), (dt_ref, SSD_HEADS, False)):
        for c0 in range(0, width, PROJ_CHUNK):
            cw = min(PROJ_CHUNK, width - c0)
            p = jnp.dot(xn, w_ref[:, col + c0:col + c0 + cw], preferred_element_type=F32)
            if rot:
                for s in range(0, cw, LANES):
                    roped = rope(p[:, s:s + LANES])
                    ref[:, c0 + s:c0 + s + LANES] = roped
                    if ref is k_ref and kt_ref is not None:
                        kt_ref[0, c0 + s:c0 + s + LANES, :] = roped.T
            else:
                ref[:, c0:c0 + cw] = p
        col += width


def _in_proj_tile(m):
    return min(256, m)


def _in_proj(x2d, nw, w_bf, cos_t, sa_t, sb_t, *, kt_seq_len=None):
    m = x2d.shape[0]
    tm = _in_proj_tile(m)
    table_tiles = cos_t.shape[0] // tm
    row = lambda i: (i, 0)
    trow = lambda i: (i % table_tiles, 0)
    const = lambda i: (0, 0)
    widths = (Q_DIM, Q_DIM, D_ATT, D_SSD, CONV_DIM, SSD_HEADS)
    out_specs = [pl.BlockSpec((tm, w), row) for w in widths]
    out_shape = [jax.ShapeDtypeStruct((m, w), F32) for w in widths]
    if kt_seq_len is not None:
        tiles_per_seq = kt_seq_len // tm
        out_specs.append(pl.BlockSpec((1, Q_DIM, tm), lambda i: (i // tiles_per_seq, 0, i % tiles_per_seq)))
        out_shape.append(jax.ShapeDtypeStruct((m // kt_seq_len, Q_DIM, kt_seq_len), F32))
    return pl.pallas_call(
        _inproj_kernel,
        grid=(m // tm,),
        in_specs=[pl.BlockSpec((tm, D_MODEL), row),
                  pl.BlockSpec((1, D_MODEL), const),
                  pl.BlockSpec((D_MODEL, IN_PROJ_DIM), const, pipeline_mode=pl.Buffered(1)),
                  pl.BlockSpec((tm, LANES), trow), pl.BlockSpec((tm, LANES), trow), pl.BlockSpec((tm, LANES), trow)],
        out_specs=out_specs,
        out_shape=out_shape,
        compiler_params=_cparams(("arbitrary",)),
        name="in_proj",
    )(x2d, nw, w_bf, cos_t, sa_t, sb_t)


def _rope_tables(pos):
    half = ROT_DIM // 2
    inv = ROPE_THETA ** (-jnp.arange(half, dtype=F32) / half)
    ang = pos.astype(F32)[:, None] * inv[None, :]
    cos, sin = jnp.cos(ang), jnp.sin(ang)
    rows = pos.shape[0]
    ones = jnp.ones((rows, ATT_DK - ROT_DIM), F32)
    zeros = jnp.zeros((rows, ATT_DK - ROT_DIM), F32)
    zh = jnp.zeros((rows, half), F32)
    cos64 = jnp.concatenate([cos, cos, ones], axis=1)
    sa64 = jnp.concatenate([-sin, zh, zeros], axis=1)
    sb64 = jnp.concatenate([zh, sin, zeros], axis=1)
    two = lambda t: jnp.concatenate([t, t], axis=1)
    return two(cos64), two(sa64), two(sb64)


def _lambda_full(lam_ref, lam_init):
    l = lam_ref[...]
    s1 = jnp.sum(l[0:1] * l[1:2], axis=1, keepdims=True)
    s2 = jnp.sum(l[2:3] * l[3:4], axis=1, keepdims=True)
    return jnp.exp(s1) - jnp.exp(s2) + lam_init


def _attn_prompt_kernel(lam_ref, subw_ref, q_ref, k_ref, v_ref, o_ref, kbf, vt, s_scr, p_scr, *, tq, lam_init):
    s_len = q_ref.shape[1]
    lam = _lambda_full(lam_ref, lam_init)
    kbf[...] = k_ref[0].astype(BF16)
    for c in range(s_len // tq):
        vt[:, c * tq:(c + 1) * tq] = v_ref[0, c * tq:(c + 1) * tq, :].T.astype(BF16)
    lane = lax.broadcasted_iota(jnp.int32, (tq, LANES), 1)
    krow = lax.broadcasted_iota(jnp.int32, (tq, 2 * tq), 0)
    qcol = lax.broadcasted_iota(jnp.int32, (tq, 2 * tq), 1)
    diag = krow <= jnp.where(qcol >= tq, qcol - tq, qcol)
    contract_last = (((1,), (1,)), ((), ()))

    for qi in range(s_len // tq):
        q = q_ref[0, qi * tq:(qi + 1) * tq, :] * (ATT_DK ** -0.5)
        qq = jnp.concatenate([jnp.where(lane < ATT_DK, q, 0.0), jnp.where(lane >= ATT_DK, q, 0.0)],
                             axis=0).astype(BF16)
        m = None
        for c in range(qi + 1):
            s = lax.dot_general(kbf[c * tq:(c + 1) * tq, :], qq, contract_last, preferred_element_type=F32)
            if c == qi:
                s = jnp.where(diag, s, NEG_INF)
            s_scr[c * tq:(c + 1) * tq, :] = s
            cm = jnp.max(s, axis=0, keepdims=True)
            m = cm if m is None else jnp.maximum(m, cm)
        l = jnp.zeros((1, 2 * tq), F32)
        for c in range(qi + 1):
            p = jnp.exp(s_scr[c * tq:(c + 1) * tq, :] - m)
            l = l + jnp.sum(p, axis=0, keepdims=True)
            p_scr[c * tq:(c + 1) * tq, :] = p.astype(BF16)
        kv = (qi + 1) * tq
        o = jnp.dot(vt[:, :kv], p_scr[:kv, :], preferred_element_type=F32) / l
        att = o[:, :tq] - lam * o[:, tq:]
        ms = jnp.mean(att * att, axis=0, keepdims=True)
        y = att * lax.rsqrt(ms + EPS) * subw_ref[...] * (1.0 - lam_init)
        o_ref[0, qi * tq:(qi + 1) * tq, :] = y.T


def _attn_prompt(lam_vecs, subw_col, q, k, v, lam_init):
    b, s, _ = q.shape
    tq = min(256, s)
    kern = functools.partial(_attn_prompt_kernel, tq=tq, lam_init=lam_init)
    head = pl.BlockSpec((1, s, LANES), lambda bb, h: (bb, 0, h))
    return pl.pallas_call(
        kern,
        grid=(b, N_ATT_HEADS),
        in_specs=[pl.BlockSpec((4, ATT_DK), lambda bb, h: (0, 0)),
                  pl.BlockSpec((ATT_DV, 1), lambda bb, h: (0, 0)),
                  head, head, head],
        out_specs=head,
        out_shape=jax.ShapeDtypeStruct((b, s, D_ATT), F32),
        scratch_shapes=[pltpu.VMEM((s, LANES), BF16), pltpu.VMEM((ATT_DV, s), BF16),
                        pltpu.VMEM((s, 2 * tq), F32), pltpu.VMEM((s, 2 * tq), BF16)],
        compiler_params=_cparams(("arbitrary", "arbitrary")),
        name="attn_prompt",
    )(lam_vecs, subw_col, q, k, v)


def _attn_sample_kernel(pt_ref, lam_ref, subw_ref, q_ref, kn_ref, vn_ref, *rest, n_pages, t_new, lam_init):
    kpages = rest[:n_pages]
    vpages = rest[n_pages:2 * n_pages]
    o_ref, kbf, vbf = rest[2 * n_pages:]
    past = n_pages * PAGE_SIZE
    n_keys = past + PAGE_SIZE
    for j in range(n_pages):
        kbf[:, j * PAGE_SIZE:(j + 1) * PAGE_SIZE] = kpages[j][0].astype(BF16)
        for h in range(N_ATT_HEADS):
            vh = vpages[j][0, pl.ds(h, PAGE_SIZE, stride=N_ATT_HEADS), :]
            vbf[j * PAGE_SIZE:(j + 1) * PAGE_SIZE, h * ATT_DV:(h + 1) * ATT_DV] = vh.astype(BF16)
    zpad = jnp.zeros((PAGE_SIZE - t_new, Q_DIM), F32)
    k_tail = jnp.concatenate([kn_ref[0], zpad], axis=0).astype(BF16)
    vbf[past:n_keys, :] = jnp.concatenate([vn_ref[0], zpad], axis=0).astype(BF16)

    lam = _lambda_full(lam_ref, lam_init)
    q = q_ref[0] * (ATT_DK ** -0.5)
    nr = t_new * N_ATT_HEADS
    qrep = jnp.concatenate([jnp.broadcast_to(q[t:t + 1], (N_ATT_HEADS, Q_DIM)) for t in range(t_new)], axis=0)
    row = lax.broadcasted_iota(jnp.int32, (nr, Q_DIM), 0)
    lane = lax.broadcasted_iota(jnp.int32, (nr, Q_DIM), 1)
    head = row % N_ATT_HEADS
    grp = lane // ATT_DK
    qq = jnp.concatenate([jnp.where(grp == 2 * head, qrep, 0.0), jnp.where(grp == 2 * head + 1, qrep, 0.0)],
                         axis=0).astype(BF16)
    s_past = jnp.dot(qq, kbf[...], preferred_element_type=F32)
    s_tail = lax.dot_general(qq, k_tail, (((1,), (1,)), ((), ())), preferred_element_type=F32)
    s = jnp.concatenate([s_past, s_tail], axis=1)
    srow = lax.broadcasted_iota(jnp.int32, (2 * nr, n_keys), 0)
    scol = lax.broadcasted_iota(jnp.int32, (2 * nr, n_keys), 1)
    tq = (srow // N_ATT_HEADS) % t_new
    s = jnp.where(scol <= past + tq, s, NEG_INF)
    m = jnp.max(s, axis=1, keepdims=True)
    p = jnp.exp(s - m)
    pn = p / jnp.sum(p, axis=1, keepdims=True)
    a = pn[:nr] - lam * pn[nr:]
    o = jnp.dot(a.astype(BF16), vbf[...], preferred_element_type=F32)
    o = jnp.where(lane // ATT_DV == head, o, 0.0)
    ms = jnp.sum(o * o, axis=1, keepdims=True) * (1.0 / ATT_DV)
    y = o * lax.rsqrt(ms + EPS)
    att = jnp.concatenate([jnp.sum(y[t * N_ATT_HEADS:(t + 1) * N_ATT_HEADS], axis=0, keepdims=True)
                           for t in range(t_new)], axis=0)
    o_ref[0] = att * subw_ref[...] * (1.0 - lam_init)


def _attn_sample(page_table, lam_vecs, subw_tiled, q, kn, vn, cache_k, cache_v, lam_init):
    db, t_new, _ = q.shape
    n_pages = page_table.shape[1]
    n_keys = (n_pages + 1) * PAGE_SIZE
    kern = functools.partial(_attn_sample_kernel, n_pages=n_pages, t_new=t_new, lam_init=lam_init)
    tok = pl.BlockSpec((1, t_new, Q_DIM), lambda b, pt: (b, 0, 0))
    page_maps = [functools.partial(lambda b, pt, j: (pt[b, j], 0, 0), j=j) for j in range(n_pages)]
    kpage_specs = [pl.BlockSpec((1, Q_DIM, PAGE_SIZE), pm) for pm in page_maps]
    vpage_specs = [pl.BlockSpec((1, PAGE_SIZE * N_ATT_HEADS, ATT_DV), pm) for pm in page_maps]
    grid_spec = pltpu.PrefetchScalarGridSpec(
        num_scalar_prefetch=1,
        grid=(db,),
        in_specs=[pl.BlockSpec((4, ATT_DK), lambda b, pt: (0, 0)),
                  pl.BlockSpec((1, D_ATT), lambda b, pt: (0, 0)),
                  tok, tok, tok] + kpage_specs + vpage_specs,
        out_specs=tok,
        scratch_shapes=[pltpu.VMEM((Q_DIM, n_keys - PAGE_SIZE), BF16), pltpu.VMEM((n_keys, D_ATT), BF16)],
    )
    return pl.pallas_call(
        kern,
        grid_spec=grid_spec,
        out_shape=jax.ShapeDtypeStruct((db, t_new, D_ATT), F32),
        compiler_params=_cparams(("arbitrary",)),
        name="attn_sample",
    )(page_table, lam_vecs, subw_tiled, q, kn, vn, *([cache_k] * n_pages), *([cache_v] * n_pages))


def _expand_heads(v, rows):
    return jnp.concatenate([jnp.broadcast_to(v[:, h:h + 1], (rows, SSD_HEADDIM)) for h in range(SSD_HEADS)], axis=1)


def _ssd_kernel(xbc_ref, dt_ref, z_ref, h0_ref, cw_ref, cb_ref, dtb_ref, alog_ref, dskip_ref, nw_ref,
                y_ref, hout_ref, xprev, *, seqs, seq_rows, valid_rows, carry):
    rows = seqs * seq_rows
    c = pl.program_id(1)

    @pl.when(c == 0)
    def _():
        hout_ref[...] = h0_ref[...]
        if carry:
            xprev[...] = jnp.zeros_like(xprev)

    xr = xbc_ref[0]
    conv = cb_ref[...] + cw_ref[SSD_CONV - 1:SSD_CONV] * xr
    if carry:
        pv = xprev[...]
        row8 = lax.broadcasted_iota(jnp.int32, (SUBLANES, CONV_DIM), 0)
    for k in range(1, SSD_CONV):
        sh = pltpu.roll(xr, k, axis=0)
        if carry:
            top = jnp.where(row8 < k, pltpu.roll(pv, k, axis=0), sh[:SUBLANES])
            sh = jnp.concatenate([top, sh[SUBLANES:]], axis=0)
        conv = conv + cw_ref[SSD_CONV - 1 - k:SSD_CONV - k] * sh
    if carry:
        xprev[...] = xr[rows - SUBLANES:]
    xc = conv * _sigmoid(conv)
    xs = xc[:, :D_SSD]

    dtv = dt_ref[0] + dtb_ref[...]
    dtv = jnp.maximum(dtv, 0.0) + jnp.log1p(jnp.exp(-jnp.abs(dtv)))
    if valid_rows < seq_rows:
        rvalid = lax.broadcasted_iota(jnp.int32, (rows, SSD_HEADS), 0) % seq_rows < valid_rows
        dtv = jnp.where(rvalid, dtv, 0.0)
    a = dtv * (-jnp.exp(alog_ref[...]))
    ri = lax.broadcasted_iota(jnp.int32, (rows, rows), 0)
    ci = lax.broadcasted_iota(jnp.int32, (rows, rows), 1)
    tril = ri >= ci
    if seqs > 1:
        tril = tril & (ri // seq_rows == ci // seq_rows)
    a_cs = jnp.dot(tril.astype(F32), a, preferred_element_type=F32, precision=lax.Precision.HIGHEST)
    totals = [a_cs[(s + 1) * seq_rows - 1:(s + 1) * seq_rows] for s in range(seqs)]
    e_tots = [jnp.exp(t) for t in totals]
    total = totals[0] if seqs == 1 else jnp.concatenate(
        [jnp.broadcast_to(t, (seq_rows, SSD_HEADS)) for t in totals], axis=0)
    xdt = xs * _expand_heads(dtv, rows)
    xdtd = (xdt * _expand_heads(jnp.exp(total - a_cs), rows)).astype(BF16)
    e_exp = _expand_heads(jnp.exp(a_cs), rows)
    eye = ri == ci

    y_diag_parts, y_off_parts = [], []
    heads_per_group = SSD_HEADS // SSD_GROUPS
    gw = heads_per_group * SSD_HEADDIM
    contract_last = (((1,), (1,)), ((), ()))
    contract_first = (((0,), (0,)), ((), ()))
    for g in range(SSD_GROUPS):
        bg = xc[:, D_SSD + g * SSD_STATE:D_SSD + (g + 1) * SSD_STATE].astype(BF16)
        cg = xc[:, D_SSD + (SSD_GROUPS + g) * SSD_STATE:D_SSD + (SSD_GROUPS + g + 1) * SSD_STATE].astype(BF16)
        y_off_seq, st_seq = [], []
        for s in range(seqs):
            rs = slice(s * seq_rows, (s + 1) * seq_rows)
            hg = hout_ref[s, g * heads_per_group:(g + 1) * heads_per_group].reshape(gw, SSD_STATE)
            y_off_seq.append(lax.dot_general(cg[rs], hg.astype(BF16), contract_last, preferred_element_type=F32))
            st_seq.append(lax.dot_general(xdtd[rs, g * gw:(g + 1) * gw], bg[rs], contract_first,
                                          preferred_element_type=F32))
        y_off = y_off_seq[0] if seqs == 1 else jnp.concatenate(y_off_seq, axis=0)
        cbm = lax.dot_general(cg, bg, contract_last, preferred_element_type=F32)
        for r in range(heads_per_group):
            h = g * heads_per_group + r
            col = a_cs[:, h:h + 1]
            rowv = jnp.sum(jnp.where(eye, col, 0.0), axis=0, keepdims=True)
            lmat = jnp.where(tril, jnp.exp(col - rowv), 0.0)
            mm = (cbm * lmat).astype(BF16)
            y_diag_parts.append(jnp.dot(mm, xdt[:, h * SSD_HEADDIM:(h + 1) * SSD_HEADDIM].astype(BF16),
                                        preferred_element_type=F32))
            for s in range(seqs):
                hout_ref[s, h] = (hout_ref[s, h] * e_tots[s][:, h:h + 1]
                                  + st_seq[s][r * SSD_HEADDIM:(r + 1) * SSD_HEADDIM])
        y_off_parts.append(y_off)
    y = (jnp.concatenate(y_diag_parts, axis=1) + jnp.concatenate(y_off_parts, axis=1) * e_exp
         + _expand_heads(dskip_ref[...], 1) * xs)
    zz = z_ref[0]
    y = y * (zz * _sigmoid(zz))
    half = D_SSD // SSD_GROUPS
    nw = nw_ref[...]
    y_ref[0] = jnp.concatenate([_rms_rows(y[:, g * half:(g + 1) * half], nw[:, g * half:(g + 1) * half])
                                for g in range(SSD_GROUPS)], axis=1)


def _ssd(xbc, dt, z, h0, cw, cb, dtb, alog, dskip, nw, *, seqs, seq_rows, valid_rows):
    b, s, _ = xbc.shape
    rows = seqs * seq_rows
    n_chunks = s // rows
    assert seqs == 1 or n_chunks == 1
    kern = functools.partial(_ssd_kernel, seqs=seqs, seq_rows=seq_rows, valid_rows=valid_rows, carry=seqs == 1)
    tokmap = lambda bb, c: (bb, c, 0)
    seqmap4 = lambda bb, c: (bb, 0, 0, 0)
    const = lambda bb, c: (0, 0)
    state_spec = pl.BlockSpec((seqs, SSD_HEADS, SSD_HEADDIM, SSD_STATE), seqmap4)
    return pl.pallas_call(
        kern,
        grid=(b, n_chunks),
        in_specs=[pl.BlockSpec((1, rows, CONV_DIM), tokmap),
                  pl.BlockSpec((1, rows, SSD_HEADS), tokmap),
                  pl.BlockSpec((1, rows, D_SSD), tokmap),
                  state_spec,
                  pl.BlockSpec((SSD_CONV, CONV_DIM), const),
                  pl.BlockSpec((1, CONV_DIM), const),
                  pl.BlockSpec((1, SSD_HEADS), const),
                  pl.BlockSpec((1, SSD_HEADS), const),
                  pl.BlockSpec((1, SSD_HEADS), const),
                  pl.BlockSpec((1, D_SSD), const)],
        out_specs=[pl.BlockSpec((1, rows, D_SSD), tokmap), state_spec],
        out_shape=[jax.ShapeDtypeStruct((b, s, D_SSD), F32),
                   jax.ShapeDtypeStruct((b * seqs, SSD_HEADS, SSD_HEADDIM, SSD_STATE), F32)],
        scratch_shapes=[pltpu.VMEM((SUBLANES, CONV_DIM), F32)],
        compiler_params=_cparams(("arbitrary", "arbitrary")),
        name="ssd",
    )(xbc, dt, z, h0, cw, cb, dtb, alog, dskip, nw)


def _outproj_kernel(x_ref, att_ref, y_ref, wa_ref, wy_ref, nw_ref, x1_ref, hf_ref):
    mix = jnp.dot(att_ref[...].astype(BF16), wa_ref[...], preferred_element_type=F32)
    mix = mix + jnp.dot(y_ref[...].astype(BF16), wy_ref[...], preferred_element_type=F32)
    x1 = x_ref[...] + mix
    x1_ref[...] = x1
    hf_ref[...] = _rms_rows(x1, nw_ref[...]).astype(BF16)


def _out_proj(x2d, att, y, wa, wy, nw):
    m = x2d.shape[0]
    tm = min(512, m)
    row = lambda i: (i, 0)
    const = lambda i: (0, 0)
    return pl.pallas_call(
        _outproj_kernel,
        grid=(m // tm,),
        in_specs=[pl.BlockSpec((tm, D_MODEL), row), pl.BlockSpec((tm, D_ATT), row), pl.BlockSpec((tm, D_SSD), row),
                  pl.BlockSpec((D_ATT, D_MODEL), const, pipeline_mode=pl.Buffered(1)),
                  pl.BlockSpec((D_SSD, D_MODEL), const, pipeline_mode=pl.Buffered(1)),
                  pl.BlockSpec((1, D_MODEL), const)],
        out_specs=[pl.BlockSpec((tm, D_MODEL), row), pl.BlockSpec((tm, D_MODEL), row)],
        out_shape=[jax.ShapeDtypeStruct((m, D_MODEL), F32), jax.ShapeDtypeStruct((m, D_MODEL), BF16)],
        compiler_params=_cparams(("arbitrary",)),
        name="out_proj",
    )(x2d, att, y, wa, wy, nw)


def _ffn_kernel(x1_ref, hf_ref, wg_ref, wu_ref, wd_ref, cw_ref, cb_ref, nfw_ref, *rest,
                tm, seq_len, tiles_per_seq):
    if seq_len >= tm:
        y_ref, gt_ref, acc, gprev = rest
    else:
        st_ref, y_ref, gt_ref, acc, p1_scr, p2_scr, g_scr = rest
    i = pl.program_id(0)
    f = pl.program_id(1)
    nf = pl.num_programs(1)

    @pl.when(f == 0)
    def _():
        acc[...] = jnp.zeros_like(acc)

    hf = hf_ref[...]
    g = jnp.dot(hf, wg_ref[...], preferred_element_type=F32)
    u = jnp.dot(hf, wu_ref[...], preferred_element_type=F32)
    tf = g.shape[1]
    if seq_len >= tm:
        gp = gprev[f]
        gp = jnp.where(i % tiles_per_seq == 0, 0.0, gp)
        row8 = lax.broadcasted_iota(jnp.int32, (SUBLANES, tf), 0)
        shifted = []
        for k in range(1, FFN_CONV):
            sh = pltpu.roll(g, k, axis=0)
            top = jnp.where(row8 < k, pltpu.roll(gp, k, axis=0), sh[:SUBLANES])
            shifted.append(jnp.concatenate([top, sh[SUBLANES:]], axis=0))
        g1, g2 = shifted
        gprev[f] = g[tm - SUBLANES:]
        gt_ref[0] = g[tm - SUBLANES:]
    else:
        ns = tm // seq_len

        @pl.when((i == 0) & (f == 0))
        def _():
            p1_scr[...] = jnp.zeros_like(p1_scr)
            p2_scr[...] = jnp.zeros_like(p2_scr)

        for c in range(tf // LANES):
            cs = slice(c * LANES, (c + 1) * LANES)
            g_scr[c] = g[:, cs]
            p1_scr[c, pl.ds(0, ns, stride=seq_len), :] = st_ref[1, :, cs]
            p2_scr[c, pl.ds(0, ns, stride=seq_len), :] = st_ref[0, :, cs]
            p2_scr[c, pl.ds(1, ns, stride=seq_len), :] = st_ref[1, :, cs]
            gt_ref[0, :, cs] = g_scr[c, pl.ds(seq_len - 2, ns, stride=seq_len), :]
            gt_ref[1, :, cs] = g_scr[c, pl.ds(seq_len - 1, ns, stride=seq_len), :]
        p1 = jnp.concatenate([p1_scr[c] for c in range(tf // LANES)], axis=1)
        p2 = jnp.concatenate([p2_scr[c] for c in range(tf // LANES)], axis=1)
        pos = lax.broadcasted_iota(jnp.int32, (tm, tf), 0) % seq_len
        g1 = jnp.where(pos >= 1, pltpu.roll(g, 1, axis=0), 0.0) + p1
        g2 = jnp.where(pos >= 2, pltpu.roll(g, 2, axis=0), 0.0) + p2
    gc = cb_ref[...] + cw_ref[0:1] * g2 + cw_ref[1:2] * g1 + cw_ref[2:3] * g
    act = (gc * _sigmoid(gc) * u).astype(BF16)
    acc[...] += jnp.dot(act, wd_ref[...], preferred_element_type=F32)

    @pl.when(f == nf - 1)
    def _():
        y_ref[...] = _rms_rows(x1_ref[...] + acc[...], nfw_ref[...])


def _ffn(x1, hf, wg, wu, wd, cw, cb, nfw, state, *, seq_len):
    m = x1.shape[0]
    tm = min(512, m)
    tf = 512
    nf = D_FF // tf
    prompt_mode = seq_len >= tm
    tiles_per_seq = max(seq_len // tm, 1)
    kern = functools.partial(_ffn_kernel, tm=tm, seq_len=seq_len, tiles_per_seq=tiles_per_seq)
    row = lambda i, f: (i, 0)
    in_specs = [pl.BlockSpec((tm, D_MODEL), row), pl.BlockSpec((tm, D_MODEL), row),
                pl.BlockSpec((D_MODEL, tf), lambda i, f: (0, f)), pl.BlockSpec((D_MODEL, tf), lambda i, f: (0, f)),
                pl.BlockSpec((tf, D_MODEL), lambda i, f: (f, 0)),
                pl.BlockSpec((FFN_CONV, tf), lambda i, f: (0, f)), pl.BlockSpec((1, tf), lambda i, f: (0, f)),
                pl.BlockSpec((1, D_MODEL), lambda i, f: (0, 0))]
    args = [x1, hf, wg, wu, wd, cw, cb, nfw]
    scratch = [pltpu.VMEM((tm, D_MODEL), F32)]
    if prompt_mode:
        out_specs = [pl.BlockSpec((tm, D_MODEL), row), pl.BlockSpec((1, SUBLANES, tf), lambda i, f: (i, 0, f))]
        out_shape = [jax.ShapeDtypeStruct((m, D_MODEL), F32), jax.ShapeDtypeStruct((m // tm, SUBLANES, D_FF), F32)]
        scratch.append(pltpu.VMEM((nf, SUBLANES, tf), F32))
    else:
        ns = tm // seq_len
        state_spec = pl.BlockSpec((FFN_CONV - 1, ns, tf), lambda i, f: (0, i, f))
        in_specs.append(state_spec)
        args.append(state)
        out_specs = [pl.BlockSpec((tm, D_MODEL), row), state_spec]
        out_shape = [jax.ShapeDtypeStruct((m, D_MODEL), F32), jax.ShapeDtypeStruct(state.shape, F32)]
        scratch += [pltpu.VMEM((tf // LANES, tm, LANES), F32)] * 3
    return pl.pallas_call(
        kern,
        grid=(m // tm, nf),
        in_specs=in_specs,
        out_specs=out_specs,
        out_shape=out_shape,
        scratch_shapes=scratch,
        compiler_params=_cparams(("arbitrary", "arbitrary")),
        name="ffn",
    )(*args)


def _layer(xp, xs, past, lam_init, p):
    (cache_k, cache_v, state_ssm, state_conv_ssd, state_conv_ffn, page_table) = past
    bp, sp, _ = xp.shape
    db, ds, _ = xs.shape
    n_pages = page_table.shape[1]
    past_len = n_pages * PAGE_SIZE

    w_in = p["w_in"].astype(BF16)
    nmw = p["norm_mix_w"].reshape(1, D_MODEL)
    lam_vecs = jnp.stack([p["lambda_q1"], p["lambda_k1"], p["lambda_q2"], p["lambda_k2"]])
    subw = p["subln_w"].reshape(1, ATT_DV)
    subw_tiled = jnp.tile(subw, (1, N_ATT_HEADS))
    cw_ssd = p["conv_ssd_w"]
    cb_ssd = p["conv_ssd_b"].reshape(1, CONV_DIM)
    dtb = p["dt_bias"].reshape(1, SSD_HEADS)
    alog = p["a_log"].reshape(1, SSD_HEADS)
    dskip = p["d_skip"].reshape(1, SSD_HEADS)
    nsw = p["norm_ssd_w"].reshape(1, D_SSD)
    w_out = p["w_out"].astype(BF16)
    wa, wy = w_out[:D_ATT], w_out[D_ATT:]
    nfw = p["norm_ffn_w"].reshape(1, D_MODEL)
    wg, wu, wd = p["w_gate"].astype(BF16), p["w_up"].astype(BF16), p["w_down"].astype(BF16)
    cw_ffn = p["conv_ffn_w"]
    cb_ffn = p["conv_ffn_b"].reshape(1, D_FF)
    nfin = p["norm_final_w"].reshape(1, D_MODEL)

    q, k, v, z, xbc, dt, kt = _in_proj(xp.reshape(bp * sp, D_MODEL), nmw, w_in, *_rope_tables(jnp.arange(sp)),
                                       kt_seq_len=sp)
    r3 = lambda t: t.reshape(bp, sp, t.shape[-1])
    att = _attn_prompt(lam_vecs, subw.reshape(ATT_DV, 1), r3(q), r3(k), r3(v), lam_init)
    rows = SSD_CHUNK if sp % SSD_CHUNK == 0 else sp
    y, ssm_p = _ssd(r3(xbc), r3(dt), r3(z), jnp.zeros((bp, SSD_HEADS, SSD_HEADDIM, SSD_STATE), F32),
                    cw_ssd, cb_ssd, dtb, alog, dskip, nsw, seqs=1, seq_rows=rows, valid_rows=rows)
    x1, hf = _out_proj(xp.reshape(bp * sp, D_MODEL), att.reshape(bp * sp, D_ATT), y.reshape(bp * sp, D_SSD), wa, wy, nfw)
    yp, gtail = _ffn(x1, hf, wg, wu, wd, cw_ffn, cb_ffn, nfin, None, seq_len=sp)
    tiles_per_seq = gtail.shape[0] // bp
    conv_ffn_p = gtail.reshape(bp, tiles_per_seq, SUBLANES, D_FF)[:, -1, SUBLANES - (FFN_CONV - 1):]
    new_k_p = jnp.transpose(kt.reshape(bp, N_ATT_HEADS, 2, ATT_DK, sp), (0, 4, 1, 2, 3))
    prompt_out = (yp.reshape(bp, sp, D_MODEL), new_k_p, v.reshape(bp, sp, N_ATT_HEADS, ATT_DV),
                  ssm_p, r3(xbc)[:, sp - (SSD_CONV - 1):], conv_ffn_p)

    pos_s = jnp.tile(past_len + jnp.arange(ds), _in_proj_tile(db * ds) // ds)
    q, k, v, z, xbc, dt = _in_proj(xs.reshape(db * ds, D_MODEL), nmw, w_in, *_rope_tables(pos_s))
    r3s = lambda t: t.reshape(db, ds, t.shape[-1])
    n_phys = cache_k.shape[0]
    cache_kt = jnp.transpose(cache_k, (0, 2, 3, 4, 1)).reshape(n_phys, Q_DIM, PAGE_SIZE)
    att = _attn_sample(page_table, lam_vecs, subw_tiled, r3s(q), r3s(k), r3s(v),
                       cache_kt, cache_v.reshape(n_phys, PAGE_SIZE * N_ATT_HEADS, ATT_DV), lam_init)
    gs = math.gcd(db, SSD_SAMPLE_GROUP)
    pad8 = lambda t: jnp.pad(t, ((0, 0), (0, SUBLANES - ds), (0, 0))).reshape(db // gs, gs * SUBLANES, t.shape[-1])
    nxt_state = jnp.roll(state_conv_ssd.reshape(db // gs, gs, SSD_CONV - 1, CONV_DIM), -1, axis=1)
    xe = jnp.concatenate([r3s(xbc), jnp.zeros((db, SUBLANES - ds - (SSD_CONV - 1), CONV_DIM), F32),
                          nxt_state.reshape(db, SSD_CONV - 1, CONV_DIM)], axis=1)
    y8, ssm_s = _ssd(xe.reshape(db // gs, gs * SUBLANES, CONV_DIM), pad8(r3s(dt)), pad8(r3s(z)), state_ssm,
                     cw_ssd, cb_ssd, dtb, alog, dskip, nsw, seqs=gs, seq_rows=SUBLANES, valid_rows=ds)
    y = y8.reshape(db, SUBLANES, D_SSD)[:, :ds]
    x1, hf = _out_proj(xs.reshape(db * ds, D_MODEL), att.reshape(db * ds, D_ATT), y.reshape(db * ds, D_SSD), wa, wy, nfw)
    ys, conv_ffn_s = _ffn(x1, hf, wg, wu, wd, cw_ffn, cb_ffn, nfin, jnp.transpose(state_conv_ffn, (1, 0, 2)),
                          seq_len=ds)
    sample_out = (ys.reshape(db, ds, D_MODEL),
                  k.reshape(db, ds, N_ATT_HEADS, 2, ATT_DK), v.reshape(db, ds, N_ATT_HEADS, ATT_DV),
                  ssm_s, r3s(xbc)[:, ds - (SSD_CONV - 1):], jnp.transpose(conv_ffn_s, (1, 0, 2)))
    return prompt_out, sample_out


def kernel(x_prompt, x_sample, cache_k, cache_v, state_ssm, state_conv_ssd, state_conv_ffn, page_table, norm_mix_w, w_in, lambda_q1, lambda_k1, lambda_q2, lambda_k2, subln_w, conv_ssd_w, conv_ssd_b, dt_bias, a_log, d_skip, norm_ssd_w, w_out, norm_ffn_w, w_gate, w_up, conv_ffn_w, conv_ffn_b, w_down, norm_final_w):
    depth = w_in.shape[0]
    assert depth == 1, "the final RMSNorm is fused into the (single) layer's FFN kernel"
    lam_init = 0.8 - 0.6 * math.exp(-0.3 * 0)
    params = dict(norm_mix_w=norm_mix_w[0], w_in=w_in[0], lambda_q1=lambda_q1[0], lambda_k1=lambda_k1[0],
                  lambda_q2=lambda_q2[0], lambda_k2=lambda_k2[0], subln_w=subln_w[0], conv_ssd_w=conv_ssd_w[0],
                  conv_ssd_b=conv_ssd_b[0], dt_bias=dt_bias[0], a_log=a_log[0], d_skip=d_skip[0],
                  norm_ssd_w=norm_ssd_w[0], w_out=w_out[0], norm_ffn_w=norm_ffn_w[0], w_gate=w_gate[0],
                  w_up=w_up[0], conv_ffn_w=conv_ffn_w[0], conv_ffn_b=conv_ffn_b[0], w_down=w_down[0],
                  norm_final_w=norm_final_w)
    past = (cache_k[0], cache_v[0], state_ssm[0], state_conv_ssd[0], state_conv_ffn[0], page_table)
    (yp, kp, vp, sp_, cp, fp), (ys, ks, vs, ss, cs, fs) = _layer(x_prompt, x_sample, past, lam_init, params)
    lead = lambda t: t[None]
    return (yp, ys, lead(kp), lead(vp), lead(sp_), lead(cp), lead(fp),
            lead(ks), lead(vs), lead(ss), lead(cs), lead(fs))
```

```python
import functools
import math

import jax
import jax.numpy as jnp
from jax import lax
from jax.experimental import pallas as pl
from jax.experimental.pallas import tpu as pltpu

F32 = jnp.float32
BF16 = jnp.bfloat16

D_MODEL = 2048
ATT_DK = 64
ATT_DV = 128
N_ATT_HEADS = 8
ROT_DIM = 16
ROPE_THETA = 500000.0
Q_DIM = N_ATT_HEADS * 2 * ATT_DK
D_ATT = N_ATT_HEADS * ATT_DV
D_SSD = 1024
SSD_HEADDIM = 64
SSD_HEADS = 16
SSD_GROUPS = 2
SSD_STATE = 128
SSD_CONV = 4
SSD_CHUNK = 128
CONV_DIM = D_SSD + 2 * SSD_GROUPS * SSD_STATE
D_FF = 5632
FFN_CONV = 3
EPS = 1e-6
PAGE_SIZE = 128

LANES = 128
SUBLANES = 8
VMEM_LIMIT_BYTES = 56 * 1024 * 1024
IN_PROJ_DIM = 2 * Q_DIM + D_ATT + D_SSD + CONV_DIM + SSD_HEADS
PROJ_CHUNK = 512
SSD_SAMPLE_GROUP = 8
PROMPT_Q_TILE = 256
FUSED_ATTN_VMEM_LIMIT_BYTES = 60 * 1024 * 1024
LOG2_E = math.log2(math.e)
NEG_INF = float("-inf")


def _cparams(sem):
    return pltpu.CompilerParams(dimension_semantics=sem, vmem_limit_bytes=VMEM_LIMIT_BYTES)


def _sigmoid(x):
    return 1.0 / (1.0 + jnp.exp(-x))


def _rms_rows(x, w):
    return x * lax.rsqrt(jnp.mean(x * x, axis=-1, keepdims=True) + EPS) * w


def _inproj_kernel(x_ref, nw_ref, w_ref, cos_ref, sa_ref, sb_ref,
                   q_ref, k_ref, v_ref, z_ref, xbc_ref, dt_ref, kt_ref=None):
    xn = _rms_rows(x_ref[...], nw_ref[...]).astype(BF16)
    cos, sa, sb = cos_ref[...], sa_ref[...], sb_ref[...]

    def rope(p):
        up = pltpu.roll(p, LANES - ROT_DIM // 2, axis=1)
        dn = pltpu.roll(p, ROT_DIM // 2, axis=1)
        return p * cos + up * sa + dn * sb

    col = 0
    for ref, width, rot in ((q_ref, Q_DIM, True), (k_ref, Q_DIM, True), (v_ref, D_ATT, False),
                            (z_ref, D_SSD, False), (xbc_ref, CONV_DIM, False), (dt_ref, SSD_HEADS, False)):
        for c0 in range(0, width, PROJ_CHUNK):
            cw = min(PROJ_CHUNK, width - c0)
            p = jnp.dot(xn, w_ref[:, col + c0:col + c0 + cw], preferred_element_type=F32)
            if rot:
                for s in range(0, cw, LANES):
                    roped = rope(p[:, s:s + LANES])
                    ref[:, c0 + s:c0 + s + LANES] = roped
                    if ref is k_ref and kt_ref is not None:
                        kt_ref[0, c0 + s:c0 + s + LANES, :] = roped.T
            else:
                ref[:, c0:c0 + cw] = p
        col += width


def _in_proj_tile(m):
    return min(256, m)


def _in_proj(x2d, nw, w_bf, cos_t, sa_t, sb_t, *, kt_seq_len=None):
    m = x2d.shape[0]
    tm = _in_proj_tile(m)
    table_tiles = cos_t.shape[0] // tm
    row = lambda i: (i, 0)
    trow = lambda i: (i % table_tiles, 0)
    const = lambda i: (0, 0)
    widths = (Q_DIM, Q_DIM, D_ATT, D_SSD, CONV_DIM, SSD_HEADS)
    out_specs = [pl.BlockSpec((tm, w), row) for w in widths]
    out_shape = [jax.ShapeDtypeStruct((m, w), F32) for w in widths]
    if kt_seq_len is not None:
        tiles_per_seq = kt_seq_len // tm
        out_specs.append(pl.BlockSpec((1, Q_DIM, tm), lambda i: (i // tiles_per_seq, 0, i % tiles_per_seq)))
        out_shape.append(jax.ShapeDtypeStruct((m // kt_seq_len, Q_DIM, kt_seq_len), F32))
    return pl.pallas_call(
        _inproj_kernel,
        grid=(m // tm,),
        in_specs=[pl.BlockSpec((tm, D_MODEL), row),
                  pl.BlockSpec((1, D_MODEL), const),
                  pl.BlockSpec((D_MODEL, IN_PROJ_DIM), const, pipeline_mode=pl.Buffered(1)),
                  pl.BlockSpec((tm, LANES), trow), pl.BlockSpec((tm, LANES), trow), pl.BlockSpec((tm, LANES), trow)],
        out_specs=out_specs,
        out_shape=out_shape,
        compiler_params=_cparams(("arbitrary",)),
        name="in_proj",
    )(x2d, nw, w_bf, cos_t, sa_t, sb_t)


def _rope_tables(pos):
    half = ROT_DIM // 2
    inv = ROPE_THETA ** (-jnp.arange(half, dtype=F32) / half)
    ang = pos.astype(F32)[:, None] * inv[None, :]
    cos, sin = jnp.cos(ang), jnp.sin(ang)
    rows = pos.shape[0]
    ones = jnp.ones((rows, ATT_DK - ROT_DIM), F32)
    zeros = jnp.zeros((rows, ATT_DK - ROT_DIM), F32)
    zh = jnp.zeros((rows, half), F32)
    cos64 = jnp.concatenate([cos, cos, ones], axis=1)
    sa64 = jnp.concatenate([-sin, zh, zeros], axis=1)
    sb64 = jnp.concatenate([zh, sin, zeros], axis=1)
    two = lambda t: jnp.concatenate([t, t], axis=1)
    return two(cos64), two(sa64), two(sb64)


def _lambda_full(lam_ref, lam_init):
    l = lam_ref[...]
    s1 = jnp.sum(l[0:1] * l[1:2], axis=1, keepdims=True)
    s2 = jnp.sum(l[2:3] * l[3:4], axis=1, keepdims=True)
    return jnp.exp(s1) - jnp.exp(s2) + lam_init


def _prompt_prepare(k_ref, v_ref, kbf, vt, tq):
    kbf[...] = k_ref[0].astype(BF16)
    for c in range(k_ref.shape[1] // tq):
        vt[:, c * tq:(c + 1) * tq] = v_ref[0, c * tq:(c + 1) * tq, :].T.astype(BF16)


def _prompt_qtile(qi, lam, subw_ref, q_ref, o_ref, kbf, vt, s_scr, p_scr, tq, lam_init):
    lane = lax.broadcasted_iota(jnp.int32, (tq, LANES), 1)
    krow = lax.broadcasted_iota(jnp.int32, (tq, 2 * tq), 0)
    qcol = lax.broadcasted_iota(jnp.int32, (tq, 2 * tq), 1)
    diag = krow <= jnp.where(qcol >= tq, qcol - tq, qcol)
    contract_last = (((1,), (1,)), ((), ()))
    q = q_ref[0, qi * tq:(qi + 1) * tq, :] * (ATT_DK ** -0.5 * LOG2_E)
    qq = jnp.concatenate([jnp.where(lane < ATT_DK, q, 0.0), jnp.where(lane >= ATT_DK, q, 0.0)],
                         axis=0).astype(BF16)
    m = None
    for c in range(qi + 1):
        s = lax.dot_general(kbf[c * tq:(c + 1) * tq, :], qq, contract_last, preferred_element_type=F32)
        if c == qi:
            s = jnp.where(diag, s, NEG_INF)
        s_scr[c * tq:(c + 1) * tq, :] = s
        cm = jnp.max(s, axis=0, keepdims=True)
        m = cm if m is None else jnp.maximum(m, cm)
    l = jnp.zeros((1, 2 * tq), F32)
    for c in range(qi + 1):
        p = jnp.exp2(s_scr[c * tq:(c + 1) * tq, :] - m)
        l = l + jnp.sum(p, axis=0, keepdims=True)
        p_scr[c * tq:(c + 1) * tq, :] = p.astype(BF16)
    kv = (qi + 1) * tq
    o = jnp.dot(vt[:, :kv], p_scr[:kv, :], preferred_element_type=F32) / l
    att = o[:, :tq] - lam * o[:, tq:]
    ms = jnp.mean(att * att, axis=0, keepdims=True)
    y = att * lax.rsqrt(ms + EPS) * subw_ref[...] * (1.0 - lam_init)
    o_ref[0, qi * tq:(qi + 1) * tq, :] = y.T


def _attn_prompt_kernel(lam_ref, subw_ref, q_ref, k_ref, v_ref, o_ref, kbf, vt, s_scr, p_scr, *, tq, lam_init):
    lam = _lambda_full(lam_ref, lam_init)
    _prompt_prepare(k_ref, v_ref, kbf, vt, tq)
    for qi in range(q_ref.shape[1] // tq):
        _prompt_qtile(qi, lam, subw_ref, q_ref, o_ref, kbf, vt, s_scr, p_scr, tq, lam_init)


def _prompt_scratch(s, tq):
    return [pltpu.VMEM((s, LANES), BF16), pltpu.VMEM((ATT_DV, s), BF16),
            pltpu.VMEM((s, 2 * tq), F32), pltpu.VMEM((s, 2 * tq), BF16)]


def _attn_prompt(lam_vecs, subw_col, q, k, v, lam_init):
    b, s, _ = q.shape
    tq = min(PROMPT_Q_TILE, s)
    kern = functools.partial(_attn_prompt_kernel, tq=tq, lam_init=lam_init)
    head = pl.BlockSpec((1, s, LANES), lambda bb, h: (bb, 0, h))
    return pl.pallas_call(
        kern,
        grid=(b, N_ATT_HEADS),
        in_specs=[pl.BlockSpec((4, ATT_DK), lambda bb, h: (0, 0)),
                  pl.BlockSpec((ATT_DV, 1), lambda bb, h: (0, 0)),
                  head, head, head],
        out_specs=head,
        out_shape=jax.ShapeDtypeStruct((b, s, D_ATT), F32),
        scratch_shapes=_prompt_scratch(s, tq),
        compiler_params=_cparams(("arbitrary", "arbitrary")),
        name="attn_prompt",
    )(lam_vecs, subw_col, q, k, v)


def _sample_attend(lam_ref, subw_ref, q_ref, kn_ref, vn_ref, kpages, vpages, o_ref, kbf, vbf, t_new, lam_init):
    n_pages = len(kpages)
    past = n_pages * PAGE_SIZE
    n_keys = past + PAGE_SIZE
    for j in range(n_pages):
        kbf[:, j * PAGE_SIZE:(j + 1) * PAGE_SIZE] = kpages[j][0].astype(BF16)
        for h in range(N_ATT_HEADS):
            vh = vpages[j][0, pl.ds(h, PAGE_SIZE, stride=N_ATT_HEADS), :]
            vbf[j * PAGE_SIZE:(j + 1) * PAGE_SIZE, h * ATT_DV:(h + 1) * ATT_DV] = vh.astype(BF16)
    zpad = jnp.zeros((PAGE_SIZE - t_new, Q_DIM), F32)
    k_tail = jnp.concatenate([kn_ref[0], zpad], axis=0).astype(BF16)
    vbf[past:n_keys, :] = jnp.concatenate([vn_ref[0], zpad], axis=0).astype(BF16)

    lam = _lambda_full(lam_ref, lam_init)
    q = q_ref[0] * (ATT_DK ** -0.5)
    nr = t_new * N_ATT_HEADS
    qrep = jnp.concatenate([jnp.broadcast_to(q[t:t + 1], (N_ATT_HEADS, Q_DIM)) for t in range(t_new)], axis=0)
    row = lax.broadcasted_iota(jnp.int32, (nr, Q_DIM), 0)
    lane = lax.broadcasted_iota(jnp.int32, (nr, Q_DIM), 1)
    head = row % N_ATT_HEADS
    grp = lane // ATT_DK
    qq = jnp.concatenate([jnp.where(grp == 2 * head, qrep, 0.0), jnp.where(grp == 2 * head + 1, qrep, 0.0)],
                         axis=0).astype(BF16)
    s_past = jnp.dot(qq, kbf[...], preferred_element_type=F32)
    s_tail = lax.dot_general(qq, k_tail, (((1,), (1,)), ((), ())), preferred_element_type=F32)
    s = jnp.concatenate([s_past, s_tail], axis=1)
    srow = lax.broadcasted_iota(jnp.int32, (2 * nr, n_keys), 0)
    scol = lax.broadcasted_iota(jnp.int32, (2 * nr, n_keys), 1)
    tq = (srow // N_ATT_HEADS) % t_new
    s = jnp.where(scol <= past + tq, s, NEG_INF)
    m = jnp.max(s, axis=1, keepdims=True)
    p = jnp.exp(s - m)
    pn = p / jnp.sum(p, axis=1, keepdims=True)
    a = pn[:nr] - lam * pn[nr:]
    o = jnp.dot(a.astype(BF16), vbf[...], preferred_element_type=F32)
    o = jnp.where(lane // ATT_DV == head, o, 0.0)
    ms = jnp.sum(o * o, axis=1, keepdims=True) * (1.0 / ATT_DV)
    y = o * lax.rsqrt(ms + EPS)
    att = jnp.concatenate([jnp.sum(y[t * N_ATT_HEADS:(t + 1) * N_ATT_HEADS], axis=0, keepdims=True)
                           for t in range(t_new)], axis=0)
    o_ref[0] = att * subw_ref[...] * (1.0 - lam_init)


def _attn_sample_kernel(pt_ref, lam_ref, subw_ref, q_ref, kn_ref, vn_ref, *rest, n_pages, t_new, lam_init):
    kpages, vpages = rest[:n_pages], rest[n_pages:2 * n_pages]
    o_ref, kbf, vbf = rest[2 * n_pages:]
    _sample_attend(lam_ref, subw_ref, q_ref, kn_ref, vn_ref, kpages, vpages, o_ref, kbf, vbf, t_new, lam_init)


def _sample_specs(page_table, t_new):
    n_pages = page_table.shape[1]
    n_keys = (n_pages + 1) * PAGE_SIZE
    tok = pl.BlockSpec((1, t_new, Q_DIM), lambda b, pt: (b, 0, 0))
    page_maps = [functools.partial(lambda b, pt, j: (pt[b, j], 0, 0), j=j) for j in range(n_pages)]
    kpage_specs = [pl.BlockSpec((1, Q_DIM, PAGE_SIZE), pm) for pm in page_maps]
    vpage_specs = [pl.BlockSpec((1, PAGE_SIZE * N_ATT_HEADS, ATT_DV), pm) for pm in page_maps]
    in_specs = [pl.BlockSpec((4, ATT_DK), lambda b, pt: (0, 0)),
                pl.BlockSpec((1, D_ATT), lambda b, pt: (0, 0)),
                tok, tok, tok] + kpage_specs + vpage_specs
    scratch = [pltpu.VMEM((Q_DIM, n_keys - PAGE_SIZE), BF16), pltpu.VMEM((n_keys, D_ATT), BF16)]
    return in_specs, tok, scratch


def _attn_sample(page_table, lam_vecs, subw_tiled, q, kn, vn, cache_k, cache_v, lam_init):
    db, t_new, _ = q.shape
    n_pages = page_table.shape[1]
    kern = functools.partial(_attn_sample_kernel, n_pages=n_pages, t_new=t_new, lam_init=lam_init)
    in_specs, tok, scratch = _sample_specs(page_table, t_new)
    grid_spec = pltpu.PrefetchScalarGridSpec(num_scalar_prefetch=1, grid=(db,), in_specs=in_specs, out_specs=tok,
                                             scratch_shapes=scratch)
    return pl.pallas_call(
        kern,
        grid_spec=grid_spec,
        out_shape=jax.ShapeDtypeStruct((db, t_new, D_ATT), F32),
        compiler_params=_cparams(("arbitrary",)),
        name="attn_sample",
    )(page_table, lam_vecs, subw_tiled, q, kn, vn, *([cache_k] * n_pages), *([cache_v] * n_pages))


def _fused_tile_plan(n_qtiles, steps_per_head):
    order = []
    lo, hi = 0, n_qtiles - 1
    while lo <= hi:
        order.append(lo)
        if hi != lo:
            order.append(hi)
        lo, hi = lo + 1, hi - 1
    per_step = n_qtiles // steps_per_head
    return [order[u * per_step:(u + 1) * per_step] for u in range(steps_per_head)]


def _attn_fused_kernel(pt_ref, lam_ref, subw_ref, q_ref, kn_ref, vn_ref, *rest, n_pages, t_new, lam_init, tq, plan):
    kpages, vpages = rest[:n_pages], rest[n_pages:2 * n_pages]
    subw_col_ref, qp_ref, kp_ref, vp_ref, o_ref, op_ref, kbf, vbf, kbf_p, vt_p, s_scr, p_scr = rest[2 * n_pages:]
    _sample_attend(lam_ref, subw_ref, q_ref, kn_ref, vn_ref, kpages, vpages, o_ref, kbf, vbf, t_new, lam_init)

    sub = pl.program_id(0) % len(plan)
    lam = _lambda_full(lam_ref, lam_init)

    @pl.when(sub == 0)
    def _():
        _prompt_prepare(kp_ref, vp_ref, kbf_p, vt_p, tq)

    for u, tiles in enumerate(plan):
        @pl.when(sub == u)
        def _(tiles=tiles):
            for qi in tiles:
                _prompt_qtile(qi, lam, subw_col_ref, qp_ref, op_ref, kbf_p, vt_p, s_scr, p_scr, tq, lam_init)


def _attn_fused_ok(db, bp, sp):
    heads = bp * N_ATT_HEADS
    tq = min(PROMPT_Q_TILE, sp)
    return db % heads == 0 and (sp // tq) % (db // heads) == 0


def _attn_fused(page_table, lam_vecs, subw_tiled, subw_col, q, kn, vn, cache_k, cache_v, qp, kp, vp, lam_init):
    db, t_new, _ = q.shape
    bp, sp, _ = qp.shape
    n_pages = page_table.shape[1]
    tq = min(PROMPT_Q_TILE, sp)
    steps_per_head = db // (bp * N_ATT_HEADS)
    plan = _fused_tile_plan(sp // tq, steps_per_head)
    kern = functools.partial(_attn_fused_kernel, n_pages=n_pages, t_new=t_new, lam_init=lam_init, tq=tq, plan=plan)
    in_specs, tok, scratch = _sample_specs(page_table, t_new)
    head = pl.BlockSpec((1, sp, LANES), lambda b, pt: (b // steps_per_head // N_ATT_HEADS, 0,
                                                       b // steps_per_head % N_ATT_HEADS))
    grid_spec = pltpu.PrefetchScalarGridSpec(
        num_scalar_prefetch=1,
        grid=(db,),
        in_specs=in_specs + [pl.BlockSpec((ATT_DV, 1), lambda b, pt: (0, 0)), head, head, head],
        out_specs=[tok, head],
        scratch_shapes=scratch + _prompt_scratch(sp, tq),
    )
    return pl.pallas_call(
        kern,
        grid_spec=grid_spec,
        out_shape=[jax.ShapeDtypeStruct((db, t_new, D_ATT), F32), jax.ShapeDtypeStruct((bp, sp, D_ATT), F32)],
        compiler_params=pltpu.CompilerParams(dimension_semantics=("arbitrary",),
                                             vmem_limit_bytes=FUSED_ATTN_VMEM_LIMIT_BYTES),
        name="attn_fused",
    )(page_table, lam_vecs, subw_tiled, q, kn, vn, *([cache_k] * n_pages), *([cache_v] * n_pages),
      subw_col, qp, kp, vp)


def _expand_heads(v, rows):
    return jnp.concatenate([jnp.broadcast_to(v[:, h:h + 1], (rows, SSD_HEADDIM)) for h in range(SSD_HEADS)], axis=1)


def _ssd_kernel(xbc_ref, dt_ref, z_ref, h0_ref, cw_ref, cb_ref, dtb_ref, alog_ref, dskip_ref, nw_ref,
                y_ref, hout_ref, xprev, *, seqs, seq_rows, valid_rows, carry):
    rows = seqs * seq_rows
    c = pl.program_id(1)

    @pl.when(c == 0)
    def _():
        hout_ref[...] = h0_ref[...]
        if carry:
            xprev[...] = jnp.zeros_like(xprev)

    xr = xbc_ref[0]
    conv = cb_ref[...] + cw_ref[SSD_CONV - 1:SSD_CONV] * xr
    if carry:
        pv = xprev[...]
        row8 = lax.broadcasted_iota(jnp.int32, (SUBLANES, CONV_DIM), 0)
    for k in range(1, SSD_CONV):
        sh = pltpu.roll(xr, k, axis=0)
        if carry:
            top = jnp.where(row8 < k, pltpu.roll(pv, k, axis=0), sh[:SUBLANES])
            sh = jnp.concatenate([top, sh[SUBLANES:]], axis=0)
        conv = conv + cw_ref[SSD_CONV - 1 - k:SSD_CONV - k] * sh
    if carry:
        xprev[...] = xr[rows - SUBLANES:]
    xc = conv * _sigmoid(conv)
    xs = xc[:, :D_SSD]

    dtv = dt_ref[0] + dtb_ref[...]
    dtv = jnp.maximum(dtv, 0.0) + jnp.log1p(jnp.exp(-jnp.abs(dtv)))
    if valid_rows < seq_rows:
        rvalid = lax.broadcasted_iota(jnp.int32, (rows, SSD_HEADS), 0) % seq_rows < valid_rows
        dtv = jnp.where(rvalid, dtv, 0.0)
    a = dtv * (-jnp.exp(alog_ref[...]))
    ri = lax.broadcasted_iota(jnp.int32, (rows, rows), 0)
    ci = lax.broadcasted_iota(jnp.int32, (rows, rows), 1)
    tril = ri >= ci
    if seqs > 1:
        tril = tril & (ri // seq_rows == ci // seq_rows)
    a_cs = jnp.dot(tril.astype(F32), a, preferred_element_type=F32, precision=lax.Precision.HIGHEST)
    totals = [a_cs[(s + 1) * seq_rows - 1:(s + 1) * seq_rows] for s in range(seqs)]
    e_tots = [jnp.exp(t) for t in totals]
    total = totals[0] if seqs == 1 else jnp.concatenate(
        [jnp.broadcast_to(t, (seq_rows, SSD_HEADS)) for t in totals], axis=0)
    xdt = xs * _expand_heads(dtv, rows)
    xdtd = (xdt * _expand_heads(jnp.exp(total - a_cs), rows)).astype(BF16)
    e_exp = _expand_heads(jnp.exp(a_cs), rows)
    eye = ri == ci

    y_diag_parts, y_off_parts = [], []
    heads_per_group = SSD_HEADS // SSD_GROUPS
    gw = heads_per_group * SSD_HEADDIM
    contract_last = (((1,), (1,)), ((), ()))
    contract_first = (((0,), (0,)), ((), ()))
    for g in range(SSD_GROUPS):
        bg = xc[:, D_SSD + g * SSD_STATE:D_SSD + (g + 1) * SSD_STATE].astype(BF16)
        cg = xc[:, D_SSD + (SSD_GROUPS + g) * SSD_STATE:D_SSD + (SSD_GROUPS + g + 1) * SSD_STATE].astype(BF16)
        y_off_seq, st_seq = [], []
        for s in range(seqs):
            rs = slice(s * seq_rows, (s + 1) * seq_rows)
            hg = hout_ref[s, g * heads_per_group:(g + 1) * heads_per_group].reshape(gw, SSD_STATE)
            y_off_seq.append(lax.dot_general(cg[rs], hg.astype(BF16), contract_last, preferred_element_type=F32))
            st_seq.append(lax.dot_general(xdtd[rs, g * gw:(g + 1) * gw], bg[rs], contract_first,
                                          preferred_element_type=F32))
        y_off = y_off_seq[0] if seqs == 1 else jnp.concatenate(y_off_seq, axis=0)
        cbm = lax.dot_general(cg, bg, contract_last, preferred_element_type=F32)
        for r in range(heads_per_group):
            h = g * heads_per_group + r
            col = a_cs[:, h:h + 1]
            rowv = jnp.sum(jnp.where(eye, col, 0.0), axis=0, keepdims=True)
            lmat = jnp.where(tril, jnp.exp(col - rowv), 0.0)
            mm = (cbm * lmat).astype(BF16)
            y_diag_parts.append(jnp.dot(mm, xdt[:, h * SSD_HEADDIM:(h + 1) * SSD_HEADDIM].astype(BF16),
                                        preferred_element_type=F32))
            for s in range(seqs):
                hout_ref[s, h] = (hout_ref[s, h] * e_tots[s][:, h:h + 1]
                                  + st_seq[s][r * SSD_HEADDIM:(r + 1) * SSD_HEADDIM])
        y_off_parts.append(y_off)
    y = (jnp.concatenate(y_diag_parts, axis=1) + jnp.concatenate(y_off_parts, axis=1) * e_exp
         + _expand_heads(dskip_ref[...], 1) * xs)
    zz = z_ref[0]
    y = y * (zz * _sigmoid(zz))
    half = D_SSD // SSD_GROUPS
    nw = nw_ref[...]
    y_ref[0] = jnp.concatenate([_rms_rows(y[:, g * half:(g + 1) * half], nw[:, g * half:(g + 1) * half])
                                for g in range(SSD_GROUPS)], axis=1)


def _ssd(xbc, dt, z, h0, cw, cb, dtb, alog, dskip, nw, *, seqs, seq_rows, valid_rows):
    b, s, _ = xbc.shape
    rows = seqs * seq_rows
    n_chunks = s // rows
    assert seqs == 1 or n_chunks == 1
    kern = functools.partial(_ssd_kernel, seqs=seqs, seq_rows=seq_rows, valid_rows=valid_rows, carry=seqs == 1)
    tokmap = lambda bb, c: (bb, c, 0)
    seqmap4 = lambda bb, c: (bb, 0, 0, 0)
    const = lambda bb, c: (0, 0)
    state_spec = pl.BlockSpec((seqs, SSD_HEADS, SSD_HEADDIM, SSD_STATE), seqmap4)
    return pl.pallas_call(
        kern,
        grid=(b, n_chunks),
        in_specs=[pl.BlockSpec((1, rows, CONV_DIM), tokmap),
                  pl.BlockSpec((1, rows, SSD_HEADS), tokmap),
                  pl.BlockSpec((1, rows, D_SSD), tokmap),
                  state_spec,
                  pl.BlockSpec((SSD_CONV, CONV_DIM), const),
                  pl.BlockSpec((1, CONV_DIM), const),
                  pl.BlockSpec((1, SSD_HEADS), const),
                  pl.BlockSpec((1, SSD_HEADS), const),
                  pl.BlockSpec((1, SSD_HEADS), const),
                  pl.BlockSpec((1, D_SSD), const)],
        out_specs=[pl.BlockSpec((1, rows, D_SSD), tokmap), state_spec],
        out_shape=[jax.ShapeDtypeStruct((b, s, D_SSD), F32),
                   jax.ShapeDtypeStruct((b * seqs, SSD_HEADS, SSD_HEADDIM, SSD_STATE), F32)],
        scratch_shapes=[pltpu.VMEM((SUBLANES, CONV_DIM), F32)],
        compiler_params=_cparams(("arbitrary", "arbitrary")),
        name="ssd",
    )(xbc, dt, z, h0, cw, cb, dtb, alog, dskip, nw)


def _outproj_kernel(x_ref, att_ref, y_ref, wa_ref, wy_ref, nw_ref, x1_ref, hf_ref):
    mix = jnp.dot(att_ref[...].astype(BF16), wa_ref[...], preferred_element_type=F32)
    mix = mix + jnp.dot(y_ref[...].astype(BF16), wy_ref[...], preferred_element_type=F32)
    x1 = x_ref[...] + mix
    x1_ref[...] = x1
    hf_ref[...] = _rms_rows(x1, nw_ref[...]).astype(BF16)


def _out_proj(x2d, att, y, wa, wy, nw):
    m = x2d.shape[0]
    tm = min(512, m)
    row = lambda i: (i, 0)
    const = lambda i: (0, 0)
    return pl.pallas_call(
        _outproj_kernel,
        grid=(m // tm,),
        in_specs=[pl.BlockSpec((tm, D_MODEL), row), pl.BlockSpec((tm, D_ATT), row), pl.BlockSpec((tm, D_SSD), row),
                  pl.BlockSpec((D_ATT, D_MODEL), const, pipeline_mode=pl.Buffered(1)),
                  pl.BlockSpec((D_SSD, D_MODEL), const, pipeline_mode=pl.Buffered(1)),
                  pl.BlockSpec((1, D_MODEL), const)],
        out_specs=[pl.BlockSpec((tm, D_MODEL), row), pl.BlockSpec((tm, D_MODEL), row)],
        out_shape=[jax.ShapeDtypeStruct((m, D_MODEL), F32), jax.ShapeDtypeStruct((m, D_MODEL), BF16)],
        compiler_params=_cparams(("arbitrary",)),
        name="out_proj",
    )(x2d, att, y, wa, wy, nw)


def _ffn_kernel(x1_ref, hf_ref, wg_ref, wu_ref, wd_ref, cw_ref, cb_ref, nfw_ref, *rest,
                tm, seq_len, tiles_per_seq):
    if seq_len >= tm:
        y_ref, gt_ref, acc, gprev = rest
    else:
        st_ref, y_ref, gt_ref, acc, p1_scr, p2_scr, g_scr = rest
    i = pl.program_id(0)
    f = pl.program_id(1)
    nf = pl.num_programs(1)

    @pl.when(f == 0)
    def _():
        acc[...] = jnp.zeros_like(acc)

    hf = hf_ref[...]
    g = jnp.dot(hf, wg_ref[...], preferred_element_type=F32)
    u = jnp.dot(hf, wu_ref[...], preferred_element_type=F32)
    tf = g.shape[1]
    if seq_len >= tm:
        gp = gprev[f]
        gp = jnp.where(i % tiles_per_seq == 0, 0.0, gp)
        row8 = lax.broadcasted_iota(jnp.int32, (SUBLANES, tf), 0)
        shifted = []
        for k in range(1, FFN_CONV):
            sh = pltpu.roll(g, k, axis=0)
            top = jnp.where(row8 < k, pltpu.roll(gp, k, axis=0), sh[:SUBLANES])
            shifted.append(jnp.concatenate([top, sh[SUBLANES:]], axis=0))
        g1, g2 = shifted
        gprev[f] = g[tm - SUBLANES:]
        gt_ref[0] = g[tm - SUBLANES:]
    else:
        ns = tm // seq_len

        @pl.when((i == 0) & (f == 0))
        def _():
            p1_scr[...] = jnp.zeros_like(p1_scr)
            p2_scr[...] = jnp.zeros_like(p2_scr)

        for c in range(tf // LANES):
            cs = slice(c * LANES, (c + 1) * LANES)
            g_scr[c] = g[:, cs]
            p1_scr[c, pl.ds(0, ns, stride=seq_len), :] = st_ref[1, :, cs]
            p2_scr[c, pl.ds(0, ns, stride=seq_len), :] = st_ref[0, :, cs]
            p2_scr[c, pl.ds(1, ns, stride=seq_len), :] = st_ref[1, :, cs]
            gt_ref[0, :, cs] = g_scr[c, pl.ds(seq_len - 2, ns, stride=seq_len), :]
            gt_ref[1, :, cs] = g_scr[c, pl.ds(seq_len - 1, ns, stride=seq_len), :]
        p1 = jnp.concatenate([p1_scr[c] for c in range(tf // LANES)], axis=1)
        p2 = jnp.concatenate([p2_scr[c] for c in range(tf // LANES)], axis=1)
        pos = lax.broadcasted_iota(jnp.int32, (tm, tf), 0) % seq_len
        g1 = jnp.where(pos >= 1, pltpu.roll(g, 1, axis=0), 0.0) + p1
        g2 = jnp.where(pos >= 2, pltpu.roll(g, 2, axis=0), 0.0) + p2
    gc = cb_ref[...] + cw_ref[0:1] * g2 + cw_ref[1:2] * g1 + cw_ref[2:3] * g
    act = (gc * _sigmoid(gc) * u).astype(BF16)
    acc[...] += jnp.dot(act, wd_ref[...], preferred_element_type=F32)

    @pl.when(f == nf - 1)
    def _():
        y_ref[...] = _rms_rows(x1_ref[...] + acc[...], nfw_ref[...])


def _ffn(x1, hf, wg, wu, wd, cw, cb, nfw, state, *, seq_len):
    m = x1.shape[0]
    tm = min(512, m)
    tf = 512
    nf = D_FF // tf
    prompt_mode = seq_len >= tm
    tiles_per_seq = max(seq_len // tm, 1)
    kern = functools.partial(_ffn_kernel, tm=tm, seq_len=seq_len, tiles_per_seq=tiles_per_seq)
    row = lambda i, f: (i, 0)
    in_specs = [pl.BlockSpec((tm, D_MODEL), row), pl.BlockSpec((tm, D_MODEL), row),
                pl.BlockSpec((D_MODEL, tf), lambda i, f: (0, f)), pl.BlockSpec((D_MODEL, tf), lambda i, f: (0, f)),
                pl.BlockSpec((tf, D_MODEL), lambda i, f: (f, 0)),
                pl.BlockSpec((FFN_CONV, tf), lambda i, f: (0, f)), pl.BlockSpec((1, tf), lambda i, f: (0, f)),
                pl.BlockSpec((1, D_MODEL), lambda i, f: (0, 0))]
    args = [x1, hf, wg, wu, wd, cw, cb, nfw]
    scratch = [pltpu.VMEM((tm, D_MODEL), F32)]
    if prompt_mode:
        out_specs = [pl.BlockSpec((tm, D_MODEL), row), pl.BlockSpec((1, SUBLANES, tf), lambda i, f: (i, 0, f))]
        out_shape = [jax.ShapeDtypeStruct((m, D_MODEL), F32), jax.ShapeDtypeStruct((m // tm, SUBLANES, D_FF), F32)]
        scratch.append(pltpu.VMEM((nf, SUBLANES, tf), F32))
    else:
        ns = tm // seq_len
        state_spec = pl.BlockSpec((FFN_CONV - 1, ns, tf), lambda i, f: (0, i, f))
        in_specs.append(state_spec)
        args.append(state)
        out_specs = [pl.BlockSpec((tm, D_MODEL), row), state_spec]
        out_shape = [jax.ShapeDtypeStruct((m, D_MODEL), F32), jax.ShapeDtypeStruct(state.shape, F32)]
        scratch += [pltpu.VMEM((tf // LANES, tm, LANES), F32)] * 3
    return pl.pallas_call(
        kern,
        grid=(m // tm, nf),
        in_specs=in_specs,
        out_specs=out_specs,
        out_shape=out_shape,
        scratch_shapes=scratch,
        compiler_params=_cparams(("arbitrary", "arbitrary")),
        name="ffn",
    )(*args)


def _layer(xp, xs, past, lam_init, p):
    (cache_k, cache_v, state_ssm, state_conv_ssd, state_conv_ffn, page_table) = past
    bp, sp, _ = xp.shape
    db, ds, _ = xs.shape
    n_pages = page_table.shape[1]
    past_len = n_pages * PAGE_SIZE

    w_in = p["w_in"].astype(BF16)
    nmw = p["norm_mix_w"].reshape(1, D_MODEL)
    lam_vecs = jnp.stack([p["lambda_q1"], p["lambda_k1"], p["lambda_q2"], p["lambda_k2"]])
    subw = p["subln_w"].reshape(1, ATT_DV)
    subw_tiled = jnp.tile(subw, (1, N_ATT_HEADS))
    cw_ssd = p["conv_ssd_w"]
    cb_ssd = p["conv_ssd_b"].reshape(1, CONV_DIM)
    dtb = p["dt_bias"].reshape(1, SSD_HEADS)
    alog = p["a_log"].reshape(1, SSD_HEADS)
    dskip = p["d_skip"].reshape(1, SSD_HEADS)
    nsw = p["norm_ssd_w"].reshape(1, D_SSD)
    w_out = p["w_out"].astype(BF16)
    wa, wy = w_out[:D_ATT], w_out[D_ATT:]
    nfw = p["norm_ffn_w"].reshape(1, D_MODEL)
    wg, wu, wd = p["w_gate"].astype(BF16), p["w_up"].astype(BF16), p["w_down"].astype(BF16)
    cw_ffn = p["conv_ffn_w"]
    cb_ffn = p["conv_ffn_b"].reshape(1, D_FF)
    nfin = p["norm_final_w"].reshape(1, D_MODEL)

    q, k, v, z, xbc, dt, kt = _in_proj(xp.reshape(bp * sp, D_MODEL), nmw, w_in, *_rope_tables(jnp.arange(sp)),
                                       kt_seq_len=sp)
    r3 = lambda t: t.reshape(bp, sp, t.shape[-1])
    pos_s = jnp.tile(past_len + jnp.arange(ds), _in_proj_tile(db * ds) // ds)
    q_s, k_s, v_s, z_s, xbc_s, dt_s = _in_proj(xs.reshape(db * ds, D_MODEL), nmw, w_in, *_rope_tables(pos_s))
    r3s = lambda t: t.reshape(db, ds, t.shape[-1])
    n_phys = cache_k.shape[0]
    cache_kt = jnp.transpose(cache_k, (0, 2, 3, 4, 1)).reshape(n_phys, Q_DIM, PAGE_SIZE)
    cache_vr = cache_v.reshape(n_phys, PAGE_SIZE * N_ATT_HEADS, ATT_DV)
    subw_col = subw.reshape(ATT_DV, 1)
    if _attn_fused_ok(db, bp, sp):
        att_s, att = _attn_fused(page_table, lam_vecs, subw_tiled, subw_col, r3s(q_s), r3s(k_s), r3s(v_s),
                                 cache_kt, cache_vr, r3(q), r3(k), r3(v), lam_init)
    else:
        att = _attn_prompt(lam_vecs, subw_col, r3(q), r3(k), r3(v), lam_init)
        att_s = _attn_sample(page_table, lam_vecs, subw_tiled, r3s(q_s), r3s(k_s), r3s(v_s), cache_kt, cache_vr,
                             lam_init)

    rows = SSD_CHUNK if sp % SSD_CHUNK == 0 else sp
    y, ssm_p = _ssd(r3(xbc), r3(dt), r3(z), jnp.zeros((bp, SSD_HEADS, SSD_HEADDIM, SSD_STATE), F32),
                    cw_ssd, cb_ssd, dtb, alog, dskip, nsw, seqs=1, seq_rows=rows, valid_rows=rows)
    x1, hf = _out_proj(xp.reshape(bp * sp, D_MODEL), att.reshape(bp * sp, D_ATT), y.reshape(bp * sp, D_SSD), wa, wy, nfw)
    yp, gtail = _ffn(x1, hf, wg, wu, wd, cw_ffn, cb_ffn, nfin, None, seq_len=sp)
    tiles_per_seq = gtail.shape[0] // bp
    conv_ffn_p = gtail.reshape(bp, tiles_per_seq, SUBLANES, D_FF)[:, -1, SUBLANES - (FFN_CONV - 1):]
    new_k_p = jnp.transpose(kt.reshape(bp, N_ATT_HEADS, 2, ATT_DK, sp), (0, 4, 1, 2, 3))
    prompt_out = (yp.reshape(bp, sp, D_MODEL), new_k_p, v.reshape(bp, sp, N_ATT_HEADS, ATT_DV),
                  ssm_p, r3(xbc)[:, sp - (SSD_CONV - 1):], conv_ffn_p)

    k, v, z, xbc, dt, att = k_s, v_s, z_s, xbc_s, dt_s, att_s
    gs = math.gcd(db, SSD_SAMPLE_GROUP)
    pad8 = lambda t: jnp.pad(t, ((0, 0), (0, SUBLANES - ds), (0, 0))).reshape(db // gs, gs * SUBLANES, t.shape[-1])
    nxt_state = jnp.roll(state_conv_ssd.reshape(db // gs, gs, SSD_CONV - 1, CONV_DIM), -1, axis=1)
    xe = jnp.concatenate([r3s(xbc), jnp.zeros((db, SUBLANES - ds - (SSD_CONV - 1), CONV_DIM), F32),
                          nxt_state.reshape(db, SSD_CONV - 1, CONV_DIM)], axis=1)
    y8, ssm_s = _ssd(xe.reshape(db // gs, gs * SUBLANES, CONV_DIM), pad8(r3s(dt)), pad8(r3s(z)), state_ssm,
                     cw_ssd, cb_ssd, dtb, alog, dskip, nsw, seqs=gs, seq_rows=SUBLANES, valid_rows=ds)
    y = y8.reshape(db, SUBLANES, D_SSD)[:, :ds]
    x1, hf = _out_proj(xs.reshape(db * ds, D_MODEL), att.reshape(db * ds, D_ATT), y.reshape(db * ds, D_SSD), wa, wy, nfw)
    ys, conv_ffn_s = _ffn(x1, hf, wg, wu, wd, cw_ffn, cb_ffn, nfin, jnp.transpose(state_conv_ffn, (1, 0, 2)),
                          seq_len=ds)
    sample_out = (ys.reshape(db, ds, D_MODEL),
                  k.reshape(db, ds, N_ATT_HEADS, 2, ATT_DK), v.reshape(db, ds, N_ATT_HEADS, ATT_DV),
                  ssm_s, r3s(xbc)[:, ds - (SSD_CONV - 1):], jnp.transpose(conv_ffn_s, (1, 0, 2)))
    return prompt_out, sample_out


def kernel(x_prompt, x_sample, cache_k, cache_v, state_ssm, state_conv_ssd, state_conv_ffn, page_table, norm_mix_w, w_in, lambda_q1, lambda_k1, lambda_q2, lambda_k2, subln_w, conv_ssd_w, conv_ssd_b, dt_bias, a_log, d_skip, norm_ssd_w, w_out, norm_ffn_w, w_gate, w_up, conv_ffn_w, conv_ffn_b, w_down, norm_final_w):
    depth = w_in.shape[0]
    assert depth == 1, "the final RMSNorm is fused into the (single) layer's FFN kernel"
    lam_init = 0.8 - 0.6 * math.exp(-0.3 * 0)
    params = dict(norm_mix_w=norm_mix_w[0], w_in=w_in[0], lambda_q1=lambda_q1[0], lambda_k1=lambda_k1[0],
                  lambda_q2=lambda_q2[0], lambda_k2=lambda_k2[0], subln_w=subln_w[0], conv_ssd_w=conv_ssd_w[0],
                  conv_ssd_b=conv_ssd_b[0], dt_bias=dt_bias[0], a_log=a_log[0], d_skip=d_skip[0],
                  norm_ssd_w=norm_ssd_w[0], w_out=w_out[0], norm_ffn_w=norm_ffn_w[0], w_gate=w_gate[0],
                  w_up=w_up[0], conv_ffn_w=conv_ffn_w[0], conv_ffn_b=conv_ffn_b[0], w_down=w_down[0],
                  norm_final_w=norm_final_w)
    past = (cache_k[0], cache_v[0], state_ssm[0], state_conv_ssd[0], state_conv_ffn[0], page_table)
    (yp, kp, vp, sp_, cp, fp), (ys, ks, vs, ss, cs, fs) = _layer(x_prompt, x_sample, past, lam_init, params)
    lead = lambda t: t[None]
    return (yp, ys, lead(kp), lead(vp), lead(sp_), lead(cp), lead(fp),
            lead(ks), lead(vs), lead(ss), lead(cs), lead(fs))
```

```python
import functools
import math

import jax
import jax.numpy as jnp
from jax import lax
from jax.experimental import pallas as pl
from jax.experimental.pallas import tpu as pltpu

F32 = jnp.float32
BF16 = jnp.bfloat16

D_MODEL = 2048
ATT_DK = 64
ATT_DV = 128
N_ATT_HEADS = 8
ROT_DIM = 16
ROPE_THETA = 500000.0
Q_DIM = N_ATT_HEADS * 2 * ATT_DK
D_ATT = N_ATT_HEADS * ATT_DV
D_SSD = 1024
SSD_HEADDIM = 64
SSD_HEADS = 16
SSD_GROUPS = 2
SSD_STATE = 128
SSD_CONV = 4
SSD_CHUNK = 128
CONV_DIM = D_SSD + 2 * SSD_GROUPS * SSD_STATE
D_FF = 5632
FFN_CONV = 3
EPS = 1e-6
PAGE_SIZE = 128

LANES = 128
SUBLANES = 8
VMEM_LIMIT_BYTES = 56 * 1024 * 1024
IN_PROJ_DIM = 2 * Q_DIM + D_ATT + D_SSD + CONV_DIM + SSD_HEADS
PROJ_CHUNK = 512
SSD_SAMPLE_GROUP = 8
SSD_GROUPS_PER_STEP = 4
PROMPT_Q_TILE = 256
FUSED_ATTN_VMEM_LIMIT_BYTES = 60 * 1024 * 1024
LOG2_E = math.log2(math.e)
NEG_INF = float("-inf")


def _cparams(sem):
    return pltpu.CompilerParams(dimension_semantics=sem, vmem_limit_bytes=VMEM_LIMIT_BYTES)


def _sigmoid(x):
    return 1.0 / (1.0 + jnp.exp(-x))


def _rms_rows(x, w):
    return x * lax.rsqrt(jnp.mean(x * x, axis=-1, keepdims=True) + EPS) * w


def _inproj_kernel(x_ref, nw_ref, w_ref, cos_ref, sa_ref, sb_ref,
                   q_ref, k_ref, v_ref, z_ref, xbc_ref, dt_ref, kt_ref=None):
    xn = _rms_rows(x_ref[...], nw_ref[...]).astype(BF16)
    cos, sa, sb = cos_ref[...], sa_ref[...], sb_ref[...]

    def rope(p):
        up = pltpu.roll(p, LANES - ROT_DIM // 2, axis=1)
        dn = pltpu.roll(p, ROT_DIM // 2, axis=1)
        return p * cos + up * sa + dn * sb

    col = 0
    for ref, width, rot in ((q_ref, Q_DIM, True), (k_ref, Q_DIM, True), (v_ref, D_ATT, False),
                            (z_ref, D_SSD, False), (xbc_ref, CONV_DIM, False), (dt_ref, SSD_HEADS, False)):
        for c0 in range(0, width, PROJ_CHUNK):
            cw = min(PROJ_CHUNK, width - c0)
            p = jnp.dot(xn, w_ref[:, col + c0:col + c0 + cw], preferred_element_type=F32)
            if rot:
                for s in range(0, cw, LANES):
                    roped = rope(p[:, s:s + LANES])
                    ref[:, c0 + s:c0 + s + LANES] = roped
                    if ref is k_ref and kt_ref is not None:
                        kt_ref[0, c0 + s:c0 + s + LANES, :] = roped.T
            else:
                ref[:, c0:c0 + cw] = p
        col += width


def _in_proj_tile(m):
    return min(256, m)


def _in_proj(x2d, nw, w_bf, cos_t, sa_t, sb_t, *, kt_seq_len=None):
    m = x2d.shape[0]
    tm = _in_proj_tile(m)
    table_tiles = cos_t.shape[0] // tm
    row = lambda i: (i, 0)
    trow = lambda i: (i % table_tiles, 0)
    const = lambda i: (0, 0)
    widths = (Q_DIM, Q_DIM, D_ATT, D_SSD, CONV_DIM, SSD_HEADS)
    out_specs = [pl.BlockSpec((tm, w), row) for w in widths]
    out_shape = [jax.ShapeDtypeStruct((m, w), F32) for w in widths]
    if kt_seq_len is not None:
        tiles_per_seq = kt_seq_len // tm
        out_specs.append(pl.BlockSpec((1, Q_DIM, tm), lambda i: (i // tiles_per_seq, 0, i % tiles_per_seq)))
        out_shape.append(jax.ShapeDtypeStruct((m // kt_seq_len, Q_DIM, kt_seq_len), F32))
    return pl.pallas_call(
        _inproj_kernel,
        grid=(m // tm,),
        in_specs=[pl.BlockSpec((tm, D_MODEL), row),
                  pl.BlockSpec((1, D_MODEL), const),
                  pl.BlockSpec((D_MODEL, IN_PROJ_DIM), const, pipeline_mode=pl.Buffered(1)),
                  pl.BlockSpec((tm, LANES), trow), pl.BlockSpec((tm, LANES), trow), pl.BlockSpec((tm, LANES), trow)],
        out_specs=out_specs,
        out_shape=out_shape,
        compiler_params=_cparams(("arbitrary",)),
        name="in_proj",
    )(x2d, nw, w_bf, cos_t, sa_t, sb_t)


def _rope_tables(pos):
    half = ROT_DIM // 2
    inv = ROPE_THETA ** (-jnp.arange(half, dtype=F32) / half)
    ang = pos.astype(F32)[:, None] * inv[None, :]
    cos, sin = jnp.cos(ang), jnp.sin(ang)
    rows = pos.shape[0]
    ones = jnp.ones((rows, ATT_DK - ROT_DIM), F32)
    zeros = jnp.zeros((rows, ATT_DK - ROT_DIM), F32)
    zh = jnp.zeros((rows, half), F32)
    cos64 = jnp.concatenate([cos, cos, ones], axis=1)
    sa64 = jnp.concatenate([-sin, zh, zeros], axis=1)
    sb64 = jnp.concatenate([zh, sin, zeros], axis=1)
    two = lambda t: jnp.concatenate([t, t], axis=1)
    return two(cos64), two(sa64), two(sb64)


def _lambda_full(lam_ref, lam_init):
    l = lam_ref[...]
    s1 = jnp.sum(l[0:1] * l[1:2], axis=1, keepdims=True)
    s2 = jnp.sum(l[2:3] * l[3:4], axis=1, keepdims=True)
    return jnp.exp(s1) - jnp.exp(s2) + lam_init


def _prompt_prepare(k_ref, v_ref, kbf, vt, tq, n_tiles=None):
    n_tiles = k_ref.shape[1] // tq if n_tiles is None else n_tiles
    for c in range(n_tiles):
        kbf[c * tq:(c + 1) * tq, :] = k_ref[0, c * tq:(c + 1) * tq, :].astype(BF16)
        vt[:, c * tq:(c + 1) * tq] = v_ref[0, c * tq:(c + 1) * tq, :].T.astype(BF16)


def _prompt_qtile(qi, lam, subw_ref, q_ref, o_ref, kbf, vt, s_scr, p_scr, tq, lam_init, out_pos=None, scr_off=0):
    out_pos = qi if out_pos is None else out_pos
    lane = lax.broadcasted_iota(jnp.int32, (tq, LANES), 1)
    krow = lax.broadcasted_iota(jnp.int32, (tq, 2 * tq), 0)
    qcol = lax.broadcasted_iota(jnp.int32, (tq, 2 * tq), 1)
    diag = krow <= jnp.where(qcol >= tq, qcol - tq, qcol)
    contract_last = (((1,), (1,)), ((), ()))
    q = q_ref[0, qi * tq:(qi + 1) * tq, :] * (ATT_DK ** -0.5 * LOG2_E)
    qq = jnp.concatenate([jnp.where(lane < ATT_DK, q, 0.0), jnp.where(lane >= ATT_DK, q, 0.0)],
                         axis=0).astype(BF16)
    m = None
    for c in range(qi + 1):
        s = lax.dot_general(kbf[c * tq:(c + 1) * tq, :], qq, contract_last, preferred_element_type=F32)
        if c == qi:
            s = jnp.where(diag, s, NEG_INF)
        s_scr[scr_off + c * tq:scr_off + (c + 1) * tq, :] = s
        cm = jnp.max(s, axis=0, keepdims=True)
        m = cm if m is None else jnp.maximum(m, cm)
    l = jnp.zeros((1, 2 * tq), F32)
    for c in range(qi + 1):
        p = jnp.exp2(s_scr[scr_off + c * tq:scr_off + (c + 1) * tq, :] - m)
        l = l + jnp.sum(p, axis=0, keepdims=True)
        p_scr[scr_off + c * tq:scr_off + (c + 1) * tq, :] = p.astype(BF16)
    kv = (qi + 1) * tq
    o = jnp.dot(vt[:, :kv], p_scr[scr_off:scr_off + kv, :], preferred_element_type=F32) / l
    att = o[:, :tq] - lam * o[:, tq:]
    ms = jnp.mean(att * att, axis=0, keepdims=True)
    y = att * lax.rsqrt(ms + EPS) * subw_ref[...] * (1.0 - lam_init)
    o_ref[0, out_pos * tq:(out_pos + 1) * tq, :] = y.T


def _attn_prompt_kernel(lam_ref, subw_ref, q_ref, k_ref, v_ref, o_ref, kbf, vt, s_scr, p_scr, *, tq, lam_init):
    lam = _lambda_full(lam_ref, lam_init)
    _prompt_prepare(k_ref, v_ref, kbf, vt, tq)
    for qi in range(q_ref.shape[1] // tq):
        _prompt_qtile(qi, lam, subw_ref, q_ref, o_ref, kbf, vt, s_scr, p_scr, tq, lam_init)


def _prompt_scratch(s, tq, score_rows=None):
    score_rows = s if score_rows is None else score_rows
    return [pltpu.VMEM((s, LANES), BF16), pltpu.VMEM((ATT_DV, s), BF16),
            pltpu.VMEM((score_rows, 2 * tq), F32), pltpu.VMEM((score_rows, 2 * tq), BF16)]


def _attn_prompt(lam_vecs, subw_col, q, k, v, lam_init):
    b, s, _ = q.shape
    tq = min(PROMPT_Q_TILE, s)
    kern = functools.partial(_attn_prompt_kernel, tq=tq, lam_init=lam_init)
    head = pl.BlockSpec((1, s, LANES), lambda bb, h: (bb, 0, h))
    return pl.pallas_call(
        kern,
        grid=(b, N_ATT_HEADS),
        in_specs=[pl.BlockSpec((4, ATT_DK), lambda bb, h: (0, 0)),
                  pl.BlockSpec((ATT_DV, 1), lambda bb, h: (0, 0)),
                  head, head, head],
        out_specs=head,
        out_shape=jax.ShapeDtypeStruct((b, s, D_ATT), F32),
        scratch_shapes=_prompt_scratch(s, tq),
        compiler_params=_cparams(("arbitrary", "arbitrary")),
        name="attn_prompt",
    )(lam_vecs, subw_col, q, k, v)


def _expansion_matrix():
    tok = lax.broadcasted_iota(jnp.int32, (PAGE_SIZE, PAGE_SIZE * N_ATT_HEADS), 0)
    row = lax.broadcasted_iota(jnp.int32, (PAGE_SIZE, PAGE_SIZE * N_ATT_HEADS), 1)
    return (row // N_ATT_HEADS == tok).astype(BF16)


def _sample_attend(lam_ref, subw_ref, rexp_ref, q_ref, kn_ref, vn_ref, kpages, vpages, o_ref, kbf, t_new, lam_init):
    n_pages = len(kpages)
    past = n_pages * PAGE_SIZE
    n_keys = past + PAGE_SIZE
    for j in range(n_pages):
        kbf[:, j * PAGE_SIZE:(j + 1) * PAGE_SIZE] = kpages[j][0].astype(BF16)
    zpad = jnp.zeros((PAGE_SIZE - t_new, Q_DIM), F32)
    k_tail = jnp.concatenate([kn_ref[0], zpad], axis=0).astype(BF16)

    lam = _lambda_full(lam_ref, lam_init)
    q = q_ref[0] * (ATT_DK ** -0.5)
    nr = t_new * N_ATT_HEADS
    qrep = jnp.concatenate([jnp.broadcast_to(q[t:t + 1], (N_ATT_HEADS, Q_DIM)) for t in range(t_new)], axis=0)
    row = lax.broadcasted_iota(jnp.int32, (nr, Q_DIM), 0)
    lane = lax.broadcasted_iota(jnp.int32, (nr, Q_DIM), 1)
    head = row % N_ATT_HEADS
    grp = lane // ATT_DK
    qq = jnp.concatenate([jnp.where(grp == 2 * head, qrep, 0.0), jnp.where(grp == 2 * head + 1, qrep, 0.0)],
                         axis=0).astype(BF16)
    s_past = jnp.dot(qq, kbf[...], preferred_element_type=F32)
    s_tail = lax.dot_general(qq, k_tail, (((1,), (1,)), ((), ())), preferred_element_type=F32)
    s = jnp.concatenate([s_past, s_tail], axis=1)
    srow = lax.broadcasted_iota(jnp.int32, (2 * nr, n_keys), 0)
    scol = lax.broadcasted_iota(jnp.int32, (2 * nr, n_keys), 1)
    tq = (srow // N_ATT_HEADS) % t_new
    s = jnp.where(scol <= past + tq, s, NEG_INF)
    m = jnp.max(s, axis=1, keepdims=True)
    p = jnp.exp(s - m)
    pn = p / jnp.sum(p, axis=1, keepdims=True)
    a = (pn[:nr] - lam * pn[nr:]).astype(BF16)
    a_pages = jnp.concatenate([a[:, j * PAGE_SIZE:(j + 1) * PAGE_SIZE] for j in range(n_pages + 1)], axis=0)
    a3 = jnp.dot(a_pages, rexp_ref[...], preferred_element_type=F32)
    hrow = lax.broadcasted_iota(jnp.int32, (N_ATT_HEADS, PAGE_SIZE * N_ATT_HEADS), 0)
    hcol = lax.broadcasted_iota(jnp.int32, (N_ATT_HEADS, PAGE_SIZE * N_ATT_HEADS), 1) % N_ATT_HEADS
    own = (hrow == hcol).astype(F32)
    a3 = (a3.reshape((n_pages + 1) * t_new, N_ATT_HEADS, PAGE_SIZE * N_ATT_HEADS) * own[None]).astype(BF16)
    a3 = a3.reshape((n_pages + 1) * nr, PAGE_SIZE * N_ATT_HEADS)
    v_tail = jnp.concatenate([vn_ref[0], jnp.zeros((PAGE_SIZE - nr, ATT_DV), F32)], axis=0)
    o = jnp.dot(a3[n_pages * nr:, :PAGE_SIZE], v_tail.astype(BF16), preferred_element_type=F32)
    for j in range(n_pages):
        o = o + jnp.dot(a3[j * nr:(j + 1) * nr], vpages[j][0].astype(BF16), preferred_element_type=F32)
    o_ref[0] = _rms_rows(o, subw_ref[...]) * (1.0 - lam_init)


def _attn_sample_kernel(pt_ref, lam_ref, subw_ref, q_ref, kn_ref, vn_ref, *rest, n_pages, t_new, lam_init):
    kpages, vpages = rest[:n_pages], rest[n_pages:2 * n_pages]
    rexp_ref, o_ref, kbf = rest[2 * n_pages:]
    _sample_attend(lam_ref, subw_ref, rexp_ref, q_ref, kn_ref, vn_ref, kpages, vpages, o_ref, kbf, t_new, lam_init)


def _sample_specs(page_table, t_new, seq_of_step=lambda s: s):
    n_pages = page_table.shape[1]
    tok = pl.BlockSpec((1, t_new, Q_DIM), lambda s, pt: (seq_of_step(s), 0, 0))
    rows = pl.BlockSpec((1, t_new * N_ATT_HEADS, ATT_DV), lambda s, pt: (seq_of_step(s), 0, 0))
    page_maps = [functools.partial(lambda s, pt, j: (pt[seq_of_step(s), j], 0, 0), j=j) for j in range(n_pages)]
    kpage_specs = [pl.BlockSpec((1, Q_DIM, PAGE_SIZE), pm) for pm in page_maps]
    vpage_specs = [pl.BlockSpec((1, PAGE_SIZE * N_ATT_HEADS, ATT_DV), pm) for pm in page_maps]
    in_specs = [pl.BlockSpec((4, ATT_DK), lambda b, pt: (0, 0)),
                pl.BlockSpec((1, ATT_DV), lambda b, pt: (0, 0)),
                tok, tok, rows] + kpage_specs + vpage_specs + [
                pl.BlockSpec((PAGE_SIZE, PAGE_SIZE * N_ATT_HEADS), lambda b, pt: (0, 0))]
    scratch = [pltpu.VMEM((Q_DIM, n_pages * PAGE_SIZE), BF16)]
    return in_specs, rows, scratch


def _attn_sample(page_table, lam_vecs, subw, q, kn, vn2, cache_k, cache_v, lam_init):
    db, t_new, _ = q.shape
    n_pages = page_table.shape[1]
    kern = functools.partial(_attn_sample_kernel, n_pages=n_pages, t_new=t_new, lam_init=lam_init)
    in_specs, out_spec, scratch = _sample_specs(page_table, t_new)
    grid_spec = pltpu.PrefetchScalarGridSpec(num_scalar_prefetch=1, grid=(db,), in_specs=in_specs,
                                             out_specs=out_spec, scratch_shapes=scratch)
    att = pl.pallas_call(
        kern,
        grid_spec=grid_spec,
        out_shape=jax.ShapeDtypeStruct((db, t_new * N_ATT_HEADS, ATT_DV), F32),
        compiler_params=_cparams(("arbitrary",)),
        name="attn_sample",
    )(page_table, lam_vecs, subw, q, kn, vn2, *([cache_k] * n_pages), *([cache_v] * n_pages), _expansion_matrix())
    return att.reshape(db, t_new, D_ATT)


def _fused_tile_plan(n_qtiles, steps_per_head):
    order = []
    lo, hi = 0, n_qtiles - 1
    while lo <= hi:
        order.append(lo)
        if hi != lo:
            order.append(hi)
        lo, hi = lo + 1, hi - 1
    per_step = n_qtiles // steps_per_head
    return [order[u * per_step:(u + 1) * per_step] for u in range(steps_per_head)]


def _attn_fused_kernel(pt_ref, lam_ref, subw_ref, q_ref, kn_ref, vn_ref, *rest, n_pages, t_new, lam_init, tq, plan):
    kpages, vpages = rest[:n_pages], rest[n_pages:2 * n_pages]
    (rexp_ref, subw_col_ref, qp_ref, kp_ref, vp_ref, o_ref, op_ref, kbf, kbf_p, vt_p, s_scr, p_scr) = rest[2 * n_pages:]
    _sample_attend(lam_ref, subw_ref, rexp_ref, q_ref, kn_ref, vn_ref, kpages, vpages, o_ref, kbf, t_new, lam_init)

    sub = pl.program_id(0) % len(plan)
    lam = _lambda_full(lam_ref, lam_init)

    @pl.when(sub == 0)
    def _():
        _prompt_prepare(kp_ref, vp_ref, kbf_p, vt_p, tq)

    for u, tiles in enumerate(plan):
        @pl.when(sub == u)
        def _(tiles=tiles):
            scr_off = 0
            for qi in tiles:
                _prompt_qtile(qi, lam, subw_col_ref, qp_ref, op_ref, kbf_p, vt_p, s_scr, p_scr, tq, lam_init,
                              scr_off=scr_off)
                scr_off += (qi + 1) * tq


def _attn_fused_ok(db, bp, sp):
    heads = bp * N_ATT_HEADS
    tq = min(PROMPT_Q_TILE, sp)
    return db % heads == 0 and (sp // tq) % (db // heads) == 0


def _attn_fused(page_table, lam_vecs, subw, subw_col, q, kn, vn2, cache_k, cache_v, qp, kp, vp, lam_init):
    db, t_new, _ = q.shape
    bp, sp, _ = qp.shape
    n_pages = page_table.shape[1]
    tq = min(PROMPT_Q_TILE, sp)
    steps_per_head = db // (bp * N_ATT_HEADS)
    plan = _fused_tile_plan(sp // tq, steps_per_head)
    kern = functools.partial(_attn_fused_kernel, n_pages=n_pages, t_new=t_new, lam_init=lam_init, tq=tq, plan=plan)
    in_specs, out_spec, scratch = _sample_specs(page_table, t_new)
    head = pl.BlockSpec((1, sp, LANES), lambda b, pt: (b // steps_per_head // N_ATT_HEADS, 0,
                                                       b // steps_per_head % N_ATT_HEADS))
    score_rows = max(sum(qi + 1 for qi in tiles) for tiles in plan) * tq
    grid_spec = pltpu.PrefetchScalarGridSpec(
        num_scalar_prefetch=1,
        grid=(db,),
        in_specs=in_specs + [pl.BlockSpec((ATT_DV, 1), lambda b, pt: (0, 0)), head, head, head],
        out_specs=[out_spec, head],
        scratch_shapes=scratch + _prompt_scratch(sp, tq, score_rows=score_rows),
    )
    att_s, att_p = pl.pallas_call(
        kern,
        grid_spec=grid_spec,
        out_shape=[jax.ShapeDtypeStruct((db, t_new * N_ATT_HEADS, ATT_DV), F32),
                   jax.ShapeDtypeStruct((bp, sp, D_ATT), F32)],
        compiler_params=pltpu.CompilerParams(dimension_semantics=("arbitrary",),
                                             vmem_limit_bytes=FUSED_ATTN_VMEM_LIMIT_BYTES),
        name="attn_fused",
    )(page_table, lam_vecs, subw, q, kn, vn2, *([cache_k] * n_pages), *([cache_v] * n_pages), _expansion_matrix(),
      subw_col, qp, kp, vp)
    return att_s.reshape(db, t_new, D_ATT), att_p


def _expand_heads(v, rows):
    return jnp.concatenate([jnp.broadcast_to(v[:, h:h + 1], (rows, SSD_HEADDIM)) for h in range(SSD_HEADS)], axis=1)


def _expand_heads_mxu(vals):
    rows = vals[0].shape[0]
    parts = []
    for v in vals:
        hi = v.astype(BF16).astype(F32)
        mid = (v - hi).astype(BF16).astype(F32)
        lo = v - hi - mid
        parts.append(jnp.concatenate([hi, mid, lo], axis=1))
    lhs = jnp.concatenate(parts, axis=0).astype(BF16)
    head = lax.broadcasted_iota(jnp.int32, (3 * SSD_HEADS, D_SSD), 0) % SSD_HEADS
    col = lax.broadcasted_iota(jnp.int32, (3 * SSD_HEADS, D_SSD), 1) // SSD_HEADDIM
    out = jnp.dot(lhs, (head == col).astype(BF16), preferred_element_type=F32)
    return [out[i * rows:(i + 1) * rows] for i in range(len(vals))]


def _ssd_kernel(xbc_ref, dt_ref, z_ref, h0_ref, cw_ref, cb_ref, dtb_ref, alog_ref, dskip_ref, nw_ref,
                y_ref, hout_ref, xprev, *, groups, seqs, seq_rows, valid_rows, carry):
    c = pl.program_id(1)

    @pl.when(c == 0)
    def _():
        hout_ref[...] = h0_ref[...]
        if carry:
            xprev[...] = jnp.zeros_like(xprev)

    for gi in range(groups):
        _ssd_group(gi, xbc_ref, dt_ref, z_ref, cw_ref, cb_ref, dtb_ref, alog_ref, dskip_ref, nw_ref,
                   y_ref, hout_ref, xprev, seqs=seqs, seq_rows=seq_rows, valid_rows=valid_rows, carry=carry)


def _ssd_group(gi, xbc_ref, dt_ref, z_ref, cw_ref, cb_ref, dtb_ref, alog_ref, dskip_ref, nw_ref,
               y_ref, hout_ref, xprev, *, seqs, seq_rows, valid_rows, carry):
    rows = seqs * seq_rows
    xr = xbc_ref[gi]
    conv = cb_ref[...] + cw_ref[SSD_CONV - 1:SSD_CONV] * xr
    if carry:
        pv = xprev[gi]
        row8 = lax.broadcasted_iota(jnp.int32, (SUBLANES, CONV_DIM), 0)
    for k in range(1, SSD_CONV):
        sh = pltpu.roll(xr, k, axis=0)
        if carry:
            top = jnp.where(row8 < k, pltpu.roll(pv, k, axis=0), sh[:SUBLANES])
            sh = jnp.concatenate([top, sh[SUBLANES:]], axis=0)
        conv = conv + cw_ref[SSD_CONV - 1 - k:SSD_CONV - k] * sh
    if carry:
        xprev[gi] = xr[rows - SUBLANES:]
    xc = conv * _sigmoid(conv)
    xs = xc[:, :D_SSD]

    dtv = dt_ref[gi] + dtb_ref[...]
    dtv = jnp.maximum(dtv, 0.0) + jnp.log1p(jnp.exp(-jnp.abs(dtv)))
    if valid_rows < seq_rows:
        rvalid = lax.broadcasted_iota(jnp.int32, (rows, SSD_HEADS), 0) % seq_rows < valid_rows
        dtv = jnp.where(rvalid, dtv, 0.0)
    a = dtv * (-jnp.exp(alog_ref[...]))
    ri = lax.broadcasted_iota(jnp.int32, (rows, rows), 0)
    ci = lax.broadcasted_iota(jnp.int32, (rows, rows), 1)
    tril = ri >= ci
    if seqs > 1:
        tril = tril & (ri // seq_rows == ci // seq_rows)
    a_cs = jnp.dot(tril.astype(F32), a, preferred_element_type=F32, precision=lax.Precision.HIGHEST)
    totals = [a_cs[(s + 1) * seq_rows - 1:(s + 1) * seq_rows] for s in range(seqs)]
    e_tots = [jnp.exp(t) for t in totals]
    total = totals[0] if seqs == 1 else jnp.concatenate(
        [jnp.broadcast_to(t, (seq_rows, SSD_HEADS)) for t in totals], axis=0)
    dt_exp, decay_exp, e_exp = _expand_heads_mxu([dtv, jnp.exp(total - a_cs), jnp.exp(a_cs)])
    xdt = xs * dt_exp
    xdtd = (xdt * decay_exp).astype(BF16)
    eye = ri == ci
    s0 = gi * seqs

    y_diag_parts, y_off_parts = [], []
    heads_per_group = SSD_HEADS // SSD_GROUPS
    gw = heads_per_group * SSD_HEADDIM
    contract_last = (((1,), (1,)), ((), ()))
    contract_first = (((0,), (0,)), ((), ()))
    for g in range(SSD_GROUPS):
        bg = xc[:, D_SSD + g * SSD_STATE:D_SSD + (g + 1) * SSD_STATE].astype(BF16)
        cg = xc[:, D_SSD + (SSD_GROUPS + g) * SSD_STATE:D_SSD + (SSD_GROUPS + g + 1) * SSD_STATE].astype(BF16)
        y_off_seq, st_seq = [], []
        for s in range(seqs):
            rs = slice(s * seq_rows, (s + 1) * seq_rows)
            hg = hout_ref[s0 + s, g * heads_per_group:(g + 1) * heads_per_group].reshape(gw, SSD_STATE)
            y_off_seq.append(lax.dot_general(cg[rs], hg.astype(BF16), contract_last, preferred_element_type=F32))
            st_seq.append(lax.dot_general(xdtd[rs, g * gw:(g + 1) * gw], bg[rs], contract_first,
                                          preferred_element_type=F32))
        y_off = y_off_seq[0] if seqs == 1 else jnp.concatenate(y_off_seq, axis=0)
        cbm = lax.dot_general(cg, bg, contract_last, preferred_element_type=F32)
        for r in range(heads_per_group):
            h = g * heads_per_group + r
            col = a_cs[:, h:h + 1]
            rowv = jnp.sum(jnp.where(eye, col, 0.0), axis=0, keepdims=True)
            lmat = jnp.where(tril, jnp.exp(col - rowv), 0.0)
            mm = (cbm * lmat).astype(BF16)
            y_diag_parts.append(jnp.dot(mm, xdt[:, h * SSD_HEADDIM:(h + 1) * SSD_HEADDIM].astype(BF16),
                                        preferred_element_type=F32))
            for s in range(seqs):
                hout_ref[s0 + s, h] = (hout_ref[s0 + s, h] * e_tots[s][:, h:h + 1]
                                       + st_seq[s][r * SSD_HEADDIM:(r + 1) * SSD_HEADDIM])
        y_off_parts.append(y_off)
    y = (jnp.concatenate(y_diag_parts, axis=1) + jnp.concatenate(y_off_parts, axis=1) * e_exp
         + _expand_heads(dskip_ref[...], 1) * xs)
    zz = z_ref[gi]
    y = y * (zz * _sigmoid(zz))
    half = D_SSD // SSD_GROUPS
    nw = nw_ref[...]
    y_ref[gi] = jnp.concatenate([_rms_rows(y[:, g * half:(g + 1) * half], nw[:, g * half:(g + 1) * half])
                                for g in range(SSD_GROUPS)], axis=1)


def _ssd(xbc, dt, z, h0, cw, cb, dtb, alog, dskip, nw, *, seqs, seq_rows, valid_rows):
    b, s, _ = xbc.shape
    rows = seqs * seq_rows
    n_chunks = s // rows
    assert seqs == 1 or n_chunks == 1
    groups = math.gcd(b, SSD_GROUPS_PER_STEP) if seqs == 1 else 1
    kern = functools.partial(_ssd_kernel, groups=groups, seqs=seqs, seq_rows=seq_rows, valid_rows=valid_rows,
                             carry=seqs == 1)
    tokmap = lambda bb, c: (bb, c, 0)
    seqmap4 = lambda bb, c: (bb, 0, 0, 0)
    const = lambda bb, c: (0, 0)
    state_spec = pl.BlockSpec((groups * seqs, SSD_HEADS, SSD_HEADDIM, SSD_STATE), seqmap4)
    return pl.pallas_call(
        kern,
        grid=(b // groups, n_chunks),
        in_specs=[pl.BlockSpec((groups, rows, CONV_DIM), tokmap),
                  pl.BlockSpec((groups, rows, SSD_HEADS), tokmap),
                  pl.BlockSpec((groups, rows, D_SSD), tokmap),
                  state_spec,
                  pl.BlockSpec((SSD_CONV, CONV_DIM), const),
                  pl.BlockSpec((1, CONV_DIM), const),
                  pl.BlockSpec((1, SSD_HEADS), const),
                  pl.BlockSpec((1, SSD_HEADS), const),
                  pl.BlockSpec((1, SSD_HEADS), const),
                  pl.BlockSpec((1, D_SSD), const)],
        out_specs=[pl.BlockSpec((groups, rows, D_SSD), tokmap), state_spec],
        out_shape=[jax.ShapeDtypeStruct((b, s, D_SSD), F32),
                   jax.ShapeDtypeStruct((b * seqs, SSD_HEADS, SSD_HEADDIM, SSD_STATE), F32)],
        scratch_shapes=[pltpu.VMEM((groups, SUBLANES, CONV_DIM), F32)],
        compiler_params=_cparams(("arbitrary", "arbitrary")),
        name="ssd",
    )(xbc, dt, z, h0, cw, cb, dtb, alog, dskip, nw)


def _outproj_kernel(x_ref, att_ref, y_ref, wa_ref, wy_ref, nw_ref, x1_ref, hf_ref):
    mix = jnp.dot(att_ref[...].astype(BF16), wa_ref[...], preferred_element_type=F32)
    mix = mix + jnp.dot(y_ref[...].astype(BF16), wy_ref[...], preferred_element_type=F32)
    x1 = x_ref[...] + mix
    x1_ref[...] = x1
    hf_ref[...] = _rms_rows(x1, nw_ref[...]).astype(BF16)


def _out_proj(x2d, att, y, wa, wy, nw):
    m = x2d.shape[0]
    tm = min(512, m)
    row = lambda i: (i, 0)
    const = lambda i: (0, 0)
    return pl.pallas_call(
        _outproj_kernel,
        grid=(m // tm,),
        in_specs=[pl.BlockSpec((tm, D_MODEL), row), pl.BlockSpec((tm, D_ATT), row), pl.BlockSpec((tm, D_SSD), row),
                  pl.BlockSpec((D_ATT, D_MODEL), const, pipeline_mode=pl.Buffered(1)),
                  pl.BlockSpec((D_SSD, D_MODEL), const, pipeline_mode=pl.Buffered(1)),
                  pl.BlockSpec((1, D_MODEL), const)],
        out_specs=[pl.BlockSpec((tm, D_MODEL), row), pl.BlockSpec((tm, D_MODEL), row)],
        out_shape=[jax.ShapeDtypeStruct((m, D_MODEL), F32), jax.ShapeDtypeStruct((m, D_MODEL), BF16)],
        compiler_params=_cparams(("arbitrary",)),
        name="out_proj",
    )(x2d, att, y, wa, wy, nw)


def _ffn_kernel(x1_ref, hf_ref, wg_ref, wu_ref, wd_ref, cw_ref, cb_ref, nfw_ref, *rest,
                tm, seq_len, tiles_per_seq):
    if seq_len >= tm:
        y_ref, gt_ref, acc, gprev = rest
    else:
        st_ref, y_ref, gt_ref, acc, p1_scr, p2_scr, g_scr = rest
    i = pl.program_id(0)
    f = pl.program_id(1)
    nf = pl.num_programs(1)

    @pl.when(f == 0)
    def _():
        acc[...] = jnp.zeros_like(acc)

    hf = hf_ref[...]
    g = jnp.dot(hf, wg_ref[...], preferred_element_type=F32)
    u = jnp.dot(hf, wu_ref[...], preferred_element_type=F32)
    tf = g.shape[1]
    if seq_len >= tm:
        gp = gprev[f]
        gp = jnp.where(i % tiles_per_seq == 0, 0.0, gp)
        row8 = lax.broadcasted_iota(jnp.int32, (SUBLANES, tf), 0)
        shifted = []
        for k in range(1, FFN_CONV):
            sh = pltpu.roll(g, k, axis=0)
            top = jnp.where(row8 < k, pltpu.roll(gp, k, axis=0), sh[:SUBLANES])
            shifted.append(jnp.concatenate([top, sh[SUBLANES:]], axis=0))
        g1, g2 = shifted
        gprev[f] = g[tm - SUBLANES:]
        gt_ref[0] = g[tm - SUBLANES:]
    else:
        ns = tm // seq_len

        @pl.when((i == 0) & (f == 0))
        def _():
            p1_scr[...] = jnp.zeros_like(p1_scr)
            p2_scr[...] = jnp.zeros_like(p2_scr)

        for c in range(tf // LANES):
            cs = slice(c * LANES, (c + 1) * LANES)
            g_scr[c] = g[:, cs]
            p1_scr[c, pl.ds(0, ns, stride=seq_len), :] = st_ref[1, :, cs]
            p2_scr[c, pl.ds(0, ns, stride=seq_len), :] = st_ref[0, :, cs]
            p2_scr[c, pl.ds(1, ns, stride=seq_len), :] = st_ref[1, :, cs]
            gt_ref[0, :, cs] = g_scr[c, pl.ds(seq_len - 2, ns, stride=seq_len), :]
            gt_ref[1, :, cs] = g_scr[c, pl.ds(seq_len - 1, ns, stride=seq_len), :]
        p1 = jnp.concatenate([p1_scr[c] for c in range(tf // LANES)], axis=1)
        p2 = jnp.concatenate([p2_scr[c] for c in range(tf // LANES)], axis=1)
        pos = lax.broadcasted_iota(jnp.int32, (tm, tf), 0) % seq_len
        g1 = jnp.where(pos >= 1, pltpu.roll(g, 1, axis=0), 0.0) + p1
        g2 = jnp.where(pos >= 2, pltpu.roll(g, 2, axis=0), 0.0) + p2
    gc = cb_ref[...] + cw_ref[0:1] * g2 + cw_ref[1:2] * g1 + cw_ref[2:3] * g
    act = (gc * _sigmoid(gc) * u).astype(BF16)
    acc[...] += jnp.dot(act, wd_ref[...], preferred_element_type=F32)

    @pl.when(f == nf - 1)
    def _():
        y_ref[...] = _rms_rows(x1_ref[...] + acc[...], nfw_ref[...])


def _ffn(x1, hf, wg, wu, wd, cw, cb, nfw, state, *, seq_len):
    m = x1.shape[0]
    tm = min(512, m)
    tf = 512
    nf = D_FF // tf
    prompt_mode = seq_len >= tm
    tiles_per_seq = max(seq_len // tm, 1)
    kern = functools.partial(_ffn_kernel, tm=tm, seq_len=seq_len, tiles_per_seq=tiles_per_seq)
    row = lambda i, f: (i, 0)
    in_specs = [pl.BlockSpec((tm, D_MODEL), row), pl.BlockSpec((tm, D_MODEL), row),
                pl.BlockSpec((D_MODEL, tf), lambda i, f: (0, f)), pl.BlockSpec((D_MODEL, tf), lambda i, f: (0, f)),
                pl.BlockSpec((tf, D_MODEL), lambda i, f: (f, 0)),
                pl.BlockSpec((FFN_CONV, tf), lambda i, f: (0, f)), pl.BlockSpec((1, tf), lambda i, f: (0, f)),
                pl.BlockSpec((1, D_MODEL), lambda i, f: (0, 0))]
    args = [x1, hf, wg, wu, wd, cw, cb, nfw]
    scratch = [pltpu.VMEM((tm, D_MODEL), F32)]
    if prompt_mode:
        out_specs = [pl.BlockSpec((tm, D_MODEL), row), pl.BlockSpec((1, SUBLANES, tf), lambda i, f: (i, 0, f))]
        out_shape = [jax.ShapeDtypeStruct((m, D_MODEL), F32), jax.ShapeDtypeStruct((m // tm, SUBLANES, D_FF), F32)]
        scratch.append(pltpu.VMEM((nf, SUBLANES, tf), F32))
    else:
        ns = tm // seq_len
        state_spec = pl.BlockSpec((FFN_CONV - 1, ns, tf), lambda i, f: (0, i, f))
        in_specs.append(state_spec)
        args.append(state)
        out_specs = [pl.BlockSpec((tm, D_MODEL), row), state_spec]
        out_shape = [jax.ShapeDtypeStruct((m, D_MODEL), F32), jax.ShapeDtypeStruct(state.shape, F32)]
        scratch += [pltpu.VMEM((tf // LANES, tm, LANES), F32)] * 3
    return pl.pallas_call(
        kern,
        grid=(m // tm, nf),
        in_specs=in_specs,
        out_specs=out_specs,
        out_shape=out_shape,
        scratch_shapes=scratch,
        compiler_params=_cparams(("arbitrary", "arbitrary")),
        name="ffn",
    )(*args)


def _layer(xp, xs, past, lam_init, p):
    (cache_k, cache_v, state_ssm, state_conv_ssd, state_conv_ffn, page_table) = past
    bp, sp, _ = xp.shape
    db, ds, _ = xs.shape
    n_pages = page_table.shape[1]
    past_len = n_pages * PAGE_SIZE

    w_in = p["w_in"].astype(BF16)
    nmw = p["norm_mix_w"].reshape(1, D_MODEL)
    lam_vecs = jnp.stack([p["lambda_q1"], p["lambda_k1"], p["lambda_q2"], p["lambda_k2"]])
    subw = p["subln_w"].reshape(1, ATT_DV)
    cw_ssd = p["conv_ssd_w"]
    cb_ssd = p["conv_ssd_b"].reshape(1, CONV_DIM)
    dtb = p["dt_bias"].reshape(1, SSD_HEADS)
    alog = p["a_log"].reshape(1, SSD_HEADS)
    dskip = p["d_skip"].reshape(1, SSD_HEADS)
    nsw = p["norm_ssd_w"].reshape(1, D_SSD)
    w_out = p["w_out"].astype(BF16)
    wa, wy = w_out[:D_ATT], w_out[D_ATT:]
    nfw = p["norm_ffn_w"].reshape(1, D_MODEL)
    wg, wu, wd = p["w_gate"].astype(BF16), p["w_up"].astype(BF16), p["w_down"].astype(BF16)
    cw_ffn = p["conv_ffn_w"]
    cb_ffn = p["conv_ffn_b"].reshape(1, D_FF)
    nfin = p["norm_final_w"].reshape(1, D_MODEL)

    q, k, v, z, xbc, dt, kt = _in_proj(xp.reshape(bp * sp, D_MODEL), nmw, w_in, *_rope_tables(jnp.arange(sp)),
                                       kt_seq_len=sp)
    r3 = lambda t: t.reshape(bp, sp, t.shape[-1])
    pos_s = jnp.tile(past_len + jnp.arange(ds), _in_proj_tile(db * ds) // ds)
    q_s, k_s, v_s, z_s, xbc_s, dt_s = _in_proj(xs.reshape(db * ds, D_MODEL), nmw, w_in, *_rope_tables(pos_s))
    r3s = lambda t: t.reshape(db, ds, t.shape[-1])
    n_phys = cache_k.shape[0]
    cache_kt = jnp.transpose(cache_k, (0, 2, 3, 4, 1)).reshape(n_phys, Q_DIM, PAGE_SIZE)
    cache_vr = cache_v.reshape(n_phys, PAGE_SIZE * N_ATT_HEADS, ATT_DV)
    subw_col = subw.reshape(ATT_DV, 1)
    vn2 = v_s.reshape(db, ds * N_ATT_HEADS, ATT_DV)
    if _attn_fused_ok(db, bp, sp):
        att_s, att = _attn_fused(page_table, lam_vecs, subw, subw_col, r3s(q_s), r3s(k_s), vn2,
                                 cache_kt, cache_vr, r3(q), r3(k), r3(v), lam_init)
    else:
        att = _attn_prompt(lam_vecs, subw_col, r3(q), r3(k), r3(v), lam_init)
        att_s = _attn_sample(page_table, lam_vecs, subw, r3s(q_s), r3s(k_s), vn2, cache_kt, cache_vr, lam_init)

    rows = SSD_CHUNK if sp % SSD_CHUNK == 0 else sp
    y, ssm_p = _ssd(r3(xbc), r3(dt), r3(z), jnp.zeros((bp, SSD_HEADS, SSD_HEADDIM, SSD_STATE), F32),
                    cw_ssd, cb_ssd, dtb, alog, dskip, nsw, seqs=1, seq_rows=rows, valid_rows=rows)
    x1, hf = _out_proj(xp.reshape(bp * sp, D_MODEL), att.reshape(bp * sp, D_ATT), y.reshape(bp * sp, D_SSD), wa, wy, nfw)
    yp, gtail = _ffn(x1, hf, wg, wu, wd, cw_ffn, cb_ffn, nfin, None, seq_len=sp)
    tiles_per_seq = gtail.shape[0] // bp
    conv_ffn_p = gtail.reshape(bp, tiles_per_seq, SUBLANES, D_FF)[:, -1, SUBLANES - (FFN_CONV - 1):]
    new_k_p = jnp.transpose(kt.reshape(bp, N_ATT_HEADS, 2, ATT_DK, sp), (0, 4, 1, 2, 3))
    prompt_out = (yp.reshape(bp, sp, D_MODEL), new_k_p, v.reshape(bp, sp, N_ATT_HEADS, ATT_DV),
                  ssm_p, r3(xbc)[:, sp - (SSD_CONV - 1):], conv_ffn_p)

    k, v, z, xbc, dt, att = k_s, v_s, z_s, xbc_s, dt_s, att_s
    gs = math.gcd(db, SSD_SAMPLE_GROUP)
    pad8 = lambda t: jnp.pad(t, ((0, 0), (0, SUBLANES - ds), (0, 0))).reshape(db // gs, gs * SUBLANES, t.shape[-1])
    nxt_state = jnp.roll(state_conv_ssd.reshape(db // gs, gs, SSD_CONV - 1, CONV_DIM), -1, axis=1)
    xe = jnp.concatenate([r3s(xbc), jnp.zeros((db, SUBLANES - ds - (SSD_CONV - 1), CONV_DIM), F32),
                          nxt_state.reshape(db, SSD_CONV - 1, CONV_DIM)], axis=1)
    y8, ssm_s = _ssd(xe.reshape(db // gs, gs * SUBLANES, CONV_DIM), pad8(r3s(dt)), pad8(r3s(z)), state_ssm,
                     cw_ssd, cb_ssd, dtb, alog, dskip, nsw, seqs=gs, seq_rows=SUBLANES, valid_rows=ds)
    y = y8.reshape(db, SUBLANES, D_SSD)[:, :ds]
    x1, hf = _out_proj(xs.reshape(db * ds, D_MODEL), att.reshape(db * ds, D_ATT), y.reshape(db * ds, D_SSD), wa, wy, nfw)
    ys, conv_ffn_s = _ffn(x1, hf, wg, wu, wd, cw_ffn, cb_ffn, nfin, jnp.transpose(state_conv_ffn, (1, 0, 2)),
                          seq_len=ds)
    sample_out = (ys.reshape(db, ds, D_MODEL),
                  k.reshape(db, ds, N_ATT_HEADS, 2, ATT_DK), v.reshape(db, ds, N_ATT_HEADS, ATT_DV),
                  ssm_s, r3s(xbc)[:, ds - (SSD_CONV - 1):], jnp.transpose(conv_ffn_s, (1, 0, 2)))
    return prompt_out, sample_out


def kernel(x_prompt, x_sample, cache_k, cache_v, state_ssm, state_conv_ssd, state_conv_ffn, page_table, norm_mix_w, w_in, lambda_q1, lambda_k1, lambda_q2, lambda_k2, subln_w, conv_ssd_w, conv_ssd_b, dt_bias, a_log, d_skip, norm_ssd_w, w_out, norm_ffn_w, w_gate, w_up, conv_ffn_w, conv_ffn_b, w_down, norm_final_w):
    depth = w_in.shape[0]
    assert depth == 1, "the final RMSNorm is fused into the (single) layer's FFN kernel"
    lam_init = 0.8 - 0.6 * math.exp(-0.3 * 0)
    params = dict(norm_mix_w=norm_mix_w[0], w_in=w_in[0], lambda_q1=lambda_q1[0], lambda_k1=lambda_k1[0],
                  lambda_q2=lambda_q2[0], lambda_k2=lambda_k2[0], subln_w=subln_w[0], conv_ssd_w=conv_ssd_w[0],
                  conv_ssd_b=conv_ssd_b[0], dt_bias=dt_bias[0], a_log=a_log[0], d_skip=d_skip[0],
                  norm_ssd_w=norm_ssd_w[0], w_out=w_out[0], norm_ffn_w=norm_ffn_w[0], w_gate=w_gate[0],
                  w_up=w_up[0], conv_ffn_w=conv_ffn_w[0], conv_ffn_b=conv_ffn_b[0], w_down=w_down[0],
                  norm_final_w=norm_final_w)
    past = (cache_k[0], cache_v[0], state_ssm[0], state_conv_ssd[0], state_conv_ffn[0], page_table)
    (yp, kp, vp, sp_, cp, fp), (ys, ks, vs, ss, cs, fs) = _layer(x_prompt, x_sample, past, lam_init, params)
    lead = lambda t: t[None]
    return (yp, ys, lead(kp), lead(vp), lead(sp_), lead(cp), lead(fp),
            lead(ks), lead(vs), lead(ss), lead(cs), lead(fs))
```

```python
import functools
import math

import jax
import jax.numpy as jnp
from jax import lax
from jax.experimental import pallas as pl
from jax.experimental.pallas import tpu as pltpu

F32 = jnp.float32
BF16 = jnp.bfloat16

D_MODEL = 2048
ATT_DK = 64
ATT_DV = 128
N_ATT_HEADS = 8
ROT_DIM = 16
ROPE_THETA = 500000.0
Q_DIM = N_ATT_HEADS * 2 * ATT_DK
D_ATT = N_ATT_HEADS * ATT_DV
D_SSD = 1024
SSD_HEADDIM = 64
SSD_HEADS = 16
SSD_GROUPS = 2
SSD_STATE = 128
SSD_CONV = 4
SSD_CHUNK = 128
CONV_DIM = D_SSD + 2 * SSD_GROUPS * SSD_STATE
D_FF = 5632
FFN_CONV = 3
EPS = 1e-6
PAGE_SIZE = 128

LANES = 128
SUBLANES = 8
VMEM_LIMIT_BYTES = 56 * 1024 * 1024
IN_PROJ_DIM = 2 * Q_DIM + D_ATT + D_SSD + CONV_DIM + SSD_HEADS
PROJ_CHUNK = 512
SSD_SAMPLE_GROUP = 8
SSD_GROUPS_PER_STEP = 4
PROMPT_Q_TILE = 256
FFN_TILE = 512
FFN_ROWS = 512
FUSED_ATTN_VMEM_LIMIT_BYTES = 60 * 1024 * 1024
LOG2_E = math.log2(math.e)
NEG_INF = float("-inf")


def _cparams(sem):
    return pltpu.CompilerParams(dimension_semantics=sem, vmem_limit_bytes=VMEM_LIMIT_BYTES)


def _sigmoid(x):
    return 1.0 / (1.0 + jnp.exp(-x))


def _rms_rows(x, w):
    return x * lax.rsqrt(jnp.mean(x * x, axis=-1, keepdims=True) + EPS) * w


def _cast_specs(arrays, n_steps):
    specs = [pl.BlockSpec((a.shape[0] // n_steps, a.shape[1]), lambda i, *_: (i, 0)) for a in arrays]
    shapes = [jax.ShapeDtypeStruct(a.shape, BF16) for a in arrays]
    return specs, shapes


def _inproj_kernel(x_ref, nw_ref, w_ref, cos_ref, sa_ref, sb_ref, *rest, n_cast, with_kt):
    cast_in, rest = rest[:n_cast], rest[n_cast:]
    q_ref, k_ref, v_ref, z_ref, xbc_ref, dt_ref = rest[:6]
    kt_ref = rest[6] if with_kt else None
    cast_out = rest[len(rest) - n_cast:]
    for src, dst in zip(cast_in, cast_out):
        dst[...] = src[...].astype(BF16)
    xn = _rms_rows(x_ref[...], nw_ref[...]).astype(BF16)
    cos, sa, sb = cos_ref[...], sa_ref[...], sb_ref[...]

    def rope(p):
        up = pltpu.roll(p, LANES - ROT_DIM // 2, axis=1)
        dn = pltpu.roll(p, ROT_DIM // 2, axis=1)
        return p * cos + up * sa + dn * sb

    col = 0
    for ref, width, rot in ((q_ref, Q_DIM, True), (k_ref, Q_DIM, True), (v_ref, D_ATT, False),
                            (z_ref, D_SSD, False), (xbc_ref, CONV_DIM, False), (dt_ref, SSD_HEADS, False)):
        for c0 in range(0, width, PROJ_CHUNK):
            cw = min(PROJ_CHUNK, width - c0)
            p = jnp.dot(xn, w_ref[:, col + c0:col + c0 + cw], preferred_element_type=F32)
            if rot:
                for s in range(0, cw, LANES):
                    roped = rope(p[:, s:s + LANES])
                    ref[:, c0 + s:c0 + s + LANES] = roped
                    if ref is k_ref and kt_ref is not None:
                        kt_ref[0, c0 + s:c0 + s + LANES, :] = roped.T
            else:
                ref[:, c0:c0 + cw] = p
        col += width


def _in_proj_tile(m):
    return min(256, m)


def _in_proj(x2d, nw, w_bf, cos_t, sa_t, sb_t, *, kt_seq_len=None, cast=()):
    m = x2d.shape[0]
    tm = _in_proj_tile(m)
    table_tiles = cos_t.shape[0] // tm
    row = lambda i: (i, 0)
    trow = lambda i: (i % table_tiles, 0)
    const = lambda i: (0, 0)
    widths = (Q_DIM, Q_DIM, D_ATT, D_SSD, CONV_DIM, SSD_HEADS)
    out_specs = [pl.BlockSpec((tm, w), row) for w in widths]
    out_shape = [jax.ShapeDtypeStruct((m, w), F32) for w in widths]
    if kt_seq_len is not None:
        tiles_per_seq = kt_seq_len // tm
        out_specs.append(pl.BlockSpec((1, Q_DIM, tm), lambda i: (i // tiles_per_seq, 0, i % tiles_per_seq)))
        out_shape.append(jax.ShapeDtypeStruct((m // kt_seq_len, Q_DIM, kt_seq_len), F32))
    cast_specs, cast_shapes = _cast_specs(cast, m // tm)
    kern = functools.partial(_inproj_kernel, n_cast=len(cast), with_kt=kt_seq_len is not None)
    return pl.pallas_call(
        kern,
        grid=(m // tm,),
        in_specs=[pl.BlockSpec((tm, D_MODEL), row),
                  pl.BlockSpec((1, D_MODEL), const),
                  pl.BlockSpec((D_MODEL, IN_PROJ_DIM), const, pipeline_mode=pl.Buffered(1)),
                  pl.BlockSpec((tm, LANES), trow), pl.BlockSpec((tm, LANES), trow), pl.BlockSpec((tm, LANES), trow)]
                 + cast_specs,
        out_specs=out_specs + cast_specs,
        out_shape=out_shape + cast_shapes,
        compiler_params=_cparams(("arbitrary",)),
        name="in_proj",
    )(x2d, nw, w_bf, cos_t, sa_t, sb_t, *cast)


def _rope_tables(pos):
    half = ROT_DIM // 2
    d = jnp.arange(LANES) % ATT_DK
    inv = ROPE_THETA ** (-(d % half).astype(F32) / half)
    ang = pos.astype(F32)[:, None] * inv[None, :]
    cos, sin = jnp.cos(ang), jnp.sin(ang)
    first, second = (d < half)[None, :], ((d >= half) & (d < ROT_DIM))[None, :]
    return (jnp.where(first | second, cos, 1.0), jnp.where(first, -sin, 0.0), jnp.where(second, sin, 0.0))


def _lambda_full(lam_ref, lam_init):
    l = lam_ref[...]
    s1 = jnp.sum(l[0:1] * l[1:2], axis=1, keepdims=True)
    s2 = jnp.sum(l[2:3] * l[3:4], axis=1, keepdims=True)
    return jnp.exp(s1) - jnp.exp(s2) + lam_init


def _prompt_prepare(k_ref, v_ref, kbf, vt, tq, n_tiles=None):
    n_tiles = k_ref.shape[1] // tq if n_tiles is None else n_tiles
    for c in range(n_tiles):
        kbf[c * tq:(c + 1) * tq, :] = k_ref[0, c * tq:(c + 1) * tq, :].astype(BF16)
        vt[:, c * tq:(c + 1) * tq] = v_ref[0, c * tq:(c + 1) * tq, :].T.astype(BF16)


def _prompt_qtile(qi, lam, subw_ref, q_ref, o_ref, kbf, vt, s_scr, p_scr, tq, lam_init, out_pos=None, scr_off=0):
    out_pos = qi if out_pos is None else out_pos
    lane = lax.broadcasted_iota(jnp.int32, (tq, LANES), 1)
    krow = lax.broadcasted_iota(jnp.int32, (tq, 2 * tq), 0)
    qcol = lax.broadcasted_iota(jnp.int32, (tq, 2 * tq), 1)
    diag = krow <= jnp.where(qcol >= tq, qcol - tq, qcol)
    contract_last = (((1,), (1,)), ((), ()))
    q = q_ref[0, qi * tq:(qi + 1) * tq, :] * (ATT_DK ** -0.5 * LOG2_E)
    qq = jnp.concatenate([jnp.where(lane < ATT_DK, q, 0.0), jnp.where(lane >= ATT_DK, q, 0.0)],
                         axis=0).astype(BF16)
    m = None
    for c in range(qi + 1):
        s = lax.dot_general(kbf[c * tq:(c + 1) * tq, :], qq, contract_last, preferred_element_type=F32)
        if c == qi:
            s = jnp.where(diag, s, NEG_INF)
        s_scr[scr_off + c * tq:scr_off + (c + 1) * tq, :] = s
        cm = jnp.max(s, axis=0, keepdims=True)
        m = cm if m is None else jnp.maximum(m, cm)
    l = jnp.zeros((1, 2 * tq), F32)
    for c in range(qi + 1):
        p = jnp.exp2(s_scr[scr_off + c * tq:scr_off + (c + 1) * tq, :] - m)
        l = l + jnp.sum(p, axis=0, keepdims=True)
        p_scr[scr_off + c * tq:scr_off + (c + 1) * tq, :] = p.astype(BF16)
    kv = (qi + 1) * tq
    o = jnp.dot(vt[:, :kv], p_scr[scr_off:scr_off + kv, :], preferred_element_type=F32) / l
    att = o[:, :tq] - lam * o[:, tq:]
    ms = jnp.mean(att * att, axis=0, keepdims=True)
    y = att * lax.rsqrt(ms + EPS) * subw_ref[...] * (1.0 - lam_init)
    o_ref[0, out_pos * tq:(out_pos + 1) * tq, :] = y.T


def _attn_prompt_kernel(lam_ref, subw_ref, q_ref, k_ref, v_ref, o_ref, kbf, vt, s_scr, p_scr, *, tq, lam_init):
    lam = _lambda_full(lam_ref, lam_init)
    _prompt_prepare(k_ref, v_ref, kbf, vt, tq)
    for qi in range(q_ref.shape[1] // tq):
        _prompt_qtile(qi, lam, subw_ref, q_ref, o_ref, kbf, vt, s_scr, p_scr, tq, lam_init)


def _prompt_scratch(s, tq, score_rows=None):
    score_rows = s if score_rows is None else score_rows
    return [pltpu.VMEM((s, LANES), BF16), pltpu.VMEM((ATT_DV, s), BF16),
            pltpu.VMEM((score_rows, 2 * tq), F32), pltpu.VMEM((score_rows, 2 * tq), BF16)]


def _attn_prompt(lam_vecs, subw_col, q, k, v, lam_init):
    b, s, _ = q.shape
    tq = min(PROMPT_Q_TILE, s)
    kern = functools.partial(_attn_prompt_kernel, tq=tq, lam_init=lam_init)
    head = pl.BlockSpec((1, s, LANES), lambda bb, h: (bb, 0, h))
    return pl.pallas_call(
        kern,
        grid=(b, N_ATT_HEADS),
        in_specs=[pl.BlockSpec((4, ATT_DK), lambda bb, h: (0, 0)),
                  pl.BlockSpec((ATT_DV, 1), lambda bb, h: (0, 0)),
                  head, head, head],
        out_specs=head,
        out_shape=jax.ShapeDtypeStruct((b, s, D_ATT), F32),
        scratch_shapes=_prompt_scratch(s, tq),
        compiler_params=_cparams(("arbitrary", "arbitrary")),
        name="attn_prompt",
    )(lam_vecs, subw_col, q, k, v)


def _expansion_matrix():
    tok = lax.broadcasted_iota(jnp.int32, (PAGE_SIZE, PAGE_SIZE * N_ATT_HEADS), 0)
    row = lax.broadcasted_iota(jnp.int32, (PAGE_SIZE, PAGE_SIZE * N_ATT_HEADS), 1)
    return (row // N_ATT_HEADS == tok).astype(BF16)


def _sample_attend(lam_ref, subw_ref, rexp_ref, q_ref, kn_ref, vn_ref, kpages, vpages, o_ref, kbf, t_new, lam_init):
    n_pages = len(kpages)
    past = n_pages * PAGE_SIZE
    n_keys = past + PAGE_SIZE
    for j in range(n_pages):
        kbf[:, j * PAGE_SIZE:(j + 1) * PAGE_SIZE] = kpages[j][0].astype(BF16)
    zpad = jnp.zeros((PAGE_SIZE - t_new, Q_DIM), F32)
    k_tail = jnp.concatenate([kn_ref[0], zpad], axis=0).astype(BF16)

    lam = _lambda_full(lam_ref, lam_init)
    q = q_ref[0] * (ATT_DK ** -0.5)
    nr = t_new * N_ATT_HEADS
    qrep = jnp.concatenate([jnp.broadcast_to(q[t:t + 1], (N_ATT_HEADS, Q_DIM)) for t in range(t_new)], axis=0)
    row = lax.broadcasted_iota(jnp.int32, (nr, Q_DIM), 0)
    lane = lax.broadcasted_iota(jnp.int32, (nr, Q_DIM), 1)
    head = row % N_ATT_HEADS
    grp = lane // ATT_DK
    qq = jnp.concatenate([jnp.where(grp == 2 * head, qrep, 0.0), jnp.where(grp == 2 * head + 1, qrep, 0.0)],
                         axis=0).astype(BF16)
    s_past = jnp.dot(qq, kbf[...], preferred_element_type=F32)
    s_tail = lax.dot_general(qq, k_tail, (((1,), (1,)), ((), ())), preferred_element_type=F32)
    s = jnp.concatenate([s_past, s_tail], axis=1)
    srow = lax.broadcasted_iota(jnp.int32, (2 * nr, n_keys), 0)
    scol = lax.broadcasted_iota(jnp.int32, (2 * nr, n_keys), 1)
    tq = (srow // N_ATT_HEADS) % t_new
    s = jnp.where(scol <= past + tq, s, NEG_INF)
    m = jnp.max(s, axis=1, keepdims=True)
    p = jnp.exp(s - m)
    pn = p / jnp.sum(p, axis=1, keepdims=True)
    a = (pn[:nr] - lam * pn[nr:]).astype(BF16)
    a_pages = jnp.concatenate([a[:, j * PAGE_SIZE:(j + 1) * PAGE_SIZE] for j in range(n_pages + 1)], axis=0)
    a3 = jnp.dot(a_pages, rexp_ref[...], preferred_element_type=F32)
    hrow = lax.broadcasted_iota(jnp.int32, (N_ATT_HEADS, PAGE_SIZE * N_ATT_HEADS), 0)
    hcol = lax.broadcasted_iota(jnp.int32, (N_ATT_HEADS, PAGE_SIZE * N_ATT_HEADS), 1) % N_ATT_HEADS
    own = (hrow == hcol).astype(F32)
    a3 = (a3.reshape((n_pages + 1) * t_new, N_ATT_HEADS, PAGE_SIZE * N_ATT_HEADS) * own[None]).astype(BF16)
    a3 = a3.reshape((n_pages + 1) * nr, PAGE_SIZE * N_ATT_HEADS)
    v_tail = jnp.concatenate([vn_ref[0], jnp.zeros((PAGE_SIZE - nr, ATT_DV), F32)], axis=0)
    o = jnp.dot(a3[n_pages * nr:, :PAGE_SIZE], v_tail.astype(BF16), preferred_element_type=F32)
    for j in range(n_pages):
        o = o + jnp.dot(a3[j * nr:(j + 1) * nr], vpages[j][0].astype(BF16), preferred_element_type=F32)
    o_ref[0] = _rms_rows(o, subw_ref[...]) * (1.0 - lam_init)


def _attn_sample_kernel(pt_ref, lam_ref, subw_ref, q_ref, kn_ref, vn_ref, *rest, n_pages, t_new, lam_init):
    kpages, vpages = rest[:n_pages], rest[n_pages:2 * n_pages]
    rexp_ref, o_ref, kbf = rest[2 * n_pages:]
    _sample_attend(lam_ref, subw_ref, rexp_ref, q_ref, kn_ref, vn_ref, kpages, vpages, o_ref, kbf, t_new, lam_init)


def _sample_specs(page_table, t_new, seq_of_step=lambda s: s):
    n_pages = page_table.shape[1]
    tok = pl.BlockSpec((1, t_new, Q_DIM), lambda s, pt: (seq_of_step(s), 0, 0))
    rows = pl.BlockSpec((1, t_new * N_ATT_HEADS, ATT_DV), lambda s, pt: (seq_of_step(s), 0, 0))
    page_maps = [functools.partial(lambda s, pt, j: (pt[seq_of_step(s), j], 0, 0), j=j) for j in range(n_pages)]
    kpage_specs = [pl.BlockSpec((1, Q_DIM, PAGE_SIZE), pm) for pm in page_maps]
    vpage_specs = [pl.BlockSpec((1, PAGE_SIZE * N_ATT_HEADS, ATT_DV), pm) for pm in page_maps]
    in_specs = [pl.BlockSpec((4, ATT_DK), lambda b, pt: (0, 0)),
                pl.BlockSpec((1, ATT_DV), lambda b, pt: (0, 0)),
                tok, tok, rows] + kpage_specs + vpage_specs + [
                pl.BlockSpec((PAGE_SIZE, PAGE_SIZE * N_ATT_HEADS), lambda b, pt: (0, 0))]
    scratch = [pltpu.VMEM((Q_DIM, n_pages * PAGE_SIZE), BF16)]
    return in_specs, rows, scratch


def _attn_sample(page_table, lam_vecs, subw, q, kn, vn2, cache_k, cache_v, lam_init):
    db, t_new, _ = q.shape
    n_pages = page_table.shape[1]
    kern = functools.partial(_attn_sample_kernel, n_pages=n_pages, t_new=t_new, lam_init=lam_init)
    in_specs, out_spec, scratch = _sample_specs(page_table, t_new)
    grid_spec = pltpu.PrefetchScalarGridSpec(num_scalar_prefetch=1, grid=(db,), in_specs=in_specs,
                                             out_specs=out_spec, scratch_shapes=scratch)
    att = pl.pallas_call(
        kern,
        grid_spec=grid_spec,
        out_shape=jax.ShapeDtypeStruct((db, t_new * N_ATT_HEADS, ATT_DV), F32),
        compiler_params=_cparams(("arbitrary",)),
        name="attn_sample",
    )(page_table, lam_vecs, subw, q, kn, vn2, *([cache_k] * n_pages), *([cache_v] * n_pages), _expansion_matrix())
    return att.reshape(db, t_new, D_ATT)


def _fused_tile_plan(n_qtiles, steps_per_head):
    order = []
    lo, hi = 0, n_qtiles - 1
    while lo <= hi:
        order.append(lo)
        if hi != lo:
            order.append(hi)
        lo, hi = lo + 1, hi - 1
    per_step = n_qtiles // steps_per_head
    return [order[u * per_step:(u + 1) * per_step] for u in range(steps_per_head)]


def _attn_fused_kernel(pt_ref, lam_ref, subw_ref, q_ref, kn_ref, vn_ref, *rest, n_pages, t_new, lam_init, tq, plan):
    kpages, vpages = rest[:n_pages], rest[n_pages:2 * n_pages]
    (rexp_ref, subw_col_ref, qp_ref, kp_ref, vp_ref, o_ref, op_ref, kbf, kbf_p, vt_p, s_scr, p_scr) = rest[2 * n_pages:]
    _sample_attend(lam_ref, subw_ref, rexp_ref, q_ref, kn_ref, vn_ref, kpages, vpages, o_ref, kbf, t_new, lam_init)

    sub = pl.program_id(0) % len(plan)
    lam = _lambda_full(lam_ref, lam_init)

    @pl.when(sub == 0)
    def _():
        _prompt_prepare(kp_ref, vp_ref, kbf_p, vt_p, tq)

    for u, tiles in enumerate(plan):
        @pl.when(sub == u)
        def _(tiles=tiles):
            scr_off = 0
            for qi in tiles:
                _prompt_qtile(qi, lam, subw_col_ref, qp_ref, op_ref, kbf_p, vt_p, s_scr, p_scr, tq, lam_init,
                              scr_off=scr_off)
                scr_off += (qi + 1) * tq


def _attn_fused_ok(db, bp, sp):
    heads = bp * N_ATT_HEADS
    tq = min(PROMPT_Q_TILE, sp)
    return db % heads == 0 and (sp // tq) % (db // heads) == 0


def _attn_fused(page_table, lam_vecs, subw, subw_col, q, kn, vn2, cache_k, cache_v, qp, kp, vp, lam_init):
    db, t_new, _ = q.shape
    bp, sp, _ = qp.shape
    n_pages = page_table.shape[1]
    tq = min(PROMPT_Q_TILE, sp)
    steps_per_head = db // (bp * N_ATT_HEADS)
    plan = _fused_tile_plan(sp // tq, steps_per_head)
    kern = functools.partial(_attn_fused_kernel, n_pages=n_pages, t_new=t_new, lam_init=lam_init, tq=tq, plan=plan)
    in_specs, out_spec, scratch = _sample_specs(page_table, t_new)
    head = pl.BlockSpec((1, sp, LANES), lambda b, pt: (b // steps_per_head // N_ATT_HEADS, 0,
                                                       b // steps_per_head % N_ATT_HEADS))
    score_rows = max(sum(qi + 1 for qi in tiles) for tiles in plan) * tq
    grid_spec = pltpu.PrefetchScalarGridSpec(
        num_scalar_prefetch=1,
        grid=(db,),
        in_specs=in_specs + [pl.BlockSpec((ATT_DV, 1), lambda b, pt: (0, 0)), head, head, head],
        out_specs=[out_spec, head],
        scratch_shapes=scratch + _prompt_scratch(sp, tq, score_rows=score_rows),
    )
    att_s, att_p = pl.pallas_call(
        kern,
        grid_spec=grid_spec,
        out_shape=[jax.ShapeDtypeStruct((db, t_new * N_ATT_HEADS, ATT_DV), F32),
                   jax.ShapeDtypeStruct((bp, sp, D_ATT), F32)],
        compiler_params=pltpu.CompilerParams(dimension_semantics=("arbitrary",),
                                             vmem_limit_bytes=FUSED_ATTN_VMEM_LIMIT_BYTES),
        name="attn_fused",
    )(page_table, lam_vecs, subw, q, kn, vn2, *([cache_k] * n_pages), *([cache_v] * n_pages), _expansion_matrix(),
      subw_col, qp, kp, vp)
    return att_s.reshape(db, t_new, D_ATT), att_p


def _expand_heads(v, rows):
    return jnp.concatenate([jnp.broadcast_to(v[:, h:h + 1], (rows, SSD_HEADDIM)) for h in range(SSD_HEADS)], axis=1)


def _expand_heads_mxu(vals):
    rows = vals[0].shape[0]
    parts = []
    for v in vals:
        hi = v.astype(BF16).astype(F32)
        mid = (v - hi).astype(BF16).astype(F32)
        lo = v - hi - mid
        parts.append(jnp.concatenate([hi, mid, lo], axis=1))
    lhs = jnp.concatenate(parts, axis=0).astype(BF16)
    head = lax.broadcasted_iota(jnp.int32, (3 * SSD_HEADS, D_SSD), 0) % SSD_HEADS
    col = lax.broadcasted_iota(jnp.int32, (3 * SSD_HEADS, D_SSD), 1) // SSD_HEADDIM
    out = jnp.dot(lhs, (head == col).astype(BF16), preferred_element_type=F32)
    return [out[i * rows:(i + 1) * rows] for i in range(len(vals))]


def _ssd_kernel(xbc_ref, dt_ref, z_ref, h0_ref, cw_ref, cb_ref, dtb_ref, alog_ref, dskip_ref, nw_ref,
                y_ref, hout_ref, xprev, *, groups, seqs, seq_rows, valid_rows, carry):
    c = pl.program_id(1)

    @pl.when(c == 0)
    def _():
        hout_ref[...] = h0_ref[...]
        if carry:
            xprev[...] = jnp.zeros_like(xprev)

    for gi in range(groups):
        _ssd_group(gi, xbc_ref, dt_ref, z_ref, cw_ref, cb_ref, dtb_ref, alog_ref, dskip_ref, nw_ref,
                   y_ref, hout_ref, xprev, seqs=seqs, seq_rows=seq_rows, valid_rows=valid_rows, carry=carry)


def _ssd_group(gi, xbc_ref, dt_ref, z_ref, cw_ref, cb_ref, dtb_ref, alog_ref, dskip_ref, nw_ref,
               y_ref, hout_ref, xprev, *, seqs, seq_rows, valid_rows, carry):
    rows = seqs * seq_rows
    xr = xbc_ref[gi]
    conv = cb_ref[...] + cw_ref[SSD_CONV - 1:SSD_CONV] * xr
    if carry:
        pv = xprev[gi]
        row8 = lax.broadcasted_iota(jnp.int32, (SUBLANES, CONV_DIM), 0)
    for k in range(1, SSD_CONV):
        sh = pltpu.roll(xr, k, axis=0)
        if carry:
            top = jnp.where(row8 < k, pltpu.roll(pv, k, axis=0), sh[:SUBLANES])
            sh = jnp.concatenate([top, sh[SUBLANES:]], axis=0)
        conv = conv + cw_ref[SSD_CONV - 1 - k:SSD_CONV - k] * sh
    if carry:
        xprev[gi] = xr[rows - SUBLANES:]
    xc = conv * _sigmoid(conv)
    xs = xc[:, :D_SSD]

    dtv = dt_ref[gi] + dtb_ref[...]
    dtv = jnp.maximum(dtv, 0.0) + jnp.log1p(jnp.exp(-jnp.abs(dtv)))
    if valid_rows < seq_rows:
        rvalid = lax.broadcasted_iota(jnp.int32, (rows, SSD_HEADS), 0) % seq_rows < valid_rows
        dtv = jnp.where(rvalid, dtv, 0.0)
    a = dtv * (-jnp.exp(alog_ref[...]))
    ri = lax.broadcasted_iota(jnp.int32, (rows, rows), 0)
    ci = lax.broadcasted_iota(jnp.int32, (rows, rows), 1)
    tril = ri >= ci
    if seqs > 1:
        tril = tril & (ri // seq_rows == ci // seq_rows)
    a_cs = jnp.dot(tril.astype(F32), a, preferred_element_type=F32, precision=lax.Precision.HIGHEST)
    totals = [a_cs[(s + 1) * seq_rows - 1:(s + 1) * seq_rows] for s in range(seqs)]
    e_tots = [jnp.exp(t) for t in totals]
    total = totals[0] if seqs == 1 else jnp.concatenate(
        [jnp.broadcast_to(t, (seq_rows, SSD_HEADS)) for t in totals], axis=0)
    dt_exp, decay_exp, e_exp = _expand_heads_mxu([dtv, jnp.exp(total - a_cs), jnp.exp(a_cs)])
    xdt = xs * dt_exp
    xdtd = (xdt * decay_exp).astype(BF16)
    eye = ri == ci
    s0 = gi * seqs

    y_diag_parts, y_off_parts = [], []
    heads_per_group = SSD_HEADS // SSD_GROUPS
    gw = heads_per_group * SSD_HEADDIM
    contract_last = (((1,), (1,)), ((), ()))
    contract_first = (((0,), (0,)), ((), ()))
    for g in range(SSD_GROUPS):
        bg = xc[:, D_SSD + g * SSD_STATE:D_SSD + (g + 1) * SSD_STATE].astype(BF16)
        cg = xc[:, D_SSD + (SSD_GROUPS + g) * SSD_STATE:D_SSD + (SSD_GROUPS + g + 1) * SSD_STATE].astype(BF16)
        y_off_seq, st_seq = [], []
        for s in range(seqs):
            rs = slice(s * seq_rows, (s + 1) * seq_rows)
            hg = hout_ref[s0 + s, g * heads_per_group:(g + 1) * heads_per_group].reshape(gw, SSD_STATE)
            y_off_seq.append(lax.dot_general(cg[rs], hg.astype(BF16), contract_last, preferred_element_type=F32))
            st_seq.append(lax.dot_general(xdtd[rs, g * gw:(g + 1) * gw], bg[rs], contract_first,
                                          preferred_element_type=F32))
        y_off = y_off_seq[0] if seqs == 1 else jnp.concatenate(y_off_seq, axis=0)
        cbm = lax.dot_general(cg, bg, contract_last, preferred_element_type=F32)
        for r in range(heads_per_group):
            h = g * heads_per_group + r
            col = a_cs[:, h:h + 1]
            rowv = jnp.sum(jnp.where(eye, col, 0.0), axis=0, keepdims=True)
            lmat = jnp.where(tril, jnp.exp(col - rowv), 0.0)
            mm = (cbm * lmat).astype(BF16)
            y_diag_parts.append(jnp.dot(mm, xdt[:, h * SSD_HEADDIM:(h + 1) * SSD_HEADDIM].astype(BF16),
                                        preferred_element_type=F32))
            for s in range(seqs):
                hout_ref[s0 + s, h] = (hout_ref[s0 + s, h] * e_tots[s][:, h:h + 1]
                                       + st_seq[s][r * SSD_HEADDIM:(r + 1) * SSD_HEADDIM])
        y_off_parts.append(y_off)
    y = (jnp.concatenate(y_diag_parts, axis=1) + jnp.concatenate(y_off_parts, axis=1) * e_exp
         + _expand_heads(dskip_ref[...], 1) * xs)
    zz = z_ref[gi]
    y = y * (zz * _sigmoid(zz))
    half = D_SSD // SSD_GROUPS
    nw = nw_ref[...]
    y_ref[gi] = jnp.concatenate([_rms_rows(y[:, g * half:(g + 1) * half], nw[:, g * half:(g + 1) * half])
                                for g in range(SSD_GROUPS)], axis=1)


def _ssd(xbc, dt, z, h0, cw, cb, dtb, alog, dskip, nw, *, seqs, seq_rows, valid_rows):
    b, s, _ = xbc.shape
    rows = seqs * seq_rows
    n_chunks = s // rows
    assert seqs == 1 or n_chunks == 1
    groups = math.gcd(b, SSD_GROUPS_PER_STEP) if seqs == 1 else 1
    kern = functools.partial(_ssd_kernel, groups=groups, seqs=seqs, seq_rows=seq_rows, valid_rows=valid_rows,
                             carry=seqs == 1)
    tokmap = lambda bb, c: (bb, c, 0)
    seqmap4 = lambda bb, c: (bb, 0, 0, 0)
    const = lambda bb, c: (0, 0)
    state_spec = pl.BlockSpec((groups * seqs, SSD_HEADS, SSD_HEADDIM, SSD_STATE), seqmap4)
    return pl.pallas_call(
        kern,
        grid=(b // groups, n_chunks),
        in_specs=[pl.BlockSpec((groups, rows, CONV_DIM), tokmap),
                  pl.BlockSpec((groups, rows, SSD_HEADS), tokmap),
                  pl.BlockSpec((groups, rows, D_SSD), tokmap),
                  state_spec,
                  pl.BlockSpec((SSD_CONV, CONV_DIM), const),
                  pl.BlockSpec((1, CONV_DIM), const),
                  pl.BlockSpec((1, SSD_HEADS), const),
                  pl.BlockSpec((1, SSD_HEADS), const),
                  pl.BlockSpec((1, SSD_HEADS), const),
                  pl.BlockSpec((1, D_SSD), const)],
        out_specs=[pl.BlockSpec((groups, rows, D_SSD), tokmap), state_spec],
        out_shape=[jax.ShapeDtypeStruct((b, s, D_SSD), F32),
                   jax.ShapeDtypeStruct((b * seqs, SSD_HEADS, SSD_HEADDIM, SSD_STATE), F32)],
        scratch_shapes=[pltpu.VMEM((groups, SUBLANES, CONV_DIM), F32)],
        compiler_params=_cparams(("arbitrary", "arbitrary")),
        name="ssd",
    )(xbc, dt, z, h0, cw, cb, dtb, alog, dskip, nw)


def _outproj_kernel(x_ref, att_ref, y_ref, wa_ref, wy_ref, nw_ref, *rest, n_cast):
    cast_in, (x1_ref, hf_ref), cast_out = rest[:n_cast], rest[n_cast:n_cast + 2], rest[n_cast + 2:]
    for src, dst in zip(cast_in, cast_out):
        dst[...] = src[...].astype(BF16)
    mix = jnp.dot(att_ref[...].astype(BF16), wa_ref[...], preferred_element_type=F32)
    mix = mix + jnp.dot(y_ref[...].astype(BF16), wy_ref[...], preferred_element_type=F32)
    x1 = x_ref[...] + mix
    x1_ref[...] = x1
    hf_ref[...] = _rms_rows(x1, nw_ref[...]).astype(BF16)


def _out_proj(x2d, att, y, w_out, nw, *, cast=()):
    assert D_ATT == D_SSD
    m = x2d.shape[0]
    tm = min(512, m)
    row = lambda i: (i, 0)
    const = lambda i: (0, 0)
    cast_specs, cast_shapes = _cast_specs(cast, m // tm)
    return pl.pallas_call(
        functools.partial(_outproj_kernel, n_cast=len(cast)),
        grid=(m // tm,),
        in_specs=[pl.BlockSpec((tm, D_MODEL), row), pl.BlockSpec((tm, D_ATT), row), pl.BlockSpec((tm, D_SSD), row),
                  pl.BlockSpec((D_ATT, D_MODEL), const, pipeline_mode=pl.Buffered(1)),
                  pl.BlockSpec((D_SSD, D_MODEL), lambda i: (1, 0), pipeline_mode=pl.Buffered(1)),
                  pl.BlockSpec((1, D_MODEL), const)] + cast_specs,
        out_specs=[pl.BlockSpec((tm, D_MODEL), row), pl.BlockSpec((tm, D_MODEL), row)] + cast_specs,
        out_shape=[jax.ShapeDtypeStruct((m, D_MODEL), F32), jax.ShapeDtypeStruct((m, D_MODEL), BF16)] + cast_shapes,
        compiler_params=_cparams(("arbitrary",)),
        name="out_proj",
    )(x2d, att, y, w_out, w_out, nw, *cast)


def _ffn_kernel(x1_ref, hf_ref, wg_ref, wu_ref, wd_ref, cw_ref, cb_ref, nfw_ref, *rest,
                tm, seq_len, tiles_per_seq):
    if seq_len >= tm:
        y_ref, gt_ref, gprev = rest
    else:
        st_ref, y_ref, gt_ref, p1_scr, p2_scr, g_scr = rest
    i = pl.program_id(0)
    f = pl.program_id(1)
    nf = pl.num_programs(1)

    @pl.when(f == 0)
    def _():
        y_ref[...] = jnp.zeros_like(y_ref)

    hf = hf_ref[...]
    g = jnp.dot(hf, wg_ref[...], preferred_element_type=F32)
    u = jnp.dot(hf, wu_ref[...], preferred_element_type=F32)
    tf = g.shape[1]
    if seq_len >= tm:
        gp = gprev[f]
        gp = jnp.where(i % tiles_per_seq == 0, 0.0, gp)
        row8 = lax.broadcasted_iota(jnp.int32, (SUBLANES, tf), 0)
        shifted = []
        for k in range(1, FFN_CONV):
            sh = pltpu.roll(g, k, axis=0)
            top = jnp.where(row8 < k, pltpu.roll(gp, k, axis=0), sh[:SUBLANES])
            shifted.append(jnp.concatenate([top, sh[SUBLANES:]], axis=0))
        g1, g2 = shifted
        gprev[f] = g[tm - SUBLANES:]
        gt_ref[0] = g[tm - SUBLANES:]
    else:
        ns = tm // seq_len

        @pl.when((i == 0) & (f == 0))
        def _():
            p1_scr[...] = jnp.zeros_like(p1_scr)
            p2_scr[...] = jnp.zeros_like(p2_scr)

        for c in range(tf // LANES):
            cs = slice(c * LANES, (c + 1) * LANES)
            g_scr[c] = g[:, cs]
            p1_scr[c, pl.ds(0, ns, stride=seq_len), :] = st_ref[1, :, cs]
            p2_scr[c, pl.ds(0, ns, stride=seq_len), :] = st_ref[0, :, cs]
            p2_scr[c, pl.ds(1, ns, stride=seq_len), :] = st_ref[1, :, cs]
            gt_ref[0, :, cs] = g_scr[c, pl.ds(seq_len - 2, ns, stride=seq_len), :]
            gt_ref[1, :, cs] = g_scr[c, pl.ds(seq_len - 1, ns, stride=seq_len), :]
        p1 = jnp.concatenate([p1_scr[c] for c in range(tf // LANES)], axis=1)
        p2 = jnp.concatenate([p2_scr[c] for c in range(tf // LANES)], axis=1)
        pos = lax.broadcasted_iota(jnp.int32, (tm, tf), 0) % seq_len
        g1 = jnp.where(pos >= 1, pltpu.roll(g, 1, axis=0), 0.0) + p1
        g2 = jnp.where(pos >= 2, pltpu.roll(g, 2, axis=0), 0.0) + p2
    gc = cb_ref[...] + cw_ref[0:1] * g2 + cw_ref[1:2] * g1 + cw_ref[2:3] * g
    act = (gc * _sigmoid(gc) * u).astype(BF16)
    y_ref[...] += jnp.dot(act, wd_ref[...], preferred_element_type=F32)

    @pl.when(f == nf - 1)
    def _():
        y_ref[...] = _rms_rows(x1_ref[...] + y_ref[...], nfw_ref[...])


def _ffn(x1, hf, wg, wu, wd, cw, cb, nfw, state, *, seq_len):
    m = x1.shape[0]
    tm = min(FFN_ROWS, m)
    prompt_mode = seq_len >= tm
    tf = FFN_TILE
    nf = D_FF // tf
    tiles_per_seq = max(seq_len // tm, 1)
    kern = functools.partial(_ffn_kernel, tm=tm, seq_len=seq_len, tiles_per_seq=tiles_per_seq)
    row = lambda i, f: (i, 0)
    in_specs = [pl.BlockSpec((tm, D_MODEL), row), pl.BlockSpec((tm, D_MODEL), row),
                pl.BlockSpec((D_MODEL, tf), lambda i, f: (0, f)), pl.BlockSpec((D_MODEL, tf), lambda i, f: (0, f)),
                pl.BlockSpec((tf, D_MODEL), lambda i, f: (f, 0)),
                pl.BlockSpec((FFN_CONV, tf), lambda i, f: (0, f)), pl.BlockSpec((1, tf), lambda i, f: (0, f)),
                pl.BlockSpec((1, D_MODEL), lambda i, f: (0, 0))]
    args = [x1, hf, wg, wu, wd, cw, cb, nfw]
    scratch = []
    if prompt_mode:
        out_specs = [pl.BlockSpec((tm, D_MODEL), row), pl.BlockSpec((1, SUBLANES, tf), lambda i, f: (i, 0, f))]
        out_shape = [jax.ShapeDtypeStruct((m, D_MODEL), F32), jax.ShapeDtypeStruct((m // tm, SUBLANES, D_FF), F32)]
        scratch.append(pltpu.VMEM((nf, SUBLANES, tf), F32))
    else:
        ns = tm // seq_len
        state_spec = pl.BlockSpec((FFN_CONV - 1, ns, tf), lambda i, f: (0, i, f))
        in_specs.append(state_spec)
        args.append(state)
        out_specs = [pl.BlockSpec((tm, D_MODEL), row), state_spec]
        out_shape = [jax.ShapeDtypeStruct((m, D_MODEL), F32), jax.ShapeDtypeStruct(state.shape, F32)]
        scratch += [pltpu.VMEM((tf // LANES, tm, LANES), F32)] * 3
    return pl.pallas_call(
        kern,
        grid=(m // tm, nf),
        in_specs=in_specs,
        out_specs=out_specs,
        out_shape=out_shape,
        scratch_shapes=scratch,
        compiler_params=_cparams(("arbitrary", "arbitrary")),
        name="ffn",
    )(*args)


def _layer(xp, xs, past, lam_init, p):
    (cache_k, cache_v, state_ssm, state_conv_ssd, state_conv_ffn, page_table) = past
    bp, sp, _ = xp.shape
    db, ds, _ = xs.shape
    n_pages = page_table.shape[1]
    past_len = n_pages * PAGE_SIZE

    w_in = p["w_in"].astype(BF16)
    nmw = p["norm_mix_w"].reshape(1, D_MODEL)
    lam_vecs = jnp.stack([p["lambda_q1"], p["lambda_k1"], p["lambda_q2"], p["lambda_k2"]])
    subw = p["subln_w"].reshape(1, ATT_DV)
    cw_ssd = p["conv_ssd_w"]
    cb_ssd = p["conv_ssd_b"].reshape(1, CONV_DIM)
    dtb = p["dt_bias"].reshape(1, SSD_HEADS)
    alog = p["a_log"].reshape(1, SSD_HEADS)
    dskip = p["d_skip"].reshape(1, SSD_HEADS)
    nsw = p["norm_ssd_w"].reshape(1, D_SSD)
    nfw = p["norm_ffn_w"].reshape(1, D_MODEL)
    cw_ffn = p["conv_ffn_w"]
    cb_ffn = p["conv_ffn_b"].reshape(1, D_FF)
    nfin = p["norm_final_w"].reshape(1, D_MODEL)

    q, k, v, z, xbc, dt, kt, wg, wu, w_out = _in_proj(
        xp.reshape(bp * sp, D_MODEL), nmw, w_in, *_rope_tables(jnp.arange(sp)), kt_seq_len=sp,
        cast=(p["w_gate"], p["w_up"], p["w_out"]))
    r3 = lambda t: t.reshape(bp, sp, t.shape[-1])
    pos_s = jnp.tile(past_len + jnp.arange(ds), _in_proj_tile(db * ds) // ds)
    q_s, k_s, v_s, z_s, xbc_s, dt_s = _in_proj(xs.reshape(db * ds, D_MODEL), nmw, w_in, *_rope_tables(pos_s))
    r3s = lambda t: t.reshape(db, ds, t.shape[-1])
    n_phys = cache_k.shape[0]
    cache_kt = jnp.transpose(cache_k, (0, 2, 3, 4, 1)).reshape(n_phys, Q_DIM, PAGE_SIZE)
    cache_vr = cache_v.reshape(n_phys, PAGE_SIZE * N_ATT_HEADS, ATT_DV)
    subw_col = subw.reshape(ATT_DV, 1)
    vn2 = v_s.reshape(db, ds * N_ATT_HEADS, ATT_DV)
    if _attn_fused_ok(db, bp, sp):
        att_s, att = _attn_fused(page_table, lam_vecs, subw, subw_col, r3s(q_s), r3s(k_s), vn2,
                                 cache_kt, cache_vr, r3(q), r3(k), r3(v), lam_init)
    else:
        att = _attn_prompt(lam_vecs, subw_col, r3(q), r3(k), r3(v), lam_init)
        att_s = _attn_sample(page_table, lam_vecs, subw, r3s(q_s), r3s(k_s), vn2, cache_kt, cache_vr, lam_init)

    rows = SSD_CHUNK if sp % SSD_CHUNK == 0 else sp
    y, ssm_p = _ssd(r3(xbc), r3(dt), r3(z), jnp.zeros((bp, SSD_HEADS, SSD_HEADDIM, SSD_STATE), F32),
                    cw_ssd, cb_ssd, dtb, alog, dskip, nsw, seqs=1, seq_rows=rows, valid_rows=rows)
    x1, hf, wd = _out_proj(xp.reshape(bp * sp, D_MODEL), att.reshape(bp * sp, D_ATT), y.reshape(bp * sp, D_SSD),
                           w_out, nfw, cast=(p["w_down"],))
    yp, gtail = _ffn(x1, hf, wg, wu, wd, cw_ffn, cb_ffn, nfin, None, seq_len=sp)
    tiles_per_seq = gtail.shape[0] // bp
    conv_ffn_p = gtail.reshape(bp, tiles_per_seq, SUBLANES, D_FF)[:, -1, SUBLANES - (FFN_CONV - 1):]
    new_k_p = jnp.transpose(kt.reshape(bp, N_ATT_HEADS, 2, ATT_DK, sp), (0, 4, 1, 2, 3))
    prompt_out = (yp.reshape(bp, sp, D_MODEL), new_k_p, v.reshape(bp, sp, N_ATT_HEADS, ATT_DV),
                  ssm_p, r3(xbc)[:, sp - (SSD_CONV - 1):], conv_ffn_p)

    k, v, z, xbc, dt, att = k_s, v_s, z_s, xbc_s, dt_s, att_s
    gs = math.gcd(db, SSD_SAMPLE_GROUP)
    pad8 = lambda t: jnp.pad(t, ((0, 0), (0, SUBLANES - ds), (0, 0))).reshape(db // gs, gs * SUBLANES, t.shape[-1])
    nxt_state = jnp.roll(state_conv_ssd.reshape(db // gs, gs, SSD_CONV - 1, CONV_DIM), -1, axis=1)
    xe = jnp.concatenate([r3s(xbc), jnp.zeros((db, SUBLANES - ds - (SSD_CONV - 1), CONV_DIM), F32),
                          nxt_state.reshape(db, SSD_CONV - 1, CONV_DIM)], axis=1)
    y8, ssm_s = _ssd(xe.reshape(db // gs, gs * SUBLANES, CONV_DIM), pad8(r3s(dt)), pad8(r3s(z)), state_ssm,
                     cw_ssd, cb_ssd, dtb, alog, dskip, nsw, seqs=gs, seq_rows=SUBLANES, valid_rows=ds)
    y = y8.reshape(db, SUBLANES, D_SSD)[:, :ds]
    x1, hf = _out_proj(xs.reshape(db * ds, D_MODEL), att.reshape(db * ds, D_ATT), y.reshape(db * ds, D_SSD), w_out, nfw)
    ys, conv_ffn_s = _ffn(x1, hf, wg, wu, wd, cw_ffn, cb_ffn, nfin, jnp.transpose(state_conv_ffn, (1, 0, 2)),
                          seq_len=ds)
    sample_out = (ys.reshape(db, ds, D_MODEL),
                  k.reshape(db, ds, N_ATT_HEADS, 2, ATT_DK), v.reshape(db, ds, N_ATT_HEADS, ATT_DV),
                  ssm_s, r3s(xbc)[:, ds - (SSD_CONV - 1):], jnp.transpose(conv_ffn_s, (1, 0, 2)))
    return prompt_out, sample_out


def kernel(x_prompt, x_sample, cache_k, cache_v, state_ssm, state_conv_ssd, state_conv_ffn, page_table, norm_mix_w, w_in, lambda_q1, lambda_k1, lambda_q2, lambda_k2, subln_w, conv_ssd_w, conv_ssd_b, dt_bias, a_log, d_skip, norm_ssd_w, w_out, norm_ffn_w, w_gate, w_up, conv_ffn_w, conv_ffn_b, w_down, norm_final_w):
    depth = w_in.shape[0]
    assert depth == 1, "the final RMSNorm is fused into the (single) layer's FFN kernel"
    lam_init = 0.8 - 0.6 * math.exp(-0.3 * 0)
    params = dict(norm_mix_w=norm_mix_w[0], w_in=w_in[0], lambda_q1=lambda_q1[0], lambda_k1=lambda_k1[0],
                  lambda_q2=lambda_q2[0], lambda_k2=lambda_k2[0], subln_w=subln_w[0], conv_ssd_w=conv_ssd_w[0],
                  conv_ssd_b=conv_ssd_b[0], dt_bias=dt_bias[0], a_log=a_log[0], d_skip=d_skip[0],
                  norm_ssd_w=norm_ssd_w[0], w_out=w_out[0], norm_ffn_w=norm_ffn_w[0], w_gate=w_gate[0],
                  w_up=w_up[0], conv_ffn_w=conv_ffn_w[0], conv_ffn_b=conv_ffn_b[0], w_down=w_down[0],
                  norm_final_w=norm_final_w)
    past = (cache_k[0], cache_v[0], state_ssm[0], state_conv_ssd[0], state_conv_ffn[0], page_table)
    (yp, kp, vp, sp_, cp, fp), (ys, ks, vs, ss, cs, fs) = _layer(x_prompt, x_sample, past, lam_init, params)
    lead = lambda t: t[None]
    return (yp, ys, lead(kp), lead(vp), lead(sp_), lead(cp), lead(fp),
            lead(ks), lead(vs), lead(ss), lead(cs), lead(fs))
```

```python
import functools
import math

import jax
import jax.numpy as jnp
from jax import lax
from jax.experimental import pallas as pl
from jax.experimental.pallas import tpu as pltpu

F32 = jnp.float32
BF16 = jnp.bfloat16

D_MODEL = 2048
ATT_DK = 64
ATT_DV = 128
N_ATT_HEADS = 8
ROT_DIM = 16
ROPE_THETA = 500000.0
Q_DIM = N_ATT_HEADS * 2 * ATT_DK
D_ATT = N_ATT_HEADS * ATT_DV
D_SSD = 1024
SSD_HEADDIM = 64
SSD_HEADS = 16
SSD_GROUPS = 2
SSD_STATE = 128
SSD_CONV = 4
SSD_CHUNK = 128
CONV_DIM = D_SSD + 2 * SSD_GROUPS * SSD_STATE
D_FF = 5632
FFN_CONV = 3
EPS = 1e-6
PAGE_SIZE = 128

LANES = 128
SUBLANES = 8
VMEM_LIMIT_BYTES = 56 * 1024 * 1024
IN_PROJ_DIM = 2 * Q_DIM + D_ATT + D_SSD + CONV_DIM + SSD_HEADS
PROJ_CHUNK = 512
SSD_SAMPLE_GROUP = 8
SSD_GROUPS_PER_STEP = 4
PROMPT_Q_TILE = 256
FFN_TILE = 512
FFN_ROWS = 512
FUSED_ATTN_VMEM_LIMIT_BYTES = 60 * 1024 * 1024
LOG2_E = math.log2(math.e)
NEG_INF = float("-inf")


def _cparams(sem):
    return pltpu.CompilerParams(dimension_semantics=sem, vmem_limit_bytes=VMEM_LIMIT_BYTES)


def _sigmoid(x):
    return 1.0 / (1.0 + jnp.exp(-x))


def _rms_rows(x, w):
    return x * lax.rsqrt(jnp.mean(x * x, axis=-1, keepdims=True) + EPS) * w


def _cast_specs(arrays, n_steps):
    specs = [pl.BlockSpec((a.shape[0] // n_steps, a.shape[1]), lambda i, *_: (i, 0)) for a in arrays]
    shapes = [jax.ShapeDtypeStruct(a.shape, BF16) for a in arrays]
    return specs, shapes


def _inproj_kernel(x_ref, nw_ref, w_ref, cos_ref, sa_ref, sb_ref, *rest, n_cast, with_kt):
    cast_in, rest = rest[:n_cast], rest[n_cast:]
    q_ref, k_ref, v_ref, z_ref, xbc_ref, dt_ref = rest[:6]
    kt_ref = rest[6] if with_kt else None
    cast_out = rest[len(rest) - n_cast:]
    for src, dst in zip(cast_in, cast_out):
        dst[...] = src[...].astype(BF16)
    xn = _rms_rows(x_ref[...], nw_ref[...]).astype(BF16)
    cos, sa, sb = cos_ref[...], sa_ref[...], sb_ref[...]

    def rope(p):
        up = pltpu.roll(p, LANES - ROT_DIM // 2, axis=1)
        dn = pltpu.roll(p, ROT_DIM // 2, axis=1)
        return p * cos + up * sa + dn * sb

    col = 0
    for ref, width, rot in ((q_ref, Q_DIM, True), (k_ref, Q_DIM, True), (v_ref, D_ATT, False),
                            (z_ref, D_SSD, False), (xbc_ref, CONV_DIM, False), (dt_ref, SSD_HEADS, False)):
        for c0 in range(0, width, PROJ_CHUNK):
            cw = min(PROJ_CHUNK, width - c0)
            p = jnp.dot(xn, w_ref[:, col + c0:col + c0 + cw], preferred_element_type=F32)
            if rot:
                for s in range(0, cw, LANES):
                    roped = rope(p[:, s:s + LANES])
                    ref[:, c0 + s:c0 + s + LANES] = roped
                    if ref is k_ref and kt_ref is not None:
                        kt_ref[0, c0 + s:c0 + s + LANES, :] = roped.T
            else:
                ref[:, c0:c0 + cw] = p
        col += width


def _in_proj_tile(m):
    return min(256, m)


def _in_proj(x2d, nw, w_bf, cos_t, sa_t, sb_t, *, kt_seq_len=None, cast=()):
    m = x2d.shape[0]
    tm = _in_proj_tile(m)
    table_tiles = cos_t.shape[0] // tm
    row = lambda i: (i, 0)
    trow = lambda i: (i % table_tiles, 0)
    const = lambda i: (0, 0)
    widths = (Q_DIM, Q_DIM, D_ATT, D_SSD, CONV_DIM, SSD_HEADS)
    out_specs = [pl.BlockSpec((tm, w), row) for w in widths]
    out_shape = [jax.ShapeDtypeStruct((m, w), F32) for w in widths]
    if kt_seq_len is not None:
        tiles_per_seq = kt_seq_len // tm
        out_specs.append(pl.BlockSpec((1, Q_DIM, tm), lambda i: (i // tiles_per_seq, 0, i % tiles_per_seq)))
        out_shape.append(jax.ShapeDtypeStruct((m // kt_seq_len, Q_DIM, kt_seq_len), F32))
    cast_specs, cast_shapes = _cast_specs(cast, m // tm)
    kern = functools.partial(_inproj_kernel, n_cast=len(cast), with_kt=kt_seq_len is not None)
    return pl.pallas_call(
        kern,
        grid=(m // tm,),
        in_specs=[pl.BlockSpec((tm, D_MODEL), row),
                  pl.BlockSpec((1, D_MODEL), const),
                  pl.BlockSpec((D_MODEL, IN_PROJ_DIM), const, pipeline_mode=pl.Buffered(1)),
                  pl.BlockSpec((tm, LANES), trow), pl.BlockSpec((tm, LANES), trow), pl.BlockSpec((tm, LANES), trow)]
                 + cast_specs,
        out_specs=out_specs + cast_specs,
        out_shape=out_shape + cast_shapes,
        compiler_params=_cparams(("arbitrary",)),
        name="in_proj",
    )(x2d, nw, w_bf, cos_t, sa_t, sb_t, *cast)


def _rope_tables(pos):
    half = ROT_DIM // 2
    d = jnp.arange(LANES) % ATT_DK
    inv = ROPE_THETA ** (-(d % half).astype(F32) / half)
    ang = pos.astype(F32)[:, None] * inv[None, :]
    cos, sin = jnp.cos(ang), jnp.sin(ang)
    first, second = (d < half)[None, :], ((d >= half) & (d < ROT_DIM))[None, :]
    return (jnp.where(first | second, cos, 1.0), jnp.where(first, -sin, 0.0), jnp.where(second, sin, 0.0))


def _lambda_full(lam_ref, lam_init):
    l = lam_ref[...]
    s1 = jnp.sum(l[0:1] * l[1:2], axis=1, keepdims=True)
    s2 = jnp.sum(l[2:3] * l[3:4], axis=1, keepdims=True)
    return jnp.exp(s1) - jnp.exp(s2) + lam_init


def _prompt_prepare(k_ref, v_ref, kbf, vt, tq, n_tiles=None):
    n_tiles = k_ref.shape[1] // tq if n_tiles is None else n_tiles
    for c in range(n_tiles):
        kbf[c * tq:(c + 1) * tq, :] = k_ref[0, c * tq:(c + 1) * tq, :].astype(BF16)
        vt[:, c * tq:(c + 1) * tq] = v_ref[0, c * tq:(c + 1) * tq, :].T.astype(BF16)


def _prompt_qtile(qi, lam, subw_ref, q_ref, o_ref, kbf, vt, s_scr, p_scr, tq, lam_init, out_pos=None, scr_off=0):
    out_pos = qi if out_pos is None else out_pos
    lane = lax.broadcasted_iota(jnp.int32, (tq, LANES), 1)
    krow = lax.broadcasted_iota(jnp.int32, (tq, 2 * tq), 0)
    qcol = lax.broadcasted_iota(jnp.int32, (tq, 2 * tq), 1)
    diag = krow <= jnp.where(qcol >= tq, qcol - tq, qcol)
    contract_last = (((1,), (1,)), ((), ()))
    q = q_ref[0, qi * tq:(qi + 1) * tq, :] * (ATT_DK ** -0.5 * LOG2_E)
    qq = jnp.concatenate([jnp.where(lane < ATT_DK, q, 0.0), jnp.where(lane >= ATT_DK, q, 0.0)],
                         axis=0).astype(BF16)
    m = None
    for c in range(qi + 1):
        s = lax.dot_general(kbf[c * tq:(c + 1) * tq, :], qq, contract_last, preferred_element_type=F32)
        if c == qi:
            s = jnp.where(diag, s, NEG_INF)
        s_scr[scr_off + c * tq:scr_off + (c + 1) * tq, :] = s
        cm = jnp.max(s, axis=0, keepdims=True)
        m = cm if m is None else jnp.maximum(m, cm)
    l = jnp.zeros((1, 2 * tq), F32)
    for c in range(qi + 1):
        p = jnp.exp2(s_scr[scr_off + c * tq:scr_off + (c + 1) * tq, :] - m)
        l = l + jnp.sum(p, axis=0, keepdims=True)
        p_scr[scr_off + c * tq:scr_off + (c + 1) * tq, :] = p.astype(BF16)
    kv = (qi + 1) * tq
    o = jnp.dot(vt[:, :kv], p_scr[scr_off:scr_off + kv, :], preferred_element_type=F32) / l
    att = o[:, :tq] - lam * o[:, tq:]
    ms = jnp.mean(att * att, axis=0, keepdims=True)
    y = att * lax.rsqrt(ms + EPS) * subw_ref[...] * (1.0 - lam_init)
    o_ref[0, out_pos * tq:(out_pos + 1) * tq, :] = y.T


def _attn_prompt_kernel(lam_ref, subw_ref, q_ref, k_ref, v_ref, o_ref, kbf, vt, s_scr, p_scr, *, tq, lam_init):
    lam = _lambda_full(lam_ref, lam_init)
    _prompt_prepare(k_ref, v_ref, kbf, vt, tq)
    for qi in range(q_ref.shape[1] // tq):
        _prompt_qtile(qi, lam, subw_ref, q_ref, o_ref, kbf, vt, s_scr, p_scr, tq, lam_init)


def _prompt_scratch(s, tq, score_rows=None):
    score_rows = s if score_rows is None else score_rows
    return [pltpu.VMEM((s, LANES), BF16), pltpu.VMEM((ATT_DV, s), BF16),
            pltpu.VMEM((score_rows, 2 * tq), F32), pltpu.VMEM((score_rows, 2 * tq), BF16)]


def _attn_prompt(lam_vecs, subw_col, q, k, v, lam_init):
    b, s, _ = q.shape
    tq = min(PROMPT_Q_TILE, s)
    kern = functools.partial(_attn_prompt_kernel, tq=tq, lam_init=lam_init)
    head = pl.BlockSpec((1, s, LANES), lambda bb, h: (bb, 0, h))
    return pl.pallas_call(
        kern,
        grid=(b, N_ATT_HEADS),
        in_specs=[pl.BlockSpec((4, ATT_DK), lambda bb, h: (0, 0)),
                  pl.BlockSpec((ATT_DV, 1), lambda bb, h: (0, 0)),
                  head, head, head],
        out_specs=head,
        out_shape=jax.ShapeDtypeStruct((b, s, D_ATT), F32),
        scratch_shapes=_prompt_scratch(s, tq),
        compiler_params=_cparams(("arbitrary", "arbitrary")),
        name="attn_prompt",
    )(lam_vecs, subw_col, q, k, v)


def _expansion_matrix():
    tok = lax.broadcasted_iota(jnp.int32, (PAGE_SIZE, PAGE_SIZE * N_ATT_HEADS), 0)
    row = lax.broadcasted_iota(jnp.int32, (PAGE_SIZE, PAGE_SIZE * N_ATT_HEADS), 1)
    return (row // N_ATT_HEADS == tok).astype(BF16)


def _sample_attend(lam_ref, subw_ref, rexp_ref, q_ref, kn_ref, vn_ref, kpage, vpage, n_pages, o_ref, kbf, t_new,
                   lam_init):
    past = n_pages * PAGE_SIZE
    n_keys = past + PAGE_SIZE
    for j in range(n_pages):
        kbf[:, j * PAGE_SIZE:(j + 1) * PAGE_SIZE] = kpage(j).astype(BF16)
    zpad = jnp.zeros((PAGE_SIZE - t_new, Q_DIM), F32)
    k_tail = jnp.concatenate([kn_ref[0], zpad], axis=0).astype(BF16)

    lam = _lambda_full(lam_ref, lam_init)
    q = q_ref[0] * (ATT_DK ** -0.5)
    nr = t_new * N_ATT_HEADS
    qrep = jnp.concatenate([jnp.broadcast_to(q[t:t + 1], (N_ATT_HEADS, Q_DIM)) for t in range(t_new)], axis=0)
    row = lax.broadcasted_iota(jnp.int32, (nr, Q_DIM), 0)
    lane = lax.broadcasted_iota(jnp.int32, (nr, Q_DIM), 1)
    head = row % N_ATT_HEADS
    grp = lane // ATT_DK
    qq = jnp.concatenate([jnp.where(grp == 2 * head, qrep, 0.0), jnp.where(grp == 2 * head + 1, qrep, 0.0)],
                         axis=0).astype(BF16)
    s_past = jnp.dot(qq, kbf[...], preferred_element_type=F32)
    s_tail = lax.dot_general(qq, k_tail, (((1,), (1,)), ((), ())), preferred_element_type=F32)
    s = jnp.concatenate([s_past, s_tail], axis=1)
    srow = lax.broadcasted_iota(jnp.int32, (2 * nr, n_keys), 0)
    scol = lax.broadcasted_iota(jnp.int32, (2 * nr, n_keys), 1)
    tq = (srow // N_ATT_HEADS) % t_new
    s = jnp.where(scol <= past + tq, s, NEG_INF)
    m = jnp.max(s, axis=1, keepdims=True)
    p = jnp.exp(s - m)
    pn = p / jnp.sum(p, axis=1, keepdims=True)
    a = (pn[:nr] - lam * pn[nr:]).astype(BF16)
    a_pages = jnp.concatenate([a[:, j * PAGE_SIZE:(j + 1) * PAGE_SIZE] for j in range(n_pages + 1)], axis=0)
    a3 = jnp.dot(a_pages, rexp_ref[...], preferred_element_type=F32)
    hrow = lax.broadcasted_iota(jnp.int32, (N_ATT_HEADS, PAGE_SIZE * N_ATT_HEADS), 0)
    hcol = lax.broadcasted_iota(jnp.int32, (N_ATT_HEADS, PAGE_SIZE * N_ATT_HEADS), 1) % N_ATT_HEADS
    own = (hrow == hcol).astype(F32)
    a3 = (a3.reshape((n_pages + 1) * t_new, N_ATT_HEADS, PAGE_SIZE * N_ATT_HEADS) * own[None]).astype(BF16)
    a3 = a3.reshape((n_pages + 1) * nr, PAGE_SIZE * N_ATT_HEADS)
    v_tail = jnp.concatenate([vn_ref[0], jnp.zeros((PAGE_SIZE - nr, ATT_DV), F32)], axis=0)
    o = jnp.dot(a3[n_pages * nr:, :PAGE_SIZE], v_tail.astype(BF16), preferred_element_type=F32)
    for j in range(n_pages):
        o = o + jnp.dot(a3[j * nr:(j + 1) * nr], vpage(j).astype(BF16), preferred_element_type=F32)
    o_ref[0] = _rms_rows(o, subw_ref[...]) * (1.0 - lam_init)


def _attn_sample_kernel(pt_ref, lam_ref, subw_ref, q_ref, kn_ref, vn_ref, *rest, n_pages, t_new, lam_init):
    kpages, vpages = rest[:n_pages], rest[n_pages:2 * n_pages]
    rexp_ref, o_ref, kbf = rest[2 * n_pages:]
    _sample_attend(lam_ref, subw_ref, rexp_ref, q_ref, kn_ref, vn_ref, lambda j: kpages[j][0], lambda j: vpages[j][0],
                   n_pages, o_ref, kbf, t_new, lam_init)


def _sample_specs(page_table, t_new, seq_of_step=lambda s: s):
    n_pages = page_table.shape[1]
    tok = pl.BlockSpec((1, t_new, Q_DIM), lambda s, pt: (seq_of_step(s), 0, 0))
    rows = pl.BlockSpec((1, t_new * N_ATT_HEADS, ATT_DV), lambda s, pt: (seq_of_step(s), 0, 0))
    page_maps = [functools.partial(lambda s, pt, j: (pt[seq_of_step(s), j], 0, 0), j=j) for j in range(n_pages)]
    kpage_specs = [pl.BlockSpec((1, Q_DIM, PAGE_SIZE), pm) for pm in page_maps]
    vpage_specs = [pl.BlockSpec((1, PAGE_SIZE * N_ATT_HEADS, ATT_DV), pm) for pm in page_maps]
    in_specs = [pl.BlockSpec((4, ATT_DK), lambda b, pt: (0, 0)),
                pl.BlockSpec((1, ATT_DV), lambda b, pt: (0, 0)),
                tok, tok, rows] + kpage_specs + vpage_specs + [
                pl.BlockSpec((PAGE_SIZE, PAGE_SIZE * N_ATT_HEADS), lambda b, pt: (0, 0))]
    scratch = [pltpu.VMEM((Q_DIM, n_pages * PAGE_SIZE), BF16)]
    return in_specs, rows, scratch


def _attn_sample(page_table, lam_vecs, subw, q, kn, vn2, cache_k, cache_v, lam_init):
    db, t_new, _ = q.shape
    n_pages = page_table.shape[1]
    kern = functools.partial(_attn_sample_kernel, n_pages=n_pages, t_new=t_new, lam_init=lam_init)
    in_specs, out_spec, scratch = _sample_specs(page_table, t_new)
    grid_spec = pltpu.PrefetchScalarGridSpec(num_scalar_prefetch=1, grid=(db,), in_specs=in_specs,
                                             out_specs=out_spec, scratch_shapes=scratch)
    att = pl.pallas_call(
        kern,
        grid_spec=grid_spec,
        out_shape=jax.ShapeDtypeStruct((db, t_new * N_ATT_HEADS, ATT_DV), F32),
        compiler_params=_cparams(("arbitrary",)),
        name="attn_sample",
    )(page_table, lam_vecs, subw, q, kn, vn2, *([cache_k] * n_pages), *([cache_v] * n_pages), _expansion_matrix())
    return att.reshape(db, t_new, D_ATT)


def _fused_tile_plan(n_qtiles, steps_per_head):
    order = []
    lo, hi = 0, n_qtiles - 1
    while lo <= hi:
        order.append(lo)
        if hi != lo:
            order.append(hi)
        lo, hi = lo + 1, hi - 1
    per_step = n_qtiles // steps_per_head
    return [order[u * per_step:(u + 1) * per_step] for u in range(steps_per_head)]


def _page_copies(pt_ref, seq, slot, ck_hbm, cv_hbm, kpg, vpg, sems, n_pages):
    copies = []
    for j in range(n_pages):
        page = pt_ref[seq, j]
        copies.append(pltpu.make_async_copy(ck_hbm.at[page], kpg.at[slot, j], sems.at[slot, 0]))
        copies.append(pltpu.make_async_copy(cv_hbm.at[page], vpg.at[slot, j], sems.at[slot, 1]))
    return copies


def _attn_fused_kernel(pt_ref, lam_ref, subw_ref, q_ref, kn_ref, vn_ref, ck_hbm, cv_hbm, rexp_ref, subw_col_ref,
                       qp_ref, kp_ref, vp_ref, o_ref, op_ref, kpg, vpg, sems, kbf, kbf_p, vt_p, s_scr, p_scr,
                       *, n_pages, t_new, lam_init, tq, plan):
    b = pl.program_id(0)
    slot = b % 2

    @pl.when(b == 0)
    def _():
        for c in _page_copies(pt_ref, 0, 0, ck_hbm, cv_hbm, kpg, vpg, sems, n_pages):
            c.start()

    @pl.when(b + 1 < pl.num_programs(0))
    def _():
        for c in _page_copies(pt_ref, b + 1, 1 - slot, ck_hbm, cv_hbm, kpg, vpg, sems, n_pages):
            c.start()

    for c in _page_copies(pt_ref, b, slot, ck_hbm, cv_hbm, kpg, vpg, sems, n_pages):
        c.wait()
    _sample_attend(lam_ref, subw_ref, rexp_ref, q_ref, kn_ref, vn_ref, lambda j: kpg[slot, j], lambda j: vpg[slot, j],
                   n_pages, o_ref, kbf, t_new, lam_init)

    sub = b % len(plan)
    lam = _lambda_full(lam_ref, lam_init)

    @pl.when(sub == 0)
    def _():
        _prompt_prepare(kp_ref, vp_ref, kbf_p, vt_p, tq)

    for u, tiles in enumerate(plan):
        @pl.when(sub == u)
        def _(tiles=tiles):
            scr_off = 0
            for qi in tiles:
                _prompt_qtile(qi, lam, subw_col_ref, qp_ref, op_ref, kbf_p, vt_p, s_scr, p_scr, tq, lam_init,
                              scr_off=scr_off)
                scr_off += (qi + 1) * tq


def _attn_fused_ok(db, bp, sp):
    heads = bp * N_ATT_HEADS
    tq = min(PROMPT_Q_TILE, sp)
    return db % heads == 0 and (sp // tq) % (db // heads) == 0


def _attn_fused(page_table, lam_vecs, subw, subw_col, q, kn, vn2, cache_k, cache_v, qp, kp, vp, lam_init):
    db, t_new, _ = q.shape
    bp, sp, _ = qp.shape
    n_pages = page_table.shape[1]
    tq = min(PROMPT_Q_TILE, sp)
    steps_per_head = db // (bp * N_ATT_HEADS)
    plan = _fused_tile_plan(sp // tq, steps_per_head)
    kern = functools.partial(_attn_fused_kernel, n_pages=n_pages, t_new=t_new, lam_init=lam_init, tq=tq, plan=plan)
    tok = pl.BlockSpec((1, t_new, Q_DIM), lambda b, pt: (b, 0, 0))
    rows = pl.BlockSpec((1, t_new * N_ATT_HEADS, ATT_DV), lambda b, pt: (b, 0, 0))
    const = lambda b, pt: (0, 0)
    head = pl.BlockSpec((1, sp, LANES), lambda b, pt: (b // steps_per_head // N_ATT_HEADS, 0,
                                                       b // steps_per_head % N_ATT_HEADS))
    score_rows = max(sum(qi + 1 for qi in tiles) for tiles in plan) * tq
    grid_spec = pltpu.PrefetchScalarGridSpec(
        num_scalar_prefetch=1,
        grid=(db,),
        in_specs=[pl.BlockSpec((4, ATT_DK), const), pl.BlockSpec((1, ATT_DV), const), tok, tok, rows,
                  pl.BlockSpec(memory_space=pl.ANY), pl.BlockSpec(memory_space=pl.ANY),
                  pl.BlockSpec((PAGE_SIZE, PAGE_SIZE * N_ATT_HEADS), const),
                  pl.BlockSpec((ATT_DV, 1), const), head, head, head],
        out_specs=[rows, head],
        scratch_shapes=[pltpu.VMEM((2, n_pages, Q_DIM, PAGE_SIZE), F32),
                        pltpu.VMEM((2, n_pages, PAGE_SIZE * N_ATT_HEADS, ATT_DV), F32),
                        pltpu.SemaphoreType.DMA((2, 2)),
                        pltpu.VMEM((Q_DIM, n_pages * PAGE_SIZE), BF16)] + _prompt_scratch(sp, tq, score_rows=score_rows),
    )
    att_s, att_p = pl.pallas_call(
        kern,
        grid_spec=grid_spec,
        out_shape=[jax.ShapeDtypeStruct((db, t_new * N_ATT_HEADS, ATT_DV), F32),
                   jax.ShapeDtypeStruct((bp, sp, D_ATT), F32)],
        compiler_params=pltpu.CompilerParams(dimension_semantics=("arbitrary",),
                                             vmem_limit_bytes=FUSED_ATTN_VMEM_LIMIT_BYTES),
        name="attn_fused",
    )(page_table, lam_vecs, subw, q, kn, vn2, cache_k, cache_v, _expansion_matrix(), subw_col, qp, kp, vp)
    return att_s.reshape(db, t_new, D_ATT), att_p


def _expand_heads(v, rows):
    return jnp.concatenate([jnp.broadcast_to(v[:, h:h + 1], (rows, SSD_HEADDIM)) for h in range(SSD_HEADS)], axis=1)


def _expand_heads_mxu(vals):
    rows = vals[0].shape[0]
    parts = []
    for v in vals:
        hi = v.astype(BF16).astype(F32)
        mid = (v - hi).astype(BF16).astype(F32)
        lo = v - hi - mid
        parts.append(jnp.concatenate([hi, mid, lo], axis=1))
    lhs = jnp.concatenate(parts, axis=0).astype(BF16)
    head = lax.broadcasted_iota(jnp.int32, (3 * SSD_HEADS, D_SSD), 0) % SSD_HEADS
    col = lax.broadcasted_iota(jnp.int32, (3 * SSD_HEADS, D_SSD), 1) // SSD_HEADDIM
    out = jnp.dot(lhs, (head == col).astype(BF16), preferred_element_type=F32)
    return [out[i * rows:(i + 1) * rows] for i in range(len(vals))]


def _ssd_kernel(xbc_ref, dt_ref, z_ref, h0_ref, cw_ref, cb_ref, dtb_ref, alog_ref, dskip_ref, nw_ref,
                y_ref, hout_ref, xprev, *, groups, seqs, seq_rows, valid_rows, carry):
    c = pl.program_id(1)

    @pl.when(c == 0)
    def _():
        hout_ref[...] = h0_ref[...]
        if carry:
            xprev[...] = jnp.zeros_like(xprev)

    for gi in range(groups):
        _ssd_group(gi, xbc_ref, dt_ref, z_ref, cw_ref, cb_ref, dtb_ref, alog_ref, dskip_ref, nw_ref,
                   y_ref, hout_ref, xprev, seqs=seqs, seq_rows=seq_rows, valid_rows=valid_rows, carry=carry)


def _ssd_group(gi, xbc_ref, dt_ref, z_ref, cw_ref, cb_ref, dtb_ref, alog_ref, dskip_ref, nw_ref,
               y_ref, hout_ref, xprev, *, seqs, seq_rows, valid_rows, carry):
    rows = seqs * seq_rows
    xr = xbc_ref[gi]
    conv = cb_ref[...] + cw_ref[SSD_CONV - 1:SSD_CONV] * xr
    if carry:
        pv = xprev[gi]
        row8 = lax.broadcasted_iota(jnp.int32, (SUBLANES, CONV_DIM), 0)
    for k in range(1, SSD_CONV):
        sh = pltpu.roll(xr, k, axis=0)
        if carry:
            top = jnp.where(row8 < k, pltpu.roll(pv, k, axis=0), sh[:SUBLANES])
            sh = jnp.concatenate([top, sh[SUBLANES:]], axis=0)
        conv = conv + cw_ref[SSD_CONV - 1 - k:SSD_CONV - k] * sh
    if carry:
        xprev[gi] = xr[rows - SUBLANES:]
    xc = conv * _sigmoid(conv)
    xs = xc[:, :D_SSD]

    dtv = dt_ref[gi] + dtb_ref[...]
    dtv = jnp.maximum(dtv, 0.0) + jnp.log1p(jnp.exp(-jnp.abs(dtv)))
    if valid_rows < seq_rows:
        rvalid = lax.broadcasted_iota(jnp.int32, (rows, SSD_HEADS), 0) % seq_rows < valid_rows
        dtv = jnp.where(rvalid, dtv, 0.0)
    a = dtv * (-jnp.exp(alog_ref[...]))
    ri = lax.broadcasted_iota(jnp.int32, (rows, rows), 0)
    ci = lax.broadcasted_iota(jnp.int32, (rows, rows), 1)
    tril = ri >= ci
    if seqs > 1:
        tril = tril & (ri // seq_rows == ci // seq_rows)
    a_cs = jnp.dot(tril.astype(F32), a, preferred_element_type=F32, precision=lax.Precision.HIGHEST)
    totals = [a_cs[(s + 1) * seq_rows - 1:(s + 1) * seq_rows] for s in range(seqs)]
    e_tots = [jnp.exp(t) for t in totals]
    total = totals[0] if seqs == 1 else jnp.concatenate(
        [jnp.broadcast_to(t, (seq_rows, SSD_HEADS)) for t in totals], axis=0)
    dt_exp, decay_exp, e_exp = _expand_heads_mxu([dtv, jnp.exp(total - a_cs), jnp.exp(a_cs)])
    xdt = xs * dt_exp
    xdtd = (xdt * decay_exp).astype(BF16)
    eye = ri == ci
    s0 = gi * seqs

    y_diag_parts, y_off_parts = [], []
    heads_per_group = SSD_HEADS // SSD_GROUPS
    gw = heads_per_group * SSD_HEADDIM
    contract_last = (((1,), (1,)), ((), ()))
    contract_first = (((0,), (0,)), ((), ()))
    for g in range(SSD_GROUPS):
        bg = xc[:, D_SSD + g * SSD_STATE:D_SSD + (g + 1) * SSD_STATE].astype(BF16)
        cg = xc[:, D_SSD + (SSD_GROUPS + g) * SSD_STATE:D_SSD + (SSD_GROUPS + g + 1) * SSD_STATE].astype(BF16)
        y_off_seq, st_seq = [], []
        for s in range(seqs):
            rs = slice(s * seq_rows, (s + 1) * seq_rows)
            hg = hout_ref[s0 + s, g * heads_per_group:(g + 1) * heads_per_group].reshape(gw, SSD_STATE)
            y_off_seq.append(lax.dot_general(cg[rs], hg.astype(BF16), contract_last, preferred_element_type=F32))
            st_seq.append(lax.dot_general(xdtd[rs, g * gw:(g + 1) * gw], bg[rs], contract_first,
                                          preferred_element_type=F32))
        y_off = y_off_seq[0] if seqs == 1 else jnp.concatenate(y_off_seq, axis=0)
        cbm = lax.dot_general(cg, bg, contract_last, preferred_element_type=F32)
        for r in range(heads_per_group):
            h = g * heads_per_group + r
            col = a_cs[:, h:h + 1]
            rowv = jnp.sum(jnp.where(eye, col, 0.0), axis=0, keepdims=True)
            lmat = jnp.where(tril, jnp.exp(col - rowv), 0.0)
            mm = (cbm * lmat).astype(BF16)
            y_diag_parts.append(jnp.dot(mm, xdt[:, h * SSD_HEADDIM:(h + 1) * SSD_HEADDIM].astype(BF16),
                                        preferred_element_type=F32))
            for s in range(seqs):
                hout_ref[s0 + s, h] = (hout_ref[s0 + s, h] * e_tots[s][:, h:h + 1]
                                       + st_seq[s][r * SSD_HEADDIM:(r + 1) * SSD_HEADDIM])
        y_off_parts.append(y_off)
    y = (jnp.concatenate(y_diag_parts, axis=1) + jnp.concatenate(y_off_parts, axis=1) * e_exp
         + _expand_heads(dskip_ref[...], 1) * xs)
    zz = z_ref[gi]
    y = y * (zz * _sigmoid(zz))
    half = D_SSD // SSD_GROUPS
    nw = nw_ref[...]
    y_ref[gi] = jnp.concatenate([_rms_rows(y[:, g * half:(g + 1) * half], nw[:, g * half:(g + 1) * half])
                                for g in range(SSD_GROUPS)], axis=1)


def _ssd(xbc, dt, z, h0, cw, cb, dtb, alog, dskip, nw, *, seqs, seq_rows, valid_rows):
    b, s, _ = xbc.shape
    rows = seqs * seq_rows
    n_chunks = s // rows
    assert seqs == 1 or n_chunks == 1
    groups = math.gcd(b, SSD_GROUPS_PER_STEP) if seqs == 1 else 1
    kern = functools.partial(_ssd_kernel, groups=groups, seqs=seqs, seq_rows=seq_rows, valid_rows=valid_rows,
                             carry=seqs == 1)
    tokmap = lambda bb, c: (bb, c, 0)
    seqmap4 = lambda bb, c: (bb, 0, 0, 0)
    const = lambda bb, c: (0, 0)
    state_spec = pl.BlockSpec((groups * seqs, SSD_HEADS, SSD_HEADDIM, SSD_STATE), seqmap4)
    return pl.pallas_call(
        kern,
        grid=(b // groups, n_chunks),
        in_specs=[pl.BlockSpec((groups, rows, CONV_DIM), tokmap),
                  pl.BlockSpec((groups, rows, SSD_HEADS), tokmap),
                  pl.BlockSpec((groups, rows, D_SSD), tokmap),
                  state_spec,
                  pl.BlockSpec((SSD_CONV, CONV_DIM), const),
                  pl.BlockSpec((1, CONV_DIM), const),
                  pl.BlockSpec((1, SSD_HEADS), const),
                  pl.BlockSpec((1, SSD_HEADS), const),
                  pl.BlockSpec((1, SSD_HEADS), const),
                  pl.BlockSpec((1, D_SSD), const)],
        out_specs=[pl.BlockSpec((groups, rows, D_SSD), tokmap), state_spec],
        out_shape=[jax.ShapeDtypeStruct((b, s, D_SSD), F32),
                   jax.ShapeDtypeStruct((b * seqs, SSD_HEADS, SSD_HEADDIM, SSD_STATE), F32)],
        scratch_shapes=[pltpu.VMEM((groups, SUBLANES, CONV_DIM), F32)],
        compiler_params=_cparams(("arbitrary", "arbitrary")),
        name="ssd",
    )(xbc, dt, z, h0, cw, cb, dtb, alog, dskip, nw)


def _outproj_kernel(x_ref, att_ref, y_ref, wa_ref, wy_ref, nw_ref, *rest, n_cast):
    cast_in, (x1_ref, hf_ref), cast_out = rest[:n_cast], rest[n_cast:n_cast + 2], rest[n_cast + 2:]
    for src, dst in zip(cast_in, cast_out):
        dst[...] = src[...].astype(BF16)
    mix = jnp.dot(att_ref[...].astype(BF16), wa_ref[...], preferred_element_type=F32)
    mix = mix + jnp.dot(y_ref[...].astype(BF16), wy_ref[...], preferred_element_type=F32)
    x1 = x_ref[...] + mix
    x1_ref[...] = x1
    hf_ref[...] = _rms_rows(x1, nw_ref[...]).astype(BF16)


def _out_proj(x2d, att, y, w_out, nw, *, cast=()):
    assert D_ATT == D_SSD
    m = x2d.shape[0]
    tm = min(512, m)
    row = lambda i: (i, 0)
    const = lambda i: (0, 0)
    cast_specs, cast_shapes = _cast_specs(cast, m // tm)
    return pl.pallas_call(
        functools.partial(_outproj_kernel, n_cast=len(cast)),
        grid=(m // tm,),
        in_specs=[pl.BlockSpec((tm, D_MODEL), row), pl.BlockSpec((tm, D_ATT), row), pl.BlockSpec((tm, D_SSD), row),
                  pl.BlockSpec((D_ATT, D_MODEL), const, pipeline_mode=pl.Buffered(1)),
                  pl.BlockSpec((D_SSD, D_MODEL), lambda i: (1, 0), pipeline_mode=pl.Buffered(1)),
                  pl.BlockSpec((1, D_MODEL), const)] + cast_specs,
        out_specs=[pl.BlockSpec((tm, D_MODEL), row), pl.BlockSpec((tm, D_MODEL), row)] + cast_specs,
        out_shape=[jax.ShapeDtypeStruct((m, D_MODEL), F32), jax.ShapeDtypeStruct((m, D_MODEL), BF16)] + cast_shapes,
        compiler_params=_cparams(("arbitrary",)),
        name="out_proj",
    )(x2d, att, y, w_out, w_out, nw, *cast)


def _ffn_kernel(x1_ref, hf_ref, wg_ref, wu_ref, wd_ref, cw_ref, cb_ref, nfw_ref, *rest,
                tm, seq_len, tiles_per_seq):
    if seq_len >= tm:
        y_ref, gt_ref, gprev = rest
    else:
        st_ref, y_ref, gt_ref, p1_scr, p2_scr, g_scr = rest
    i = pl.program_id(0)
    f = pl.program_id(1)
    nf = pl.num_programs(1)

    @pl.when(f == 0)
    def _():
        y_ref[...] = jnp.zeros_like(y_ref)

    hf = hf_ref[...]
    g = jnp.dot(hf, wg_ref[...], preferred_element_type=F32)
    u = jnp.dot(hf, wu_ref[...], preferred_element_type=F32)
    tf = g.shape[1]
    if seq_len >= tm:
        gp = gprev[f]
        gp = jnp.where(i % tiles_per_seq == 0, 0.0, gp)
        row8 = lax.broadcasted_iota(jnp.int32, (SUBLANES, tf), 0)
        shifted = []
        for k in range(1, FFN_CONV):
            sh = pltpu.roll(g, k, axis=0)
            top = jnp.where(row8 < k, pltpu.roll(gp, k, axis=0), sh[:SUBLANES])
            shifted.append(jnp.concatenate([top, sh[SUBLANES:]], axis=0))
        g1, g2 = shifted
        gprev[f] = g[tm - SUBLANES:]
        gt_ref[0] = g[tm - SUBLANES:]
    else:
        ns = tm // seq_len

        @pl.when((i == 0) & (f == 0))
        def _():
            p1_scr[...] = jnp.zeros_like(p1_scr)
            p2_scr[...] = jnp.zeros_like(p2_scr)

        for c in range(tf // LANES):
            cs = slice(c * LANES, (c + 1) * LANES)
            g_scr[c] = g[:, cs]
            p1_scr[c, pl.ds(0, ns, stride=seq_len), :] = st_ref[1, :, cs]
            p2_scr[c, pl.ds(0, ns, stride=seq_len), :] = st_ref[0, :, cs]
            p2_scr[c, pl.ds(1, ns, stride=seq_len), :] = st_ref[1, :, cs]
            gt_ref[0, :, cs] = g_scr[c, pl.ds(seq_len - 2, ns, stride=seq_len), :]
            gt_ref[1, :, cs] = g_scr[c, pl.ds(seq_len - 1, ns, stride=seq_len), :]
        p1 = jnp.concatenate([p1_scr[c] for c in range(tf // LANES)], axis=1)
        p2 = jnp.concatenate([p2_scr[c] for c in range(tf // LANES)], axis=1)
        pos = lax.broadcasted_iota(jnp.int32, (tm, tf), 0) % seq_len
        g1 = jnp.where(pos >= 1, pltpu.roll(g, 1, axis=0), 0.0) + p1
        g2 = jnp.where(pos >= 2, pltpu.roll(g, 2, axis=0), 0.0) + p2
    gc = cb_ref[...] + cw_ref[0:1] * g2 + cw_ref[1:2] * g1 + cw_ref[2:3] * g
    act = (gc * _sigmoid(gc) * u).astype(BF16)
    y_ref[...] += jnp.dot(act, wd_ref[...], preferred_element_type=F32)

    @pl.when(f == nf - 1)
    def _():
        y_ref[...] = _rms_rows(x1_ref[...] + y_ref[...], nfw_ref[...])


def _ffn(x1, hf, wg, wu, wd, cw, cb, nfw, state, *, seq_len):
    m = x1.shape[0]
    tm = min(FFN_ROWS, m)
    prompt_mode = seq_len >= tm
    tf = FFN_TILE
    nf = D_FF // tf
    tiles_per_seq = max(seq_len // tm, 1)
    kern = functools.partial(_ffn_kernel, tm=tm, seq_len=seq_len, tiles_per_seq=tiles_per_seq)
    row = lambda i, f: (i, 0)
    in_specs = [pl.BlockSpec((tm, D_MODEL), row), pl.BlockSpec((tm, D_MODEL), row),
                pl.BlockSpec((D_MODEL, tf), lambda i, f: (0, f)), pl.BlockSpec((D_MODEL, tf), lambda i, f: (0, f)),
                pl.BlockSpec((tf, D_MODEL), lambda i, f: (f, 0)),
                pl.BlockSpec((FFN_CONV, tf), lambda i, f: (0, f)), pl.BlockSpec((1, tf), lambda i, f: (0, f)),
                pl.BlockSpec((1, D_MODEL), lambda i, f: (0, 0))]
    args = [x1, hf, wg, wu, wd, cw, cb, nfw]
    scratch = []
    if prompt_mode:
        out_specs = [pl.BlockSpec((tm, D_MODEL), row), pl.BlockSpec((1, SUBLANES, tf), lambda i, f: (i, 0, f))]
        out_shape = [jax.ShapeDtypeStruct((m, D_MODEL), F32), jax.ShapeDtypeStruct((m // tm, SUBLANES, D_FF), F32)]
        scratch.append(pltpu.VMEM((nf, SUBLANES, tf), F32))
    else:
        ns = tm // seq_len
        state_spec = pl.BlockSpec((FFN_CONV - 1, ns, tf), lambda i, f: (0, i, f))
        in_specs.append(state_spec)
        args.append(state)
        out_specs = [pl.BlockSpec((tm, D_MODEL), row), state_spec]
        out_shape = [jax.ShapeDtypeStruct((m, D_MODEL), F32), jax.ShapeDtypeStruct(state.shape, F32)]
        scratch += [pltpu.VMEM((tf // LANES, tm, LANES), F32)] * 3
    return pl.pallas_call(
        kern,
        grid=(m // tm, nf),
        in_specs=in_specs,
        out_specs=out_specs,
        out_shape=out_shape,
        scratch_shapes=scratch,
        compiler_params=_cparams(("arbitrary", "arbitrary")),
        name="ffn",
    )(*args)


def _layer(xp, xs, past, lam_init, p):
    (cache_k, cache_v, state_ssm, state_conv_ssd, state_conv_ffn, page_table) = past
    bp, sp, _ = xp.shape
    db, ds, _ = xs.shape
    n_pages = page_table.shape[1]
    past_len = n_pages * PAGE_SIZE

    w_in = p["w_in"].astype(BF16)
    nmw = p["norm_mix_w"].reshape(1, D_MODEL)
    lam_vecs = jnp.stack([p["lambda_q1"], p["lambda_k1"], p["lambda_q2"], p["lambda_k2"]])
    subw = p["subln_w"].reshape(1, ATT_DV)
    cw_ssd = p["conv_ssd_w"]
    cb_ssd = p["conv_ssd_b"].reshape(1, CONV_DIM)
    dtb = p["dt_bias"].reshape(1, SSD_HEADS)
    alog = p["a_log"].reshape(1, SSD_HEADS)
    dskip = p["d_skip"].reshape(1, SSD_HEADS)
    nsw = p["norm_ssd_w"].reshape(1, D_SSD)
    nfw = p["norm_ffn_w"].reshape(1, D_MODEL)
    cw_ffn = p["conv_ffn_w"]
    cb_ffn = p["conv_ffn_b"].reshape(1, D_FF)
    nfin = p["norm_final_w"].reshape(1, D_MODEL)

    q, k, v, z, xbc, dt, kt, wg, wu, w_out = _in_proj(
        xp.reshape(bp * sp, D_MODEL), nmw, w_in, *_rope_tables(jnp.arange(sp)), kt_seq_len=sp,
        cast=(p["w_gate"], p["w_up"], p["w_out"]))
    r3 = lambda t: t.reshape(bp, sp, t.shape[-1])
    pos_s = jnp.tile(past_len + jnp.arange(ds), _in_proj_tile(db * ds) // ds)
    q_s, k_s, v_s, z_s, xbc_s, dt_s = _in_proj(xs.reshape(db * ds, D_MODEL), nmw, w_in, *_rope_tables(pos_s))
    r3s = lambda t: t.reshape(db, ds, t.shape[-1])
    n_phys = cache_k.shape[0]
    cache_kt = jnp.transpose(cache_k, (0, 2, 3, 4, 1)).reshape(n_phys, Q_DIM, PAGE_SIZE)
    cache_vr = cache_v.reshape(n_phys, PAGE_SIZE * N_ATT_HEADS, ATT_DV)
    subw_col = subw.reshape(ATT_DV, 1)
    vn2 = v_s.reshape(db, ds * N_ATT_HEADS, ATT_DV)
    if _attn_fused_ok(db, bp, sp):
        att_s, att = _attn_fused(page_table, lam_vecs, subw, subw_col, r3s(q_s), r3s(k_s), vn2,
                                 cache_kt, cache_vr, r3(q), r3(k), r3(v), lam_init)
    else:
        att = _attn_prompt(lam_vecs, subw_col, r3(q), r3(k), r3(v), lam_init)
        att_s = _attn_sample(page_table, lam_vecs, subw, r3s(q_s), r3s(k_s), vn2, cache_kt, cache_vr, lam_init)

    rows = SSD_CHUNK if sp % SSD_CHUNK == 0 else sp
    y, ssm_p = _ssd(r3(xbc), r3(dt), r3(z), jnp.zeros((bp, SSD_HEADS, SSD_HEADDIM, SSD_STATE), F32),
                    cw_ssd, cb_ssd, dtb, alog, dskip, nsw, seqs=1, seq_rows=rows, valid_rows=rows)
    x1, hf, wd = _out_proj(xp.reshape(bp * sp, D_MODEL), att.reshape(bp * sp, D_ATT), y.reshape(bp * sp, D_SSD),
                           w_out, nfw, cast=(p["w_down"],))
    yp, gtail = _ffn(x1, hf, wg, wu, wd, cw_ffn, cb_ffn, nfin, None, seq_len=sp)
    tiles_per_seq = gtail.shape[0] // bp
    conv_ffn_p = gtail.reshape(bp, tiles_per_seq, SUBLANES, D_FF)[:, -1, SUBLANES - (FFN_CONV - 1):]
    new_k_p = jnp.transpose(kt.reshape(bp, N_ATT_HEADS, 2, ATT_DK, sp), (0, 4, 1, 2, 3))
    prompt_out = (yp.reshape(bp, sp, D_MODEL), new_k_p, v.reshape(bp, sp, N_ATT_HEADS, ATT_DV),
                  ssm_p, r3(xbc)[:, sp - (SSD_CONV - 1):], conv_ffn_p)

    k, v, z, xbc, dt, att = k_s, v_s, z_s, xbc_s, dt_s, att_s
    gs = math.gcd(db, SSD_SAMPLE_GROUP)
    pad8 = lambda t: jnp.pad(t, ((0, 0), (0, SUBLANES - ds), (0, 0))).reshape(db // gs, gs * SUBLANES, t.shape[-1])
    nxt_state = jnp.roll(state_conv_ssd.reshape(db // gs, gs, SSD_CONV - 1, CONV_DIM), -1, axis=1)
    xe = jnp.concatenate([r3s(xbc), jnp.zeros((db, SUBLANES - ds - (SSD_CONV - 1), CONV_DIM), F32),
                          nxt_state.reshape(db, SSD_CONV - 1, CONV_DIM)], axis=1)
    y8, ssm_s = _ssd(xe.reshape(db // gs, gs * SUBLANES, CONV_DIM), pad8(r3s(dt)), pad8(r3s(z)), state_ssm,
                     cw_ssd, cb_ssd, dtb, alog, dskip, nsw, seqs=gs, seq_rows=SUBLANES, valid_rows=ds)
    y = y8.reshape(db, SUBLANES, D_SSD)[:, :ds]
    x1, hf = _out_proj(xs.reshape(db * ds, D_MODEL), att.reshape(db * ds, D_ATT), y.reshape(db * ds, D_SSD), w_out, nfw)
    ys, conv_ffn_s = _ffn(x1, hf, wg, wu, wd, cw_ffn, cb_ffn, nfin, jnp.transpose(state_conv_ffn, (1, 0, 2)),
                          seq_len=ds)
    sample_out = (ys.reshape(db, ds, D_MODEL),
                  k.reshape(db, ds, N_ATT_HEADS, 2, ATT_DK), v.reshape(db, ds, N_ATT_HEADS, ATT_DV),
                  ssm_s, r3s(xbc)[:, ds - (SSD_CONV - 1):], jnp.transpose(conv_ffn_s, (1, 0, 2)))
    return prompt_out, sample_out


def kernel(x_prompt, x_sample, cache_k, cache_v, state_ssm, state_conv_ssd, state_conv_ffn, page_table, norm_mix_w, w_in, lambda_q1, lambda_k1, lambda_q2, lambda_k2, subln_w, conv_ssd_w, conv_ssd_b, dt_bias, a_log, d_skip, norm_ssd_w, w_out, norm_ffn_w, w_gate, w_up, conv_ffn_w, conv_ffn_b, w_down, norm_final_w):
    depth = w_in.shape[0]
    assert depth == 1, "the final RMSNorm is fused into the (single) layer's FFN kernel"
    lam_init = 0.8 - 0.6 * math.exp(-0.3 * 0)
    params = dict(norm_mix_w=norm_mix_w[0], w_in=w_in[0], lambda_q1=lambda_q1[0], lambda_k1=lambda_k1[0],
                  lambda_q2=lambda_q2[0], lambda_k2=lambda_k2[0], subln_w=subln_w[0], conv_ssd_w=conv_ssd_w[0],
                  conv_ssd_b=conv_ssd_b[0], dt_bias=dt_bias[0], a_log=a_log[0], d_skip=d_skip[0],
                  norm_ssd_w=norm_ssd_w[0], w_out=w_out[0], norm_ffn_w=norm_ffn_w[0], w_gate=w_gate[0],
                  w_up=w_up[0], conv_ffn_w=conv_ffn_w[0], conv_ffn_b=conv_ffn_b[0], w_down=w_down[0],
                  norm_final_w=norm_final_w)
    past = (cache_k[0], cache_v[0], state_ssm[0], state_conv_ssd[0], state_conv_ffn[0], page_table)
    (yp, kp, vp, sp_, cp, fp), (ys, ks, vs, ss, cs, fs) = _layer(x_prompt, x_sample, past, lam_init, params)
    lead = lambda t: t[None]
    return (yp, ys, lead(kp), lead(vp), lead(sp_), lead(cp), lead(fp),
            lead(ks), lead(vs), lead(ss), lead(cs), lead(fs))
```

```python
import functools
import math

import jax
import jax.numpy as jnp
from jax import lax
from jax.experimental import pallas as pl
from jax.experimental.pallas import tpu as pltpu

F32 = jnp.float32
BF16 = jnp.bfloat16

D_MODEL = 2048
ATT_DK = 64
ATT_DV = 128
N_ATT_HEADS = 8
ROT_DIM = 16
ROPE_THETA = 500000.0
Q_DIM = N_ATT_HEADS * 2 * ATT_DK
D_ATT = N_ATT_HEADS * ATT_DV
D_SSD = 1024
SSD_HEADDIM = 64
SSD_HEADS = 16
SSD_GROUPS = 2
SSD_STATE = 128
SSD_CONV = 4
SSD_CHUNK = 128
CONV_DIM = D_SSD + 2 * SSD_GROUPS * SSD_STATE
D_FF = 5632
FFN_CONV = 3
EPS = 1e-6
PAGE_SIZE = 128

LANES = 128
SUBLANES = 8
VMEM_LIMIT_BYTES = 56 * 1024 * 1024
IN_PROJ_DIM = 2 * Q_DIM + D_ATT + D_SSD + CONV_DIM + SSD_HEADS
PROJ_CHUNK = 512
SSD_SAMPLE_GROUP = 8
SSD_GROUPS_PER_STEP = 4
PROMPT_Q_TILE = 256
FFN_TILE = 512
FFN_ROWS = 512
FUSED_ATTN_VMEM_LIMIT_BYTES = 60 * 1024 * 1024
LOG2_E = math.log2(math.e)
NEG_INF = float("-inf")


def _cparams(sem):
    return pltpu.CompilerParams(dimension_semantics=sem, vmem_limit_bytes=VMEM_LIMIT_BYTES)


def _sigmoid(x):
    return 1.0 / (1.0 + jnp.exp(-x))


def _rms_rows(x, w):
    return x * lax.rsqrt(jnp.mean(x * x, axis=-1, keepdims=True) + EPS) * w


def _cast_specs(arrays, n_steps):
    specs = [pl.BlockSpec((a.shape[0] // n_steps, a.shape[1]), lambda i, *_: (i, 0)) for a in arrays]
    shapes = [jax.ShapeDtypeStruct(a.shape, BF16) for a in arrays]
    return specs, shapes


def _inproj_kernel(x_ref, nw_ref, w_ref, cos_ref, sa_ref, sb_ref, *rest, n_cast, with_kt):
    cast_in, rest = rest[:n_cast], rest[n_cast:]
    q_ref, k_ref, v_ref, z_ref, xbc_ref, dt_ref = rest[:6]
    kt_ref = rest[6] if with_kt else None
    cast_out = rest[len(rest) - n_cast:]
    for src, dst in zip(cast_in, cast_out):
        dst[...] = src[...].astype(BF16)
    xn = _rms_rows(x_ref[...], nw_ref[...]).astype(BF16)
    cos, sa, sb = cos_ref[...], sa_ref[...], sb_ref[...]

    def rope(p):
        up = pltpu.roll(p, LANES - ROT_DIM // 2, axis=1)
        dn = pltpu.roll(p, ROT_DIM // 2, axis=1)
        return p * cos + up * sa + dn * sb

    col = 0
    for ref, width, rot in ((q_ref, Q_DIM, True), (k_ref, Q_DIM, True), (v_ref, D_ATT, False),
                            (z_ref, D_SSD, False), (xbc_ref, CONV_DIM, False), (dt_ref, SSD_HEADS, False)):
        for c0 in range(0, width, PROJ_CHUNK):
            cw = min(PROJ_CHUNK, width - c0)
            p = jnp.dot(xn, w_ref[:, col + c0:col + c0 + cw], preferred_element_type=F32)
            if rot:
                for s in range(0, cw, LANES):
                    roped = rope(p[:, s:s + LANES])
                    ref[:, c0 + s:c0 + s + LANES] = roped
                    if ref is k_ref and kt_ref is not None:
                        kt_ref[0, c0 + s:c0 + s + LANES, :] = roped.T
            else:
                ref[:, c0:c0 + cw] = p
        col += width


def _in_proj_tile(m):
    return min(256, m)


def _in_proj(x2d, nw, w_bf, cos_t, sa_t, sb_t, *, kt_seq_len=None, cast=()):
    m = x2d.shape[0]
    tm = _in_proj_tile(m)
    table_tiles = cos_t.shape[0] // tm
    row = lambda i: (i, 0)
    trow = lambda i: (i % table_tiles, 0)
    const = lambda i: (0, 0)
    widths = (Q_DIM, Q_DIM, D_ATT, D_SSD, CONV_DIM, SSD_HEADS)
    out_specs = [pl.BlockSpec((tm, w), row) for w in widths]
    out_shape = [jax.ShapeDtypeStruct((m, w), F32) for w in widths]
    if kt_seq_len is not None:
        tiles_per_seq = kt_seq_len // tm
        out_specs.append(pl.BlockSpec((1, Q_DIM, tm), lambda i: (i // tiles_per_seq, 0, i % tiles_per_seq)))
        out_shape.append(jax.ShapeDtypeStruct((m // kt_seq_len, Q_DIM, kt_seq_len), F32))
    cast_specs, cast_shapes = _cast_specs(cast, m // tm)
    kern = functools.partial(_inproj_kernel, n_cast=len(cast), with_kt=kt_seq_len is not None)
    return pl.pallas_call(
        kern,
        grid=(m // tm,),
        in_specs=[pl.BlockSpec((tm, D_MODEL), row),
                  pl.BlockSpec((1, D_MODEL), const),
                  pl.BlockSpec((D_MODEL, IN_PROJ_DIM), const, pipeline_mode=pl.Buffered(1)),
                  pl.BlockSpec((tm, LANES), trow), pl.BlockSpec((tm, LANES), trow), pl.BlockSpec((tm, LANES), trow)]
                 + cast_specs,
        out_specs=out_specs + cast_specs,
        out_shape=out_shape + cast_shapes,
        compiler_params=_cparams(("arbitrary",)),
        name="in_proj",
    )(x2d, nw, w_bf, cos_t, sa_t, sb_t, *cast)


def _rope_tables(pos):
    half = ROT_DIM // 2
    d = jnp.arange(LANES) % ATT_DK
    inv = ROPE_THETA ** (-(d % half).astype(F32) / half)
    ang = pos.astype(F32)[:, None] * inv[None, :]
    cos, sin = jnp.cos(ang), jnp.sin(ang)
    first, second = (d < half)[None, :], ((d >= half) & (d < ROT_DIM))[None, :]
    return (jnp.where(first | second, cos, 1.0), jnp.where(first, -sin, 0.0), jnp.where(second, sin, 0.0))


def _lambda_full(lam_ref, lam_init):
    l = lam_ref[...]
    s1 = jnp.sum(l[0:1] * l[1:2], axis=1, keepdims=True)
    s2 = jnp.sum(l[2:3] * l[3:4], axis=1, keepdims=True)
    return jnp.exp(s1) - jnp.exp(s2) + lam_init


def _prompt_prepare(k_ref, v_ref, kbf, vt, tq, n_tiles=None):
    n_tiles = k_ref.shape[1] // tq if n_tiles is None else n_tiles
    for c in range(n_tiles):
        kbf[c * tq:(c + 1) * tq, :] = k_ref[0, c * tq:(c + 1) * tq, :].astype(BF16)
        vt[:, c * tq:(c + 1) * tq] = v_ref[0, c * tq:(c + 1) * tq, :].T.astype(BF16)


def _prompt_qtile(qi, lam, subw_ref, q_ref, o_ref, kbf, vt, s_scr, p_scr, tq, lam_init, out_pos=None, scr_off=0):
    out_pos = qi if out_pos is None else out_pos
    lane = lax.broadcasted_iota(jnp.int32, (tq, LANES), 1)
    krow = lax.broadcasted_iota(jnp.int32, (tq, 2 * tq), 0)
    qcol = lax.broadcasted_iota(jnp.int32, (tq, 2 * tq), 1)
    diag = krow <= jnp.where(qcol >= tq, qcol - tq, qcol)
    contract_last = (((1,), (1,)), ((), ()))
    q = q_ref[0, qi * tq:(qi + 1) * tq, :] * (ATT_DK ** -0.5 * LOG2_E)
    qq = jnp.concatenate([jnp.where(lane < ATT_DK, q, 0.0), jnp.where(lane >= ATT_DK, q, 0.0)],
                         axis=0).astype(BF16)
    m = None
    for c in range(qi + 1):
        s = lax.dot_general(kbf[c * tq:(c + 1) * tq, :], qq, contract_last, preferred_element_type=F32)
        if c == qi:
            s = jnp.where(diag, s, NEG_INF)
        s_scr[scr_off + c * tq:scr_off + (c + 1) * tq, :] = s
        cm = jnp.max(s, axis=0, keepdims=True)
        m = cm if m is None else jnp.maximum(m, cm)
    l = jnp.zeros((1, 2 * tq), F32)
    for c in range(qi + 1):
        p = jnp.exp2(s_scr[scr_off + c * tq:scr_off + (c + 1) * tq, :] - m)
        l = l + jnp.sum(p, axis=0, keepdims=True)
        p_scr[scr_off + c * tq:scr_off + (c + 1) * tq, :] = p.astype(BF16)
    kv = (qi + 1) * tq
    o = jnp.dot(vt[:, :kv], p_scr[scr_off:scr_off + kv, :], preferred_element_type=F32) / l
    att = o[:, :tq] - lam * o[:, tq:]
    ms = jnp.mean(att * att, axis=0, keepdims=True)
    y = att * lax.rsqrt(ms + EPS) * subw_ref[...] * (1.0 - lam_init)
    o_ref[0, out_pos * tq:(out_pos + 1) * tq, :] = y.T


def _attn_prompt_kernel(lam_ref, subw_ref, q_ref, k_ref, v_ref, o_ref, kbf, vt, s_scr, p_scr, *, tq, lam_init):
    lam = _lambda_full(lam_ref, lam_init)
    _prompt_prepare(k_ref, v_ref, kbf, vt, tq)
    for qi in range(q_ref.shape[1] // tq):
        _prompt_qtile(qi, lam, subw_ref, q_ref, o_ref, kbf, vt, s_scr, p_scr, tq, lam_init)


def _prompt_scratch(s, tq, score_rows=None):
    score_rows = s if score_rows is None else score_rows
    return [pltpu.VMEM((s, LANES), BF16), pltpu.VMEM((ATT_DV, s), BF16),
            pltpu.VMEM((score_rows, 2 * tq), F32), pltpu.VMEM((score_rows, 2 * tq), BF16)]


def _attn_prompt(lam_vecs, subw_col, q, k, v, lam_init):
    b, s, _ = q.shape
    tq = min(PROMPT_Q_TILE, s)
    kern = functools.partial(_attn_prompt_kernel, tq=tq, lam_init=lam_init)
    head = pl.BlockSpec((1, s, LANES), lambda bb, h: (bb, 0, h))
    return pl.pallas_call(
        kern,
        grid=(b, N_ATT_HEADS),
        in_specs=[pl.BlockSpec((4, ATT_DK), lambda bb, h: (0, 0)),
                  pl.BlockSpec((ATT_DV, 1), lambda bb, h: (0, 0)),
                  head, head, head],
        out_specs=head,
        out_shape=jax.ShapeDtypeStruct((b, s, D_ATT), F32),
        scratch_shapes=_prompt_scratch(s, tq),
        compiler_params=_cparams(("arbitrary", "arbitrary")),
        name="attn_prompt",
    )(lam_vecs, subw_col, q, k, v)


def _expansion_matrix():
    tok = lax.broadcasted_iota(jnp.int32, (PAGE_SIZE, PAGE_SIZE * N_ATT_HEADS), 0)
    row = lax.broadcasted_iota(jnp.int32, (PAGE_SIZE, PAGE_SIZE * N_ATT_HEADS), 1)
    return (row // N_ATT_HEADS == tok).astype(BF16)


def _sample_attend(lam_ref, subw_ref, rexp_ref, q_ref, kn_ref, vn_ref, kpage, vpage, n_pages, o_ref, kbf, t_new,
                   lam_init):
    past = n_pages * PAGE_SIZE
    n_keys = past + PAGE_SIZE
    for j in range(n_pages):
        kbf[:, j * PAGE_SIZE:(j + 1) * PAGE_SIZE] = kpage(j).astype(BF16)
    zpad = jnp.zeros((PAGE_SIZE - t_new, Q_DIM), F32)
    k_tail = jnp.concatenate([kn_ref[0], zpad], axis=0).astype(BF16)

    lam = _lambda_full(lam_ref, lam_init)
    q = q_ref[0] * (ATT_DK ** -0.5)
    nr = t_new * N_ATT_HEADS
    qrep = jnp.concatenate([jnp.broadcast_to(q[t:t + 1], (N_ATT_HEADS, Q_DIM)) for t in range(t_new)], axis=0)
    row = lax.broadcasted_iota(jnp.int32, (nr, Q_DIM), 0)
    lane = lax.broadcasted_iota(jnp.int32, (nr, Q_DIM), 1)
    head = row % N_ATT_HEADS
    grp = lane // ATT_DK
    qq = jnp.concatenate([jnp.where(grp == 2 * head, qrep, 0.0), jnp.where(grp == 2 * head + 1, qrep, 0.0)],
                         axis=0).astype(BF16)
    s_past = jnp.dot(qq, kbf[...], preferred_element_type=F32)
    s_tail = lax.dot_general(qq, k_tail, (((1,), (1,)), ((), ())), preferred_element_type=F32)
    s = jnp.concatenate([s_past, s_tail], axis=1)
    srow = lax.broadcasted_iota(jnp.int32, (2 * nr, n_keys), 0)
    scol = lax.broadcasted_iota(jnp.int32, (2 * nr, n_keys), 1)
    tq = (srow // N_ATT_HEADS) % t_new
    s = jnp.where(scol <= past + tq, s, NEG_INF)
    m = jnp.max(s, axis=1, keepdims=True)
    p = jnp.exp(s - m)
    pn = p / jnp.sum(p, axis=1, keepdims=True)
    a = (pn[:nr] - lam * pn[nr:]).astype(BF16)
    a_pages = jnp.concatenate([a[:, j * PAGE_SIZE:(j + 1) * PAGE_SIZE] for j in range(n_pages + 1)], axis=0)
    a3 = jnp.dot(a_pages, rexp_ref[...], preferred_element_type=F32)
    hrow = lax.broadcasted_iota(jnp.int32, (N_ATT_HEADS, PAGE_SIZE * N_ATT_HEADS), 0)
    hcol = lax.broadcasted_iota(jnp.int32, (N_ATT_HEADS, PAGE_SIZE * N_ATT_HEADS), 1) % N_ATT_HEADS
    own = (hrow == hcol).astype(F32)
    a3 = (a3.reshape((n_pages + 1) * t_new, N_ATT_HEADS, PAGE_SIZE * N_ATT_HEADS) * own[None]).astype(BF16)
    a3 = a3.reshape((n_pages + 1) * nr, PAGE_SIZE * N_ATT_HEADS)
    v_tail = jnp.concatenate([vn_ref[0], jnp.zeros((PAGE_SIZE - nr, ATT_DV), F32)], axis=0)
    o = jnp.dot(a3[n_pages * nr:, :PAGE_SIZE], v_tail.astype(BF16), preferred_element_type=F32)
    for j in range(n_pages):
        o = o + jnp.dot(a3[j * nr:(j + 1) * nr], vpage(j).astype(BF16), preferred_element_type=F32)
    o_ref[0] = _rms_rows(o, subw_ref[...]) * (1.0 - lam_init)


def _attn_sample_kernel(pt_ref, lam_ref, subw_ref, q_ref, kn_ref, vn_ref, *rest, n_pages, t_new, lam_init):
    kpages, vpages = rest[:n_pages], rest[n_pages:2 * n_pages]
    rexp_ref, o_ref, kbf = rest[2 * n_pages:]
    _sample_attend(lam_ref, subw_ref, rexp_ref, q_ref, kn_ref, vn_ref, lambda j: kpages[j][0], lambda j: vpages[j][0],
                   n_pages, o_ref, kbf, t_new, lam_init)


def _sample_specs(page_table, t_new, seq_of_step=lambda s: s):
    n_pages = page_table.shape[1]
    tok = pl.BlockSpec((1, t_new, Q_DIM), lambda s, pt: (seq_of_step(s), 0, 0))
    rows = pl.BlockSpec((1, t_new * N_ATT_HEADS, ATT_DV), lambda s, pt: (seq_of_step(s), 0, 0))
    page_maps = [functools.partial(lambda s, pt, j: (pt[seq_of_step(s), j], 0, 0), j=j) for j in range(n_pages)]
    kpage_specs = [pl.BlockSpec((1, Q_DIM, PAGE_SIZE), pm) for pm in page_maps]
    vpage_specs = [pl.BlockSpec((1, PAGE_SIZE * N_ATT_HEADS, ATT_DV), pm) for pm in page_maps]
    in_specs = [pl.BlockSpec((4, ATT_DK), lambda b, pt: (0, 0)),
                pl.BlockSpec((1, ATT_DV), lambda b, pt: (0, 0)),
                tok, tok, rows] + kpage_specs + vpage_specs + [
                pl.BlockSpec((PAGE_SIZE, PAGE_SIZE * N_ATT_HEADS), lambda b, pt: (0, 0))]
    scratch = [pltpu.VMEM((Q_DIM, n_pages * PAGE_SIZE), BF16)]
    return in_specs, rows, scratch


def _attn_sample(page_table, lam_vecs, subw, q, kn, vn2, cache_k, cache_v, lam_init):
    db, t_new, _ = q.shape
    n_pages = page_table.shape[1]
    kern = functools.partial(_attn_sample_kernel, n_pages=n_pages, t_new=t_new, lam_init=lam_init)
    in_specs, out_spec, scratch = _sample_specs(page_table, t_new)
    grid_spec = pltpu.PrefetchScalarGridSpec(num_scalar_prefetch=1, grid=(db,), in_specs=in_specs,
                                             out_specs=out_spec, scratch_shapes=scratch)
    att = pl.pallas_call(
        kern,
        grid_spec=grid_spec,
        out_shape=jax.ShapeDtypeStruct((db, t_new * N_ATT_HEADS, ATT_DV), F32),
        compiler_params=_cparams(("arbitrary",)),
        name="attn_sample",
    )(page_table, lam_vecs, subw, q, kn, vn2, *([cache_k] * n_pages), *([cache_v] * n_pages), _expansion_matrix())
    return att.reshape(db, t_new, D_ATT)


def _fused_tile_plan(n_qtiles, steps_per_head):
    order = []
    lo, hi = 0, n_qtiles - 1
    while lo <= hi:
        order.append(lo)
        if hi != lo:
            order.append(hi)
        lo, hi = lo + 1, hi - 1
    per_step = n_qtiles // steps_per_head
    return [order[u * per_step:(u + 1) * per_step] for u in range(steps_per_head)]


def _page_copies(pt_ref, seq, slot, ck_hbm, cv_hbm, kpg, vpg, sems, n_pages):
    copies = []
    for j in range(n_pages):
        page = pt_ref[seq, j]
        copies.append(pltpu.make_async_copy(ck_hbm.at[page], kpg.at[slot, j], sems.at[slot, 0]))
        copies.append(pltpu.make_async_copy(cv_hbm.at[page], vpg.at[slot, j], sems.at[slot, 1]))
    return copies


def _attn_fused_kernel(pt_ref, lam_ref, subw_ref, q_ref, kn_ref, vn_ref, ck_hbm, cv_hbm, rexp_ref, subw_col_ref,
                       qp_ref, kp_ref, vp_ref, xbc_ref, dt_ref, z_ref, h0_ref, cw_ref, cb_ref, dtb_ref, alog_ref,
                       dskip_ref, nw_ref, o_ref, op_ref, y_ref, hout_ref, kpg, vpg, sems, kbf, kbf_p, vt_p, s_scr,
                       p_scr, xprev, *, n_pages, t_new, lam_init, tq, plan, ssd_rows, ssd_chunks):
    b = pl.program_id(0)
    slot = b % 2

    @pl.when(b == 0)
    def _():
        for c in _page_copies(pt_ref, 0, 0, ck_hbm, cv_hbm, kpg, vpg, sems, n_pages):
            c.start()

    @pl.when(b + 1 < pl.num_programs(0))
    def _():
        for c in _page_copies(pt_ref, b + 1, 1 - slot, ck_hbm, cv_hbm, kpg, vpg, sems, n_pages):
            c.start()

    for c in _page_copies(pt_ref, b, slot, ck_hbm, cv_hbm, kpg, vpg, sems, n_pages):
        c.wait()
    _sample_attend(lam_ref, subw_ref, rexp_ref, q_ref, kn_ref, vn_ref, lambda j: kpg[slot, j], lambda j: vpg[slot, j],
                   n_pages, o_ref, kbf, t_new, lam_init)

    sub = b % len(plan)
    lam = _lambda_full(lam_ref, lam_init)

    @pl.when(sub == 0)
    def _():
        _prompt_prepare(kp_ref, vp_ref, kbf_p, vt_p, tq)

    for u, tiles in enumerate(plan):
        @pl.when(sub == u)
        def _(tiles=tiles):
            scr_off = 0
            for qi in tiles:
                _prompt_qtile(qi, lam, subw_col_ref, qp_ref, op_ref, kbf_p, vt_p, s_scr, p_scr, tq, lam_init,
                              scr_off=scr_off)
                scr_off += (qi + 1) * tq

    @pl.when(b % ssd_chunks == 0)
    def _():
        hout_ref[...] = h0_ref[...]
        xprev[...] = jnp.zeros_like(xprev)

    _ssd_group(0, xbc_ref, dt_ref, z_ref, cw_ref, cb_ref, dtb_ref, alog_ref, dskip_ref, nw_ref, y_ref, hout_ref, xprev,
               seqs=1, seq_rows=ssd_rows, valid_rows=ssd_rows, carry=True)


def _fused_ssd_rows(db, bp, sp):
    if db % bp or sp % (db // bp):
        return 0
    rows = sp // (db // bp)
    return rows if rows % SUBLANES == 0 and rows <= 2 * SSD_CHUNK else 0


def _attn_fused_ok(db, bp, sp):
    heads = bp * N_ATT_HEADS
    tq = min(PROMPT_Q_TILE, sp)
    return db % heads == 0 and (sp // tq) % (db // heads) == 0 and _fused_ssd_rows(db, bp, sp) > 0


def _attn_fused(page_table, lam_vecs, subw, subw_col, q, kn, vn2, cache_k, cache_v, qp, kp, vp, lam_init, ssd_args):
    db, t_new, _ = q.shape
    bp, sp, _ = qp.shape
    n_pages = page_table.shape[1]
    tq = min(PROMPT_Q_TILE, sp)
    steps_per_head = db // (bp * N_ATT_HEADS)
    plan = _fused_tile_plan(sp // tq, steps_per_head)
    ssd_rows = _fused_ssd_rows(db, bp, sp)
    ssd_chunks = sp // ssd_rows
    kern = functools.partial(_attn_fused_kernel, n_pages=n_pages, t_new=t_new, lam_init=lam_init, tq=tq, plan=plan,
                             ssd_rows=ssd_rows, ssd_chunks=ssd_chunks)
    chunk = lambda width: pl.BlockSpec((1, ssd_rows, width), lambda b, pt: (b // ssd_chunks, b % ssd_chunks, 0))
    state_spec = pl.BlockSpec((1, SSD_HEADS, SSD_HEADDIM, SSD_STATE), lambda b, pt: (b // ssd_chunks, 0, 0, 0))
    tok = pl.BlockSpec((1, t_new, Q_DIM), lambda b, pt: (b, 0, 0))
    rows = pl.BlockSpec((1, t_new * N_ATT_HEADS, ATT_DV), lambda b, pt: (b, 0, 0))
    const = lambda b, pt: (0, 0)
    head = pl.BlockSpec((1, sp, LANES), lambda b, pt: (b // steps_per_head // N_ATT_HEADS, 0,
                                                       b // steps_per_head % N_ATT_HEADS))
    score_rows = max(sum(qi + 1 for qi in tiles) for tiles in plan) * tq
    grid_spec = pltpu.PrefetchScalarGridSpec(
        num_scalar_prefetch=1,
        grid=(db,),
        in_specs=[pl.BlockSpec((4, ATT_DK), const), pl.BlockSpec((1, ATT_DV), const), tok, tok, rows,
                  pl.BlockSpec(memory_space=pl.ANY), pl.BlockSpec(memory_space=pl.ANY),
                  pl.BlockSpec((PAGE_SIZE, PAGE_SIZE * N_ATT_HEADS), const),
                  pl.BlockSpec((ATT_DV, 1), const), head, head, head,
                  chunk(CONV_DIM), chunk(SSD_HEADS), chunk(D_SSD), state_spec,
                  pl.BlockSpec((SSD_CONV, CONV_DIM), const), pl.BlockSpec((1, CONV_DIM), const),
                  pl.BlockSpec((1, SSD_HEADS), const), pl.BlockSpec((1, SSD_HEADS), const),
                  pl.BlockSpec((1, SSD_HEADS), const), pl.BlockSpec((1, D_SSD), const)],
        out_specs=[rows, head, chunk(D_SSD), state_spec],
        scratch_shapes=[pltpu.VMEM((2, n_pages, Q_DIM, PAGE_SIZE), F32),
                        pltpu.VMEM((2, n_pages, PAGE_SIZE * N_ATT_HEADS, ATT_DV), F32),
                        pltpu.SemaphoreType.DMA((2, 2)),
                        pltpu.VMEM((Q_DIM, n_pages * PAGE_SIZE), BF16)]
                       + _prompt_scratch(sp, tq, score_rows=score_rows)
                       + [pltpu.VMEM((1, SUBLANES, CONV_DIM), F32)],
    )
    att_s, att_p, y_p, ssm_p = pl.pallas_call(
        kern,
        grid_spec=grid_spec,
        out_shape=[jax.ShapeDtypeStruct((db, t_new * N_ATT_HEADS, ATT_DV), F32),
                   jax.ShapeDtypeStruct((bp, sp, D_ATT), F32),
                   jax.ShapeDtypeStruct((bp, sp, D_SSD), F32),
                   jax.ShapeDtypeStruct((bp, SSD_HEADS, SSD_HEADDIM, SSD_STATE), F32)],
        compiler_params=pltpu.CompilerParams(dimension_semantics=("arbitrary",),
                                             vmem_limit_bytes=FUSED_ATTN_VMEM_LIMIT_BYTES),
        name="attn_fused",
    )(page_table, lam_vecs, subw, q, kn, vn2, cache_k, cache_v, _expansion_matrix(), subw_col, qp, kp, vp, *ssd_args)
    return att_s.reshape(db, t_new, D_ATT), att_p, y_p, ssm_p


def _expand_heads(v, rows):
    return jnp.concatenate([jnp.broadcast_to(v[:, h:h + 1], (rows, SSD_HEADDIM)) for h in range(SSD_HEADS)], axis=1)


def _expand_heads_mxu(vals):
    rows = vals[0].shape[0]
    parts = []
    for v in vals:
        hi = v.astype(BF16).astype(F32)
        mid = (v - hi).astype(BF16).astype(F32)
        lo = v - hi - mid
        parts.append(jnp.concatenate([hi, mid, lo], axis=1))
    lhs = jnp.concatenate(parts, axis=0).astype(BF16)
    head = lax.broadcasted_iota(jnp.int32, (3 * SSD_HEADS, D_SSD), 0) % SSD_HEADS
    col = lax.broadcasted_iota(jnp.int32, (3 * SSD_HEADS, D_SSD), 1) // SSD_HEADDIM
    out = jnp.dot(lhs, (head == col).astype(BF16), preferred_element_type=F32)
    return [out[i * rows:(i + 1) * rows] for i in range(len(vals))]


def _ssd_kernel(xbc_ref, dt_ref, z_ref, h0_ref, cw_ref, cb_ref, dtb_ref, alog_ref, dskip_ref, nw_ref,
                y_ref, hout_ref, xprev, *, groups, seqs, seq_rows, valid_rows, carry):
    c = pl.program_id(1)

    @pl.when(c == 0)
    def _():
        hout_ref[...] = h0_ref[...]
        if carry:
            xprev[...] = jnp.zeros_like(xprev)

    for gi in range(groups):
        _ssd_group(gi, xbc_ref, dt_ref, z_ref, cw_ref, cb_ref, dtb_ref, alog_ref, dskip_ref, nw_ref,
                   y_ref, hout_ref, xprev, seqs=seqs, seq_rows=seq_rows, valid_rows=valid_rows, carry=carry)


def _ssd_group(gi, xbc_ref, dt_ref, z_ref, cw_ref, cb_ref, dtb_ref, alog_ref, dskip_ref, nw_ref,
               y_ref, hout_ref, xprev, *, seqs, seq_rows, valid_rows, carry):
    rows = seqs * seq_rows
    xr = xbc_ref[gi]
    conv = cb_ref[...] + cw_ref[SSD_CONV - 1:SSD_CONV] * xr
    if carry:
        pv = xprev[gi]
        row8 = lax.broadcasted_iota(jnp.int32, (SUBLANES, CONV_DIM), 0)
    for k in range(1, SSD_CONV):
        sh = pltpu.roll(xr, k, axis=0)
        if carry:
            top = jnp.where(row8 < k, pltpu.roll(pv, k, axis=0), sh[:SUBLANES])
            sh = jnp.concatenate([top, sh[SUBLANES:]], axis=0)
        conv = conv + cw_ref[SSD_CONV - 1 - k:SSD_CONV - k] * sh
    if carry:
        xprev[gi] = xr[rows - SUBLANES:]
    xc = conv * _sigmoid(conv)
    xs = xc[:, :D_SSD]

    dtv = dt_ref[gi] + dtb_ref[...]
    dtv = jnp.maximum(dtv, 0.0) + jnp.log1p(jnp.exp(-jnp.abs(dtv)))
    if valid_rows < seq_rows:
        rvalid = lax.broadcasted_iota(jnp.int32, (rows, SSD_HEADS), 0) % seq_rows < valid_rows
        dtv = jnp.where(rvalid, dtv, 0.0)
    a = dtv * (-jnp.exp(alog_ref[...]))
    ri = lax.broadcasted_iota(jnp.int32, (rows, rows), 0)
    ci = lax.broadcasted_iota(jnp.int32, (rows, rows), 1)
    tril = ri >= ci
    if seqs > 1:
        tril = tril & (ri // seq_rows == ci // seq_rows)
    a_cs = jnp.dot(tril.astype(F32), a, preferred_element_type=F32, precision=lax.Precision.HIGHEST)
    totals = [a_cs[(s + 1) * seq_rows - 1:(s + 1) * seq_rows] for s in range(seqs)]
    e_tots = [jnp.exp(t) for t in totals]
    total = totals[0] if seqs == 1 else jnp.concatenate(
        [jnp.broadcast_to(t, (seq_rows, SSD_HEADS)) for t in totals], axis=0)
    dt_exp, decay_exp, e_exp = _expand_heads_mxu([dtv, jnp.exp(total - a_cs), jnp.exp(a_cs)])
    xdt = xs * dt_exp
    xdtd = (xdt * decay_exp).astype(BF16)
    eye = ri == ci
    s0 = gi * seqs

    y_diag_parts, y_off_parts = [], []
    heads_per_group = SSD_HEADS // SSD_GROUPS
    gw = heads_per_group * SSD_HEADDIM
    contract_last = (((1,), (1,)), ((), ()))
    contract_first = (((0,), (0,)), ((), ()))
    for g in range(SSD_GROUPS):
        bg = xc[:, D_SSD + g * SSD_STATE:D_SSD + (g + 1) * SSD_STATE].astype(BF16)
        cg = xc[:, D_SSD + (SSD_GROUPS + g) * SSD_STATE:D_SSD + (SSD_GROUPS + g + 1) * SSD_STATE].astype(BF16)
        y_off_seq, st_seq = [], []
        for s in range(seqs):
            rs = slice(s * seq_rows, (s + 1) * seq_rows)
            hg = hout_ref[s0 + s, g * heads_per_group:(g + 1) * heads_per_group].reshape(gw, SSD_STATE)
            y_off_seq.append(lax.dot_general(cg[rs], hg.astype(BF16), contract_last, preferred_element_type=F32))
            st_seq.append(lax.dot_general(xdtd[rs, g * gw:(g + 1) * gw], bg[rs], contract_first,
                                          preferred_element_type=F32))
        y_off = y_off_seq[0] if seqs == 1 else jnp.concatenate(y_off_seq, axis=0)
        cbm = lax.dot_general(cg, bg, contract_last, preferred_element_type=F32)
        for r in range(heads_per_group):
            h = g * heads_per_group + r
            col = a_cs[:, h:h + 1]
            rowv = jnp.sum(jnp.where(eye, col, 0.0), axis=0, keepdims=True)
            lmat = jnp.where(tril, jnp.exp(col - rowv), 0.0)
            mm = (cbm * lmat).astype(BF16)
            y_diag_parts.append(jnp.dot(mm, xdt[:, h * SSD_HEADDIM:(h + 1) * SSD_HEADDIM].astype(BF16),
                                        preferred_element_type=F32))
            for s in range(seqs):
                hout_ref[s0 + s, h] = (hout_ref[s0 + s, h] * e_tots[s][:, h:h + 1]
                                       + st_seq[s][r * SSD_HEADDIM:(r + 1) * SSD_HEADDIM])
        y_off_parts.append(y_off)
    y = (jnp.concatenate(y_diag_parts, axis=1) + jnp.concatenate(y_off_parts, axis=1) * e_exp
         + _expand_heads(dskip_ref[...], 1) * xs)
    zz = z_ref[gi]
    y = y * (zz * _sigmoid(zz))
    half = D_SSD // SSD_GROUPS
    nw = nw_ref[...]
    y_ref[gi] = jnp.concatenate([_rms_rows(y[:, g * half:(g + 1) * half], nw[:, g * half:(g + 1) * half])
                                for g in range(SSD_GROUPS)], axis=1)


def _ssd(xbc, dt, z, h0, cw, cb, dtb, alog, dskip, nw, *, seqs, seq_rows, valid_rows):
    b, s, _ = xbc.shape
    rows = seqs * seq_rows
    n_chunks = s // rows
    assert seqs == 1 or n_chunks == 1
    groups = math.gcd(b, SSD_GROUPS_PER_STEP) if seqs == 1 else 1
    kern = functools.partial(_ssd_kernel, groups=groups, seqs=seqs, seq_rows=seq_rows, valid_rows=valid_rows,
                             carry=seqs == 1)
    tokmap = lambda bb, c: (bb, c, 0)
    seqmap4 = lambda bb, c: (bb, 0, 0, 0)
    const = lambda bb, c: (0, 0)
    state_spec = pl.BlockSpec((groups * seqs, SSD_HEADS, SSD_HEADDIM, SSD_STATE), seqmap4)
    return pl.pallas_call(
        kern,
        grid=(b // groups, n_chunks),
        in_specs=[pl.BlockSpec((groups, rows, CONV_DIM), tokmap),
                  pl.BlockSpec((groups, rows, SSD_HEADS), tokmap),
                  pl.BlockSpec((groups, rows, D_SSD), tokmap),
                  state_spec,
                  pl.BlockSpec((SSD_CONV, CONV_DIM), const),
                  pl.BlockSpec((1, CONV_DIM), const),
                  pl.BlockSpec((1, SSD_HEADS), const),
                  pl.BlockSpec((1, SSD_HEADS), const),
                  pl.BlockSpec((1, SSD_HEADS), const),
                  pl.BlockSpec((1, D_SSD), const)],
        out_specs=[pl.BlockSpec((groups, rows, D_SSD), tokmap), state_spec],
        out_shape=[jax.ShapeDtypeStruct((b, s, D_SSD), F32),
                   jax.ShapeDtypeStruct((b * seqs, SSD_HEADS, SSD_HEADDIM, SSD_STATE), F32)],
        scratch_shapes=[pltpu.VMEM((groups, SUBLANES, CONV_DIM), F32)],
        compiler_params=_cparams(("arbitrary", "arbitrary")),
        name="ssd",
    )(xbc, dt, z, h0, cw, cb, dtb, alog, dskip, nw)


def _outproj_kernel(x_ref, att_ref, y_ref, wa_ref, wy_ref, nw_ref, *rest, n_cast):
    cast_in, (x1_ref, hf_ref), cast_out = rest[:n_cast], rest[n_cast:n_cast + 2], rest[n_cast + 2:]
    for src, dst in zip(cast_in, cast_out):
        dst[...] = src[...].astype(BF16)
    mix = jnp.dot(att_ref[...].astype(BF16), wa_ref[...], preferred_element_type=F32)
    mix = mix + jnp.dot(y_ref[...].astype(BF16), wy_ref[...], preferred_element_type=F32)
    x1 = x_ref[...] + mix
    x1_ref[...] = x1
    hf_ref[...] = _rms_rows(x1, nw_ref[...]).astype(BF16)


def _out_proj(x2d, att, y, w_out, nw, *, cast=()):
    assert D_ATT == D_SSD
    m = x2d.shape[0]
    tm = min(512, m)
    row = lambda i: (i, 0)
    const = lambda i: (0, 0)
    cast_specs, cast_shapes = _cast_specs(cast, m // tm)
    return pl.pallas_call(
        functools.partial(_outproj_kernel, n_cast=len(cast)),
        grid=(m // tm,),
        in_specs=[pl.BlockSpec((tm, D_MODEL), row), pl.BlockSpec((tm, D_ATT), row), pl.BlockSpec((tm, D_SSD), row),
                  pl.BlockSpec((D_ATT, D_MODEL), const, pipeline_mode=pl.Buffered(1)),
                  pl.BlockSpec((D_SSD, D_MODEL), lambda i: (1, 0), pipeline_mode=pl.Buffered(1)),
                  pl.BlockSpec((1, D_MODEL), const)] + cast_specs,
        out_specs=[pl.BlockSpec((tm, D_MODEL), row), pl.BlockSpec((tm, D_MODEL), row)] + cast_specs,
        out_shape=[jax.ShapeDtypeStruct((m, D_MODEL), F32), jax.ShapeDtypeStruct((m, D_MODEL), BF16)] + cast_shapes,
        compiler_params=_cparams(("arbitrary",)),
        name="out_proj",
    )(x2d, att, y, w_out, w_out, nw, *cast)


def _ffn_kernel(x1_ref, hf_ref, wg_ref, wu_ref, wd_ref, cw_ref, cb_ref, nfw_ref, *rest,
                tm, seq_len, tiles_per_seq):
    if seq_len >= tm:
        y_ref, gt_ref, gprev = rest
    else:
        st_ref, y_ref, gt_ref, p1_scr, p2_scr, g_scr = rest
    i = pl.program_id(0)
    f = pl.program_id(1)
    nf = pl.num_programs(1)

    @pl.when(f == 0)
    def _():
        y_ref[...] = jnp.zeros_like(y_ref)

    hf = hf_ref[...]
    g = jnp.dot(hf, wg_ref[...], preferred_element_type=F32)
    u = jnp.dot(hf, wu_ref[...], preferred_element_type=F32)
    tf = g.shape[1]
    if seq_len >= tm:
        gp = gprev[f]
        gp = jnp.where(i % tiles_per_seq == 0, 0.0, gp)
        row8 = lax.broadcasted_iota(jnp.int32, (SUBLANES, tf), 0)
        shifted = []
        for k in range(1, FFN_CONV):
            sh = pltpu.roll(g, k, axis=0)
            top = jnp.where(row8 < k, pltpu.roll(gp, k, axis=0), sh[:SUBLANES])
            shifted.append(jnp.concatenate([top, sh[SUBLANES:]], axis=0))
        g1, g2 = shifted
        gprev[f] = g[tm - SUBLANES:]
        gt_ref[0] = g[tm - SUBLANES:]
    else:
        ns = tm // seq_len

        @pl.when((i == 0) & (f == 0))
        def _():
            p1_scr[...] = jnp.zeros_like(p1_scr)
            p2_scr[...] = jnp.zeros_like(p2_scr)

        for c in range(tf // LANES):
            cs = slice(c * LANES, (c + 1) * LANES)
            g_scr[c] = g[:, cs]
            p1_scr[c, pl.ds(0, ns, stride=seq_len), :] = st_ref[1, :, cs]
            p2_scr[c, pl.ds(0, ns, stride=seq_len), :] = st_ref[0, :, cs]
            p2_scr[c, pl.ds(1, ns, stride=seq_len), :] = st_ref[1, :, cs]
            gt_ref[0, :, cs] = g_scr[c, pl.ds(seq_len - 2, ns, stride=seq_len), :]
            gt_ref[1, :, cs] = g_scr[c, pl.ds(seq_len - 1, ns, stride=seq_len), :]
        p1 = jnp.concatenate([p1_scr[c] for c in range(tf // LANES)], axis=1)
        p2 = jnp.concatenate([p2_scr[c] for c in range(tf // LANES)], axis=1)
        pos = lax.broadcasted_iota(jnp.int32, (tm, tf), 0) % seq_len
        g1 = jnp.where(pos >= 1, pltpu.roll(g, 1, axis=0), 0.0) + p1
        g2 = jnp.where(pos >= 2, pltpu.roll(g, 2, axis=0), 0.0) + p2
    gc = cb_ref[...] + cw_ref[0:1] * g2 + cw_ref[1:2] * g1 + cw_ref[2:3] * g
    act = (gc * _sigmoid(gc) * u).astype(BF16)
    y_ref[...] += jnp.dot(act, wd_ref[...], preferred_element_type=F32)

    @pl.when(f == nf - 1)
    def _():
        y_ref[...] = _rms_rows(x1_ref[...] + y_ref[...], nfw_ref[...])


def _ffn(x1, hf, wg, wu, wd, cw, cb, nfw, state, *, seq_len):
    m = x1.shape[0]
    tm = min(FFN_ROWS, m)
    prompt_mode = seq_len >= tm
    tf = FFN_TILE
    nf = D_FF // tf
    tiles_per_seq = max(seq_len // tm, 1)
    kern = functools.partial(_ffn_kernel, tm=tm, seq_len=seq_len, tiles_per_seq=tiles_per_seq)
    row = lambda i, f: (i, 0)
    in_specs = [pl.BlockSpec((tm, D_MODEL), row), pl.BlockSpec((tm, D_MODEL), row),
                pl.BlockSpec((D_MODEL, tf), lambda i, f: (0, f)), pl.BlockSpec((D_MODEL, tf), lambda i, f: (0, f)),
                pl.BlockSpec((tf, D_MODEL), lambda i, f: (f, 0)),
                pl.BlockSpec((FFN_CONV, tf), lambda i, f: (0, f)), pl.BlockSpec((1, tf), lambda i, f: (0, f)),
                pl.BlockSpec((1, D_MODEL), lambda i, f: (0, 0))]
    args = [x1, hf, wg, wu, wd, cw, cb, nfw]
    scratch = []
    if prompt_mode:
        out_specs = [pl.BlockSpec((tm, D_MODEL), row), pl.BlockSpec((1, SUBLANES, tf), lambda i, f: (i, 0, f))]
        out_shape = [jax.ShapeDtypeStruct((m, D_MODEL), F32), jax.ShapeDtypeStruct((m // tm, SUBLANES, D_FF), F32)]
        scratch.append(pltpu.VMEM((nf, SUBLANES, tf), F32))
    else:
        ns = tm // seq_len
        state_spec = pl.BlockSpec((FFN_CONV - 1, ns, tf), lambda i, f: (0, i, f))
        in_specs.append(state_spec)
        args.append(state)
        out_specs = [pl.BlockSpec((tm, D_MODEL), row), state_spec]
        out_shape = [jax.ShapeDtypeStruct((m, D_MODEL), F32), jax.ShapeDtypeStruct(state.shape, F32)]
        scratch += [pltpu.VMEM((tf // LANES, tm, LANES), F32)] * 3
    return pl.pallas_call(
        kern,
        grid=(m // tm, nf),
        in_specs=in_specs,
        out_specs=out_specs,
        out_shape=out_shape,
        scratch_shapes=scratch,
        compiler_params=_cparams(("arbitrary", "arbitrary")),
        name="ffn",
    )(*args)


def _layer(xp, xs, past, lam_init, p):
    (cache_k, cache_v, state_ssm, state_conv_ssd, state_conv_ffn, page_table) = past
    bp, sp, _ = xp.shape
    db, ds, _ = xs.shape
    n_pages = page_table.shape[1]
    past_len = n_pages * PAGE_SIZE

    w_in = p["w_in"].astype(BF16)
    nmw = p["norm_mix_w"].reshape(1, D_MODEL)
    lam_vecs = jnp.stack([p["lambda_q1"], p["lambda_k1"], p["lambda_q2"], p["lambda_k2"]])
    subw = p["subln_w"].reshape(1, ATT_DV)
    cw_ssd = p["conv_ssd_w"]
    cb_ssd = p["conv_ssd_b"].reshape(1, CONV_DIM)
    dtb = p["dt_bias"].reshape(1, SSD_HEADS)
    alog = p["a_log"].reshape(1, SSD_HEADS)
    dskip = p["d_skip"].reshape(1, SSD_HEADS)
    nsw = p["norm_ssd_w"].reshape(1, D_SSD)
    nfw = p["norm_ffn_w"].reshape(1, D_MODEL)
    cw_ffn = p["conv_ffn_w"]
    cb_ffn = p["conv_ffn_b"].reshape(1, D_FF)
    nfin = p["norm_final_w"].reshape(1, D_MODEL)

    q, k, v, z, xbc, dt, kt, wg, wu, w_out = _in_proj(
        xp.reshape(bp * sp, D_MODEL), nmw, w_in, *_rope_tables(jnp.arange(sp)), kt_seq_len=sp,
        cast=(p["w_gate"], p["w_up"], p["w_out"]))
    r3 = lambda t: t.reshape(bp, sp, t.shape[-1])
    pos_s = jnp.tile(past_len + jnp.arange(ds), _in_proj_tile(db * ds) // ds)
    q_s, k_s, v_s, z_s, xbc_s, dt_s = _in_proj(xs.reshape(db * ds, D_MODEL), nmw, w_in, *_rope_tables(pos_s))
    r3s = lambda t: t.reshape(db, ds, t.shape[-1])
    n_phys = cache_k.shape[0]
    cache_kt = jnp.transpose(cache_k, (0, 2, 3, 4, 1)).reshape(n_phys, Q_DIM, PAGE_SIZE)
    cache_vr = cache_v.reshape(n_phys, PAGE_SIZE * N_ATT_HEADS, ATT_DV)
    subw_col = subw.reshape(ATT_DV, 1)
    vn2 = v_s.reshape(db, ds * N_ATT_HEADS, ATT_DV)
    ssd_args = (r3(xbc), r3(dt), r3(z), jnp.zeros((bp, SSD_HEADS, SSD_HEADDIM, SSD_STATE), F32),
                cw_ssd, cb_ssd, dtb, alog, dskip, nsw)
    if _attn_fused_ok(db, bp, sp):
        att_s, att, y, ssm_p = _attn_fused(page_table, lam_vecs, subw, subw_col, r3s(q_s), r3s(k_s), vn2,
                                           cache_kt, cache_vr, r3(q), r3(k), r3(v), lam_init, ssd_args)
    else:
        att = _attn_prompt(lam_vecs, subw_col, r3(q), r3(k), r3(v), lam_init)
        att_s = _attn_sample(page_table, lam_vecs, subw, r3s(q_s), r3s(k_s), vn2, cache_kt, cache_vr, lam_init)
        rows = SSD_CHUNK if sp % SSD_CHUNK == 0 else sp
        y, ssm_p = _ssd(*ssd_args, seqs=1, seq_rows=rows, valid_rows=rows)

    x1, hf, wd = _out_proj(xp.reshape(bp * sp, D_MODEL), att.reshape(bp * sp, D_ATT), y.reshape(bp * sp, D_SSD),
                           w_out, nfw, cast=(p["w_down"],))
    yp, gtail = _ffn(x1, hf, wg, wu, wd, cw_ffn, cb_ffn, nfin, None, seq_len=sp)
    tiles_per_seq = gtail.shape[0] // bp
    conv_ffn_p = gtail.reshape(bp, tiles_per_seq, SUBLANES, D_FF)[:, -1, SUBLANES - (FFN_CONV - 1):]
    new_k_p = jnp.transpose(kt.reshape(bp, N_ATT_HEADS, 2, ATT_DK, sp), (0, 4, 1, 2, 3))
    prompt_out = (yp.reshape(bp, sp, D_MODEL), new_k_p, v.reshape(bp, sp, N_ATT_HEADS, ATT_DV),
                  ssm_p, r3(xbc)[:, sp - (SSD_CONV - 1):], conv_ffn_p)

    k, v, z, xbc, dt, att = k_s, v_s, z_s, xbc_s, dt_s, att_s
    gs = math.gcd(db, SSD_SAMPLE_GROUP)
    pad8 = lambda t: jnp.pad(t, ((0, 0), (0, SUBLANES - ds), (0, 0))).reshape(db // gs, gs * SUBLANES, t.shape[-1])
    nxt_state = jnp.roll(state_conv_ssd.reshape(db // gs, gs, SSD_CONV - 1, CONV_DIM), -1, axis=1)
    xe = jnp.concatenate([r3s(xbc), jnp.zeros((db, SUBLANES - ds - (SSD_CONV - 1), CONV_DIM), F32),
                          nxt_state.reshape(db, SSD_CONV - 1, CONV_DIM)], axis=1)
    y8, ssm_s = _ssd(xe.reshape(db // gs, gs * SUBLANES, CONV_DIM), pad8(r3s(dt)), pad8(r3s(z)), state_ssm,
                     cw_ssd, cb_ssd, dtb, alog, dskip, nsw, seqs=gs, seq_rows=SUBLANES, valid_rows=ds)
    y = y8.reshape(db, SUBLANES, D_SSD)[:, :ds]
    x1, hf = _out_proj(xs.reshape(db * ds, D_MODEL), att.reshape(db * ds, D_ATT), y.reshape(db * ds, D_SSD), w_out, nfw)
    ys, conv_ffn_s = _ffn(x1, hf, wg, wu, wd, cw_ffn, cb_ffn, nfin, jnp.transpose(state_conv_ffn, (1, 0, 2)),
                          seq_len=ds)
    sample_out = (ys.reshape(db, ds, D_MODEL),
                  k.reshape(db, ds, N_ATT_HEADS, 2, ATT_DK), v.reshape(db, ds, N_ATT_HEADS, ATT_DV),
                  ssm_s, r3s(xbc)[:, ds - (SSD_CONV - 1):], jnp.transpose(conv_ffn_s, (1, 0, 2)))
    return prompt_out, sample_out


def kernel(x_prompt, x_sample, cache_k, cache_v, state_ssm, state_conv_ssd, state_conv_ffn, page_table, norm_mix_w, w_in, lambda_q1, lambda_k1, lambda_q2, lambda_k2, subln_w, conv_ssd_w, conv_ssd_b, dt_bias, a_log, d_skip, norm_ssd_w, w_out, norm_ffn_w, w_gate, w_up, conv_ffn_w, conv_ffn_b, w_down, norm_final_w):
    depth = w_in.shape[0]
    assert depth == 1, "the final RMSNorm is fused into the (single) layer's FFN kernel"
    lam_init = 0.8 - 0.6 * math.exp(-0.3 * 0)
    params = dict(norm_mix_w=norm_mix_w[0], w_in=w_in[0], lambda_q1=lambda_q1[0], lambda_k1=lambda_k1[0],
                  lambda_q2=lambda_q2[0], lambda_k2=lambda_k2[0], subln_w=subln_w[0], conv_ssd_w=conv_ssd_w[0],
                  conv_ssd_b=conv_ssd_b[0], dt_bias=dt_bias[0], a_log=a_log[0], d_skip=d_skip[0],
                  norm_ssd_w=norm_ssd_w[0], w_out=w_out[0], norm_ffn_w=norm_ffn_w[0], w_gate=w_gate[0],
                  w_up=w_up[0], conv_ffn_w=conv_ffn_w[0], conv_ffn_b=conv_ffn_b[0], w_down=w_down[0],
                  norm_final_w=norm_final_w)
    past = (cache_k[0], cache_v[0], state_ssm[0], state_conv_ssd[0], state_conv_ffn[0], page_table)
    (yp, kp, vp, sp_, cp, fp), (ys, ks, vs, ss, cs, fs) = _layer(x_prompt, x_sample, past, lam_init, params)
    lead = lambda t: t[None]
    return (yp, ys, lead(kp), lead(vp), lead(sp_), lead(cp), lead(fp),
            lead(ks), lead(vs), lead(ss), lead(cs), lead(fs))
```

```python
import functools
import math

import jax
import jax.numpy as jnp
from jax import lax
from jax.experimental import pallas as pl
from jax.experimental.pallas import tpu as pltpu

F32 = jnp.float32
BF16 = jnp.bfloat16

D_MODEL = 2048
ATT_DK = 64
ATT_DV = 128
N_ATT_HEADS = 8
ROT_DIM = 16
ROPE_THETA = 500000.0
Q_DIM = N_ATT_HEADS * 2 * ATT_DK
D_ATT = N_ATT_HEADS * ATT_DV
D_SSD = 1024
SSD_HEADDIM = 64
SSD_HEADS = 16
SSD_GROUPS = 2
SSD_STATE = 128
SSD_CONV = 4
SSD_CHUNK = 128
CONV_DIM = D_SSD + 2 * SSD_GROUPS * SSD_STATE
D_FF = 5632
FFN_CONV = 3
EPS = 1e-6
PAGE_SIZE = 128

LANES = 128
SUBLANES = 8
VMEM_LIMIT_BYTES = 56 * 1024 * 1024
IN_PROJ_DIM = 2 * Q_DIM + D_ATT + D_SSD + CONV_DIM + SSD_HEADS
PROJ_CHUNK = 512
SSD_SAMPLE_GROUP = 16
SSD_GROUPS_PER_STEP = 4
PROMPT_Q_TILE = 256
FFN_TILE = 512
FFN_ROWS = 512
FUSED_ATTN_VMEM_LIMIT_BYTES = 60 * 1024 * 1024
LOG2_E = math.log2(math.e)
NEG_INF = float("-inf")


def _cparams(sem):
    return pltpu.CompilerParams(dimension_semantics=sem, vmem_limit_bytes=VMEM_LIMIT_BYTES)


def _silu(x):
    h = 0.5 * x
    return h + h * jnp.tanh(h)


def _rms_rows(x, w):
    return x * lax.rsqrt(jnp.mean(x * x, axis=-1, keepdims=True) + EPS) * w


def _cast_specs(arrays, n_steps):
    specs = [pl.BlockSpec((a.shape[0] // n_steps, a.shape[1]), lambda i, *_: (i, 0)) for a in arrays]
    shapes = [jax.ShapeDtypeStruct(a.shape, BF16) for a in arrays]
    return specs, shapes


def _inproj_kernel(x_ref, nw_ref, w_ref, cos_ref, sa_ref, sb_ref, *rest, n_cast, with_kt):
    cast_in, rest = rest[:n_cast], rest[n_cast:]
    q_ref, k_ref, v_ref, z_ref, xbc_ref, dt_ref = rest[:6]
    kt_ref = rest[6] if with_kt else None
    cast_out = rest[len(rest) - n_cast:]
    for src, dst in zip(cast_in, cast_out):
        dst[...] = src[...].astype(BF16)
    xn = _rms_rows(x_ref[...], nw_ref[...]).astype(BF16)
    cos, sa, sb = cos_ref[...], sa_ref[...], sb_ref[...]

    def rope(p):
        up = pltpu.roll(p, LANES - ROT_DIM // 2, axis=1)
        dn = pltpu.roll(p, ROT_DIM // 2, axis=1)
        return p * cos + up * sa + dn * sb

    col = 0
    for ref, width, rot in ((q_ref, Q_DIM, True), (k_ref, Q_DIM, True), (v_ref, D_ATT, False),
                            (z_ref, D_SSD, False), (xbc_ref, CONV_DIM, False), (dt_ref, SSD_HEADS, False)):
        for c0 in range(0, width, PROJ_CHUNK):
            cw = min(PROJ_CHUNK, width - c0)
            p = jnp.dot(xn, w_ref[:, col + c0:col + c0 + cw], preferred_element_type=F32)
            if rot:
                for s in range(0, cw, LANES):
                    roped = rope(p[:, s:s + LANES])
                    ref[:, c0 + s:c0 + s + LANES] = roped
                    if ref is k_ref and kt_ref is not None:
                        kt_ref[0, c0 + s:c0 + s + LANES, :] = roped.T
            else:
                ref[:, c0:c0 + cw] = p
        col += width


def _in_proj_tile(m):
    return min(256, m)


def _in_proj(x2d, nw, w_bf, cos_t, sa_t, sb_t, *, kt_seq_len=None, cast=()):
    m = x2d.shape[0]
    tm = _in_proj_tile(m)
    table_tiles = cos_t.shape[0] // tm
    row = lambda i: (i, 0)
    trow = lambda i: (i % table_tiles, 0)
    const = lambda i: (0, 0)
    widths = (Q_DIM, Q_DIM, D_ATT, D_SSD, CONV_DIM, SSD_HEADS)
    out_specs = [pl.BlockSpec((tm, w), row) for w in widths]
    out_shape = [jax.ShapeDtypeStruct((m, w), F32) for w in widths]
    if kt_seq_len is not None:
        tiles_per_seq = kt_seq_len // tm
        out_specs.append(pl.BlockSpec((1, Q_DIM, tm), lambda i: (i // tiles_per_seq, 0, i % tiles_per_seq)))
        out_shape.append(jax.ShapeDtypeStruct((m // kt_seq_len, Q_DIM, kt_seq_len), F32))
    cast_specs, cast_shapes = _cast_specs(cast, m // tm)
    kern = functools.partial(_inproj_kernel, n_cast=len(cast), with_kt=kt_seq_len is not None)
    return pl.pallas_call(
        kern,
        grid=(m // tm,),
        in_specs=[pl.BlockSpec((tm, D_MODEL), row),
                  pl.BlockSpec((1, D_MODEL), const),
                  pl.BlockSpec((D_MODEL, IN_PROJ_DIM), const, pipeline_mode=pl.Buffered(1)),
                  pl.BlockSpec((tm, LANES), trow), pl.BlockSpec((tm, LANES), trow), pl.BlockSpec((tm, LANES), trow)]
                 + cast_specs,
        out_specs=out_specs + cast_specs,
        out_shape=out_shape + cast_shapes,
        compiler_params=_cparams(("arbitrary",)),
        name="in_proj",
    )(x2d, nw, w_bf, cos_t, sa_t, sb_t, *cast)


def _rope_tables(pos):
    half = ROT_DIM // 2
    d = jnp.arange(LANES) % ATT_DK
    inv = ROPE_THETA ** (-(d % half).astype(F32) / half)
    ang = pos.astype(F32)[:, None] * inv[None, :]
    cos, sin = jnp.cos(ang), jnp.sin(ang)
    first, second = (d < half)[None, :], ((d >= half) & (d < ROT_DIM))[None, :]
    return (jnp.where(first | second, cos, 1.0), jnp.where(first, -sin, 0.0), jnp.where(second, sin, 0.0))


def _lambda_full(lam_ref, lam_init):
    l = lam_ref[...]
    s1 = jnp.sum(l[0:1] * l[1:2], axis=1, keepdims=True)
    s2 = jnp.sum(l[2:3] * l[3:4], axis=1, keepdims=True)
    return jnp.exp(s1) - jnp.exp(s2) + lam_init


def _prompt_prepare(k_ref, v_ref, kbf, vt, tq, n_tiles=None):
    n_tiles = k_ref.shape[1] // tq if n_tiles is None else n_tiles
    for c in range(n_tiles):
        kbf[c * tq:(c + 1) * tq, :] = k_ref[0, c * tq:(c + 1) * tq, :].astype(BF16)
        vt[:, c * tq:(c + 1) * tq] = v_ref[0, c * tq:(c + 1) * tq, :].T.astype(BF16)


def _prompt_qtile(qi, lam, subw_ref, q_ref, o_ref, kbf, vt, s_scr, p_scr, tq, lam_init, out_pos=None, scr_off=0):
    out_pos = qi if out_pos is None else out_pos
    lane = lax.broadcasted_iota(jnp.int32, (tq, LANES), 1)
    krow = lax.broadcasted_iota(jnp.int32, (tq, 2 * tq), 0)
    qcol = lax.broadcasted_iota(jnp.int32, (tq, 2 * tq), 1)
    diag = krow <= jnp.where(qcol >= tq, qcol - tq, qcol)
    contract_last = (((1,), (1,)), ((), ()))
    q = q_ref[0, qi * tq:(qi + 1) * tq, :] * (ATT_DK ** -0.5 * LOG2_E)
    qq = jnp.concatenate([jnp.where(lane < ATT_DK, q, 0.0), jnp.where(lane >= ATT_DK, q, 0.0)],
                         axis=0).astype(BF16)
    m = None
    for c in range(qi + 1):
        s = lax.dot_general(kbf[c * tq:(c + 1) * tq, :], qq, contract_last, preferred_element_type=F32)
        if c == qi:
            s = jnp.where(diag, s, NEG_INF)
        s_scr[scr_off + c * tq:scr_off + (c + 1) * tq, :] = s
        cm = jnp.max(s, axis=0, keepdims=True)
        m = cm if m is None else jnp.maximum(m, cm)
    l = jnp.zeros((1, 2 * tq), F32)
    for c in range(qi + 1):
        p = jnp.exp2(s_scr[scr_off + c * tq:scr_off + (c + 1) * tq, :] - m)
        l = l + jnp.sum(p, axis=0, keepdims=True)
        p_scr[scr_off + c * tq:scr_off + (c + 1) * tq, :] = p.astype(BF16)
    kv = (qi + 1) * tq
    o = jnp.dot(vt[:, :kv], p_scr[scr_off:scr_off + kv, :], preferred_element_type=F32) / l
    att = o[:, :tq] - lam * o[:, tq:]
    ms = jnp.mean(att * att, axis=0, keepdims=True)
    y = att * lax.rsqrt(ms + EPS) * subw_ref[...] * (1.0 - lam_init)
    o_ref[0, out_pos * tq:(out_pos + 1) * tq, :] = y.T


def _attn_prompt_kernel(lam_ref, subw_ref, q_ref, k_ref, v_ref, o_ref, kbf, vt, s_scr, p_scr, *, tq, lam_init):
    lam = _lambda_full(lam_ref, lam_init)
    _prompt_prepare(k_ref, v_ref, kbf, vt, tq)
    for qi in range(q_ref.shape[1] // tq):
        _prompt_qtile(qi, lam, subw_ref, q_ref, o_ref, kbf, vt, s_scr, p_scr, tq, lam_init)


def _prompt_scratch(s, tq, score_rows=None):
    score_rows = s if score_rows is None else score_rows
    return [pltpu.VMEM((s, LANES), BF16), pltpu.VMEM((ATT_DV, s), BF16),
            pltpu.VMEM((score_rows, 2 * tq), F32), pltpu.VMEM((score_rows, 2 * tq), BF16)]


def _attn_prompt(lam_vecs, subw_col, q, k, v, lam_init):
    b, s, _ = q.shape
    tq = min(PROMPT_Q_TILE, s)
    kern = functools.partial(_attn_prompt_kernel, tq=tq, lam_init=lam_init)
    head = pl.BlockSpec((1, s, LANES), lambda bb, h: (bb, 0, h))
    return pl.pallas_call(
        kern,
        grid=(b, N_ATT_HEADS),
        in_specs=[pl.BlockSpec((4, ATT_DK), lambda bb, h: (0, 0)),
                  pl.BlockSpec((ATT_DV, 1), lambda bb, h: (0, 0)),
                  head, head, head],
        out_specs=head,
        out_shape=jax.ShapeDtypeStruct((b, s, D_ATT), F32),
        scratch_shapes=_prompt_scratch(s, tq),
        compiler_params=_cparams(("arbitrary", "arbitrary")),
        name="attn_prompt",
    )(lam_vecs, subw_col, q, k, v)


def _expansion_matrix():
    tok = lax.broadcasted_iota(jnp.int32, (PAGE_SIZE, PAGE_SIZE * N_ATT_HEADS), 0)
    row = lax.broadcasted_iota(jnp.int32, (PAGE_SIZE, PAGE_SIZE * N_ATT_HEADS), 1)
    return (row // N_ATT_HEADS == tok).astype(BF16)


def _sample_attend(lam_ref, subw_ref, rexp_ref, q_ref, kn_ref, vn_ref, kpage, vpage, n_pages, o_ref, kbf, t_new,
                   lam_init):
    past = n_pages * PAGE_SIZE
    n_keys = past + PAGE_SIZE
    for j in range(n_pages):
        kbf[:, j * PAGE_SIZE:(j + 1) * PAGE_SIZE] = kpage(j).astype(BF16)
    zpad = jnp.zeros((PAGE_SIZE - t_new, Q_DIM), F32)
    k_tail = jnp.concatenate([kn_ref[0], zpad], axis=0).astype(BF16)

    lam = _lambda_full(lam_ref, lam_init)
    q = q_ref[0] * (ATT_DK ** -0.5)
    nr = t_new * N_ATT_HEADS
    qrep = jnp.concatenate([jnp.broadcast_to(q[t:t + 1], (N_ATT_HEADS, Q_DIM)) for t in range(t_new)], axis=0)
    row = lax.broadcasted_iota(jnp.int32, (nr, Q_DIM), 0)
    lane = lax.broadcasted_iota(jnp.int32, (nr, Q_DIM), 1)
    head = row % N_ATT_HEADS
    grp = lane // ATT_DK
    qq = jnp.concatenate([jnp.where(grp == 2 * head, qrep, 0.0), jnp.where(grp == 2 * head + 1, qrep, 0.0)],
                         axis=0).astype(BF16)
    s_past = jnp.dot(qq, kbf[...], preferred_element_type=F32)
    s_tail = lax.dot_general(qq, k_tail, (((1,), (1,)), ((), ())), preferred_element_type=F32)
    s = jnp.concatenate([s_past, s_tail], axis=1)
    srow = lax.broadcasted_iota(jnp.int32, (2 * nr, n_keys), 0)
    scol = lax.broadcasted_iota(jnp.int32, (2 * nr, n_keys), 1)
    tq = (srow // N_ATT_HEADS) % t_new
    s = jnp.where(scol <= past + tq, s, NEG_INF)
    m = jnp.max(s, axis=1, keepdims=True)
    p = jnp.exp(s - m)
    pn = p / jnp.sum(p, axis=1, keepdims=True)
    a = (pn[:nr] - lam * pn[nr:]).astype(BF16)
    a_pages = jnp.concatenate([a[:, j * PAGE_SIZE:(j + 1) * PAGE_SIZE] for j in range(n_pages + 1)], axis=0)
    a3 = jnp.dot(a_pages, rexp_ref[...], preferred_element_type=F32)
    hrow = lax.broadcasted_iota(jnp.int32, (N_ATT_HEADS, PAGE_SIZE * N_ATT_HEADS), 0)
    hcol = lax.broadcasted_iota(jnp.int32, (N_ATT_HEADS, PAGE_SIZE * N_ATT_HEADS), 1) % N_ATT_HEADS
    own = (hrow == hcol).astype(F32)
    a3 = (a3.reshape((n_pages + 1) * t_new, N_ATT_HEADS, PAGE_SIZE * N_ATT_HEADS) * own[None]).astype(BF16)
    a3 = a3.reshape((n_pages + 1) * nr, PAGE_SIZE * N_ATT_HEADS)
    v_tail = jnp.concatenate([vn_ref[0], jnp.zeros((PAGE_SIZE - nr, ATT_DV), F32)], axis=0)
    o = jnp.dot(a3[n_pages * nr:, :PAGE_SIZE], v_tail.astype(BF16), preferred_element_type=F32)
    for j in range(n_pages):
        o = o + jnp.dot(a3[j * nr:(j + 1) * nr], vpage(j).astype(BF16), preferred_element_type=F32)
    o_ref[0] = _rms_rows(o, subw_ref[...]) * (1.0 - lam_init)


def _attn_sample_kernel(pt_ref, lam_ref, subw_ref, q_ref, kn_ref, vn_ref, *rest, n_pages, t_new, lam_init):
    kpages, vpages = rest[:n_pages], rest[n_pages:2 * n_pages]
    rexp_ref, o_ref, kbf = rest[2 * n_pages:]
    _sample_attend(lam_ref, subw_ref, rexp_ref, q_ref, kn_ref, vn_ref, lambda j: kpages[j][0], lambda j: vpages[j][0],
                   n_pages, o_ref, kbf, t_new, lam_init)


def _sample_specs(page_table, t_new, seq_of_step=lambda s: s):
    n_pages = page_table.shape[1]
    tok = pl.BlockSpec((1, t_new, Q_DIM), lambda s, pt: (seq_of_step(s), 0, 0))
    rows = pl.BlockSpec((1, t_new * N_ATT_HEADS, ATT_DV), lambda s, pt: (seq_of_step(s), 0, 0))
    page_maps = [functools.partial(lambda s, pt, j: (pt[seq_of_step(s), j], 0, 0), j=j) for j in range(n_pages)]
    kpage_specs = [pl.BlockSpec((1, Q_DIM, PAGE_SIZE), pm) for pm in page_maps]
    vpage_specs = [pl.BlockSpec((1, PAGE_SIZE * N_ATT_HEADS, ATT_DV), pm) for pm in page_maps]
    in_specs = [pl.BlockSpec((4, ATT_DK), lambda b, pt: (0, 0)),
                pl.BlockSpec((1, ATT_DV), lambda b, pt: (0, 0)),
                tok, tok, rows] + kpage_specs + vpage_specs + [
                pl.BlockSpec((PAGE_SIZE, PAGE_SIZE * N_ATT_HEADS), lambda b, pt: (0, 0))]
    scratch = [pltpu.VMEM((Q_DIM, n_pages * PAGE_SIZE), BF16)]
    return in_specs, rows, scratch


def _attn_sample(page_table, lam_vecs, subw, q, kn, vn2, cache_k, cache_v, lam_init):
    db, t_new, _ = q.shape
    n_pages = page_table.shape[1]
    kern = functools.partial(_attn_sample_kernel, n_pages=n_pages, t_new=t_new, lam_init=lam_init)
    in_specs, out_spec, scratch = _sample_specs(page_table, t_new)
    grid_spec = pltpu.PrefetchScalarGridSpec(num_scalar_prefetch=1, grid=(db,), in_specs=in_specs,
                                             out_specs=out_spec, scratch_shapes=scratch)
    att = pl.pallas_call(
        kern,
        grid_spec=grid_spec,
        out_shape=jax.ShapeDtypeStruct((db, t_new * N_ATT_HEADS, ATT_DV), F32),
        compiler_params=_cparams(("arbitrary",)),
        name="attn_sample",
    )(page_table, lam_vecs, subw, q, kn, vn2, *([cache_k] * n_pages), *([cache_v] * n_pages), _expansion_matrix())
    return att.reshape(db, t_new, D_ATT)


def _fused_tile_plan(n_qtiles, steps_per_head):
    order = []
    lo, hi = 0, n_qtiles - 1
    while lo <= hi:
        order.append(lo)
        if hi != lo:
            order.append(hi)
        lo, hi = lo + 1, hi - 1
    per_step = n_qtiles // steps_per_head
    return [order[u * per_step:(u + 1) * per_step] for u in range(steps_per_head)]


def _page_copies(pt_ref, seq, slot, ck_hbm, cv_hbm, kpg, vpg, sems, n_pages):
    copies = []
    for j in range(n_pages):
        page = pt_ref[seq, j]
        copies.append(pltpu.make_async_copy(ck_hbm.at[page], kpg.at[slot, j], sems.at[slot, 0]))
        copies.append(pltpu.make_async_copy(cv_hbm.at[page], vpg.at[slot, j], sems.at[slot, 1]))
    return copies


def _attn_fused_kernel(pt_ref, lam_ref, subw_ref, q_ref, kn_ref, vn_ref, ck_hbm, cv_hbm, rexp_ref, subw_col_ref,
                       qp_ref, kp_ref, vp_ref, xbc_ref, dt_ref, z_ref, h0_ref, cw_ref, cb_ref, dtb_ref, alog_ref,
                       dskip_ref, nw_ref, o_ref, op_ref, y_ref, hout_ref, kpg, vpg, sems, kbf, kbf_p, vt_p, s_scr,
                       p_scr, xprev, *, n_pages, t_new, lam_init, tq, plan, ssd_rows, ssd_chunks):
    b = pl.program_id(0)
    slot = b % 2

    @pl.when(b == 0)
    def _():
        for c in _page_copies(pt_ref, 0, 0, ck_hbm, cv_hbm, kpg, vpg, sems, n_pages):
            c.start()

    @pl.when(b + 1 < pl.num_programs(0))
    def _():
        for c in _page_copies(pt_ref, b + 1, 1 - slot, ck_hbm, cv_hbm, kpg, vpg, sems, n_pages):
            c.start()

    for c in _page_copies(pt_ref, b, slot, ck_hbm, cv_hbm, kpg, vpg, sems, n_pages):
        c.wait()
    _sample_attend(lam_ref, subw_ref, rexp_ref, q_ref, kn_ref, vn_ref, lambda j: kpg[slot, j], lambda j: vpg[slot, j],
                   n_pages, o_ref, kbf, t_new, lam_init)

    sub = b % len(plan)
    lam = _lambda_full(lam_ref, lam_init)

    @pl.when(sub == 0)
    def _():
        _prompt_prepare(kp_ref, vp_ref, kbf_p, vt_p, tq)

    for u, tiles in enumerate(plan):
        @pl.when(sub == u)
        def _(tiles=tiles):
            scr_off = 0
            for qi in tiles:
                _prompt_qtile(qi, lam, subw_col_ref, qp_ref, op_ref, kbf_p, vt_p, s_scr, p_scr, tq, lam_init,
                              scr_off=scr_off)
                scr_off += (qi + 1) * tq

    @pl.when(b % ssd_chunks == 0)
    def _():
        hout_ref[...] = h0_ref[...]
        xprev[...] = jnp.zeros_like(xprev)

    _ssd_group(0, xbc_ref, dt_ref, z_ref, cw_ref, cb_ref, dtb_ref, alog_ref, dskip_ref, nw_ref, y_ref, hout_ref, xprev,
               seqs=1, seq_rows=ssd_rows, valid_rows=ssd_rows, carry=True)


def _fused_ssd_rows(db, bp, sp):
    if db % bp or sp % (db // bp):
        return 0
    rows = sp // (db // bp)
    return rows if rows % SUBLANES == 0 and rows <= 2 * SSD_CHUNK else 0


def _attn_fused_ok(db, bp, sp):
    heads = bp * N_ATT_HEADS
    tq = min(PROMPT_Q_TILE, sp)
    return db % heads == 0 and (sp // tq) % (db // heads) == 0 and _fused_ssd_rows(db, bp, sp) > 0


def _attn_fused(page_table, lam_vecs, subw, subw_col, q, kn, vn2, cache_k, cache_v, qp, kp, vp, lam_init, ssd_args):
    db, t_new, _ = q.shape
    bp, sp, _ = qp.shape
    n_pages = page_table.shape[1]
    tq = min(PROMPT_Q_TILE, sp)
    steps_per_head = db // (bp * N_ATT_HEADS)
    plan = _fused_tile_plan(sp // tq, steps_per_head)
    ssd_rows = _fused_ssd_rows(db, bp, sp)
    ssd_chunks = sp // ssd_rows
    kern = functools.partial(_attn_fused_kernel, n_pages=n_pages, t_new=t_new, lam_init=lam_init, tq=tq, plan=plan,
                             ssd_rows=ssd_rows, ssd_chunks=ssd_chunks)
    chunk = lambda width: pl.BlockSpec((1, ssd_rows, width), lambda b, pt: (b // ssd_chunks, b % ssd_chunks, 0))
    state_spec = pl.BlockSpec((1, SSD_HEADS, SSD_HEADDIM, SSD_STATE), lambda b, pt: (b // ssd_chunks, 0, 0, 0))
    tok = pl.BlockSpec((1, t_new, Q_DIM), lambda b, pt: (b, 0, 0))
    rows = pl.BlockSpec((1, t_new * N_ATT_HEADS, ATT_DV), lambda b, pt: (b, 0, 0))
    const = lambda b, pt: (0, 0)
    head = pl.BlockSpec((1, sp, LANES), lambda b, pt: (b // steps_per_head // N_ATT_HEADS, 0,
                                                       b // steps_per_head % N_ATT_HEADS))
    score_rows = max(sum(qi + 1 for qi in tiles) for tiles in plan) * tq
    grid_spec = pltpu.PrefetchScalarGridSpec(
        num_scalar_prefetch=1,
        grid=(db,),
        in_specs=[pl.BlockSpec((4, ATT_DK), const), pl.BlockSpec((1, ATT_DV), const), tok, tok, rows,
                  pl.BlockSpec(memory_space=pl.ANY), pl.BlockSpec(memory_space=pl.ANY),
                  pl.BlockSpec((PAGE_SIZE, PAGE_SIZE * N_ATT_HEADS), const),
                  pl.BlockSpec((ATT_DV, 1), const), head, head, head,
                  chunk(CONV_DIM), chunk(SSD_HEADS), chunk(D_SSD), state_spec,
                  pl.BlockSpec((SSD_CONV, CONV_DIM), const), pl.BlockSpec((1, CONV_DIM), const),
                  pl.BlockSpec((1, SSD_HEADS), const), pl.BlockSpec((1, SSD_HEADS), const),
                  pl.BlockSpec((1, SSD_HEADS), const), pl.BlockSpec((1, D_SSD), const)],
        out_specs=[rows, head, chunk(D_SSD), state_spec],
        scratch_shapes=[pltpu.VMEM((2, n_pages, Q_DIM, PAGE_SIZE), F32),
                        pltpu.VMEM((2, n_pages, PAGE_SIZE * N_ATT_HEADS, ATT_DV), F32),
                        pltpu.SemaphoreType.DMA((2, 2)),
                        pltpu.VMEM((Q_DIM, n_pages * PAGE_SIZE), BF16)]
                       + _prompt_scratch(sp, tq, score_rows=score_rows)
                       + [pltpu.VMEM((1, SUBLANES, CONV_DIM), F32)],
    )
    att_s, att_p, y_p, ssm_p = pl.pallas_call(
        kern,
        grid_spec=grid_spec,
        out_shape=[jax.ShapeDtypeStruct((db, t_new * N_ATT_HEADS, ATT_DV), F32),
                   jax.ShapeDtypeStruct((bp, sp, D_ATT), F32),
                   jax.ShapeDtypeStruct((bp, sp, D_SSD), F32),
                   jax.ShapeDtypeStruct((bp, SSD_HEADS, SSD_HEADDIM, SSD_STATE), F32)],
        compiler_params=pltpu.CompilerParams(dimension_semantics=("arbitrary",),
                                             vmem_limit_bytes=FUSED_ATTN_VMEM_LIMIT_BYTES),
        name="attn_fused",
    )(page_table, lam_vecs, subw, q, kn, vn2, cache_k, cache_v, _expansion_matrix(), subw_col, qp, kp, vp, *ssd_args)
    return att_s.reshape(db, t_new, D_ATT), att_p, y_p, ssm_p


def _expand_heads(v, rows):
    return jnp.concatenate([jnp.broadcast_to(v[:, h:h + 1], (rows, SSD_HEADDIM)) for h in range(SSD_HEADS)], axis=1)


def _expand_heads_mxu(vals):
    rows = vals[0].shape[0]
    parts = []
    for v in vals:
        hi = v.astype(BF16).astype(F32)
        mid = (v - hi).astype(BF16).astype(F32)
        lo = v - hi - mid
        parts.append(jnp.concatenate([hi, mid, lo], axis=1))
    lhs = jnp.concatenate(parts, axis=0).astype(BF16)
    head = lax.broadcasted_iota(jnp.int32, (3 * SSD_HEADS, D_SSD), 0) % SSD_HEADS
    col = lax.broadcasted_iota(jnp.int32, (3 * SSD_HEADS, D_SSD), 1) // SSD_HEADDIM
    out = jnp.dot(lhs, (head == col).astype(BF16), preferred_element_type=F32)
    return [out[i * rows:(i + 1) * rows] for i in range(len(vals))]


def _ssd_kernel(xbc_ref, dt_ref, z_ref, h0_ref, cw_ref, cb_ref, dtb_ref, alog_ref, dskip_ref, nw_ref,
                y_ref, hout_ref, xprev, *, groups, seqs, seq_rows, valid_rows, carry):
    c = pl.program_id(1)

    @pl.when(c == 0)
    def _():
        hout_ref[...] = h0_ref[...]
        if carry:
            xprev[...] = jnp.zeros_like(xprev)

    for gi in range(groups):
        _ssd_group(gi, xbc_ref, dt_ref, z_ref, cw_ref, cb_ref, dtb_ref, alog_ref, dskip_ref, nw_ref,
                   y_ref, hout_ref, xprev, seqs=seqs, seq_rows=seq_rows, valid_rows=valid_rows, carry=carry)


def _ssd_group(gi, xbc_ref, dt_ref, z_ref, cw_ref, cb_ref, dtb_ref, alog_ref, dskip_ref, nw_ref,
               y_ref, hout_ref, xprev, *, seqs, seq_rows, valid_rows, carry):
    rows = seqs * seq_rows
    xr = xbc_ref[gi]
    conv = cb_ref[...] + cw_ref[SSD_CONV - 1:SSD_CONV] * xr
    if carry:
        pv = xprev[gi]
        row8 = lax.broadcasted_iota(jnp.int32, (SUBLANES, CONV_DIM), 0)
    for k in range(1, SSD_CONV):
        sh = pltpu.roll(xr, k, axis=0)
        if carry:
            top = jnp.where(row8 < k, pltpu.roll(pv, k, axis=0), sh[:SUBLANES])
            sh = jnp.concatenate([top, sh[SUBLANES:]], axis=0)
        conv = conv + cw_ref[SSD_CONV - 1 - k:SSD_CONV - k] * sh
    if carry:
        xprev[gi] = xr[rows - SUBLANES:]
    xc = _silu(conv)
    xs = xc[:, :D_SSD]

    dtv = dt_ref[gi] + dtb_ref[...]
    dtv = jnp.maximum(dtv, 0.0) + jnp.log1p(jnp.exp(-jnp.abs(dtv)))
    if valid_rows < seq_rows:
        rvalid = lax.broadcasted_iota(jnp.int32, (rows, SSD_HEADS), 0) % seq_rows < valid_rows
        dtv = jnp.where(rvalid, dtv, 0.0)
    a = dtv * (-jnp.exp(alog_ref[...]))
    ri = lax.broadcasted_iota(jnp.int32, (rows, rows), 0)
    ci = lax.broadcasted_iota(jnp.int32, (rows, rows), 1)
    tril = ri >= ci
    if seqs > 1:
        tril = tril & (ri // seq_rows == ci // seq_rows)
    a_cs = jnp.dot(tril.astype(F32), a, preferred_element_type=F32, precision=lax.Precision.HIGHEST)
    totals = [a_cs[(s + 1) * seq_rows - 1:(s + 1) * seq_rows] for s in range(seqs)]
    e_tots = [jnp.exp(t) for t in totals]
    total = totals[0] if seqs == 1 else jnp.concatenate(
        [jnp.broadcast_to(t, (seq_rows, SSD_HEADS)) for t in totals], axis=0)
    dt_exp, decay_exp, e_exp = _expand_heads_mxu([dtv, jnp.exp(total - a_cs), jnp.exp(a_cs)])
    xdt = xs * dt_exp
    xdtd = (xdt * decay_exp).astype(BF16)
    eye = ri == ci
    s0 = gi * seqs

    y_diag_parts, y_off_parts = [], []
    heads_per_group = SSD_HEADS // SSD_GROUPS
    gw = heads_per_group * SSD_HEADDIM
    contract_last = (((1,), (1,)), ((), ()))
    contract_first = (((0,), (0,)), ((), ()))
    for g in range(SSD_GROUPS):
        bg = xc[:, D_SSD + g * SSD_STATE:D_SSD + (g + 1) * SSD_STATE].astype(BF16)
        cg = xc[:, D_SSD + (SSD_GROUPS + g) * SSD_STATE:D_SSD + (SSD_GROUPS + g + 1) * SSD_STATE].astype(BF16)
        y_off_seq, st_seq = [], []
        for s in range(seqs):
            rs = slice(s * seq_rows, (s + 1) * seq_rows)
            hg = hout_ref[s0 + s, g * heads_per_group:(g + 1) * heads_per_group].reshape(gw, SSD_STATE)
            y_off_seq.append(lax.dot_general(cg[rs], hg.astype(BF16), contract_last, preferred_element_type=F32))
            st_seq.append(lax.dot_general(xdtd[rs, g * gw:(g + 1) * gw], bg[rs], contract_first,
                                          preferred_element_type=F32))
        y_off = y_off_seq[0] if seqs == 1 else jnp.concatenate(y_off_seq, axis=0)
        cbm = lax.dot_general(cg, bg, contract_last, preferred_element_type=F32)
        for r in range(heads_per_group):
            h = g * heads_per_group + r
            col = a_cs[:, h:h + 1]
            rowv = jnp.sum(jnp.where(eye, col, 0.0), axis=0, keepdims=True)
            lmat = jnp.where(tril, jnp.exp(col - rowv), 0.0)
            mm = (cbm * lmat).astype(BF16)
            y_diag_parts.append(jnp.dot(mm, xdt[:, h * SSD_HEADDIM:(h + 1) * SSD_HEADDIM].astype(BF16),
                                        preferred_element_type=F32))
            for s in range(seqs):
                hout_ref[s0 + s, h] = (hout_ref[s0 + s, h] * e_tots[s][:, h:h + 1]
                                       + st_seq[s][r * SSD_HEADDIM:(r + 1) * SSD_HEADDIM])
        y_off_parts.append(y_off)
    y = (jnp.concatenate(y_diag_parts, axis=1) + jnp.concatenate(y_off_parts, axis=1) * e_exp
         + _expand_heads(dskip_ref[...], 1) * xs)
    zz = z_ref[gi]
    y = y * _silu(zz)
    half = D_SSD // SSD_GROUPS
    nw = nw_ref[...]
    y_ref[gi] = jnp.concatenate([_rms_rows(y[:, g * half:(g + 1) * half], nw[:, g * half:(g + 1) * half])
                                for g in range(SSD_GROUPS)], axis=1)


def _ssd(xbc, dt, z, h0, cw, cb, dtb, alog, dskip, nw, *, seqs, seq_rows, valid_rows):
    b, s, _ = xbc.shape
    rows = seqs * seq_rows
    n_chunks = s // rows
    assert seqs == 1 or n_chunks == 1
    groups = math.gcd(b, SSD_GROUPS_PER_STEP) if seqs == 1 else 1
    kern = functools.partial(_ssd_kernel, groups=groups, seqs=seqs, seq_rows=seq_rows, valid_rows=valid_rows,
                             carry=seqs == 1)
    tokmap = lambda bb, c: (bb, c, 0)
    seqmap4 = lambda bb, c: (bb, 0, 0, 0)
    const = lambda bb, c: (0, 0)
    state_spec = pl.BlockSpec((groups * seqs, SSD_HEADS, SSD_HEADDIM, SSD_STATE), seqmap4)
    return pl.pallas_call(
        kern,
        grid=(b // groups, n_chunks),
        in_specs=[pl.BlockSpec((groups, rows, CONV_DIM), tokmap),
                  pl.BlockSpec((groups, rows, SSD_HEADS), tokmap),
                  pl.BlockSpec((groups, rows, D_SSD), tokmap),
                  state_spec,
                  pl.BlockSpec((SSD_CONV, CONV_DIM), const),
                  pl.BlockSpec((1, CONV_DIM), const),
                  pl.BlockSpec((1, SSD_HEADS), const),
                  pl.BlockSpec((1, SSD_HEADS), const),
                  pl.BlockSpec((1, SSD_HEADS), const),
                  pl.BlockSpec((1, D_SSD), const)],
        out_specs=[pl.BlockSpec((groups, rows, D_SSD), tokmap), state_spec],
        out_shape=[jax.ShapeDtypeStruct((b, s, D_SSD), F32),
                   jax.ShapeDtypeStruct((b * seqs, SSD_HEADS, SSD_HEADDIM, SSD_STATE), F32)],
        scratch_shapes=[pltpu.VMEM((groups, SUBLANES, CONV_DIM), F32)],
        compiler_params=_cparams(("arbitrary", "arbitrary")),
        name="ssd",
    )(xbc, dt, z, h0, cw, cb, dtb, alog, dskip, nw)


def _outproj_kernel(x_ref, att_ref, y_ref, wa_ref, wy_ref, nw_ref, *rest, n_cast):
    cast_in, (x1_ref, hf_ref), cast_out = rest[:n_cast], rest[n_cast:n_cast + 2], rest[n_cast + 2:]
    for src, dst in zip(cast_in, cast_out):
        dst[...] = src[...].astype(BF16)
    mix = jnp.dot(att_ref[...].astype(BF16), wa_ref[...], preferred_element_type=F32)
    mix = mix + jnp.dot(y_ref[...].astype(BF16), wy_ref[...], preferred_element_type=F32)
    x1 = x_ref[...] + mix
    x1_ref[...] = x1
    hf_ref[...] = _rms_rows(x1, nw_ref[...]).astype(BF16)


def _out_proj(x2d, att, y, w_out, nw, *, cast=()):
    assert D_ATT == D_SSD
    m = x2d.shape[0]
    tm = min(512, m)
    row = lambda i: (i, 0)
    const = lambda i: (0, 0)
    cast_specs, cast_shapes = _cast_specs(cast, m // tm)
    return pl.pallas_call(
        functools.partial(_outproj_kernel, n_cast=len(cast)),
        grid=(m // tm,),
        in_specs=[pl.BlockSpec((tm, D_MODEL), row), pl.BlockSpec((tm, D_ATT), row), pl.BlockSpec((tm, D_SSD), row),
                  pl.BlockSpec((D_ATT, D_MODEL), const, pipeline_mode=pl.Buffered(1)),
                  pl.BlockSpec((D_SSD, D_MODEL), lambda i: (1, 0), pipeline_mode=pl.Buffered(1)),
                  pl.BlockSpec((1, D_MODEL), const)] + cast_specs,
        out_specs=[pl.BlockSpec((tm, D_MODEL), row), pl.BlockSpec((tm, D_MODEL), row)] + cast_specs,
        out_shape=[jax.ShapeDtypeStruct((m, D_MODEL), F32), jax.ShapeDtypeStruct((m, D_MODEL), BF16)] + cast_shapes,
        compiler_params=_cparams(("arbitrary",)),
        name="out_proj",
    )(x2d, att, y, w_out, w_out, nw, *cast)


def _ffn_kernel(x1_ref, hf_ref, wg_ref, wu_ref, wd_ref, cw_ref, cb_ref, nfw_ref, *rest,
                tm, seq_len, tiles_per_seq):
    if seq_len >= tm:
        y_ref, gt_ref, gprev = rest
    else:
        st_ref, y_ref, gt_ref, p1_scr, p2_scr, g_scr = rest
    i = pl.program_id(0)
    f = pl.program_id(1)
    nf = pl.num_programs(1)

    @pl.when(f == 0)
    def _():
        y_ref[...] = jnp.zeros_like(y_ref)

    hf = hf_ref[...]
    g = jnp.dot(hf, wg_ref[...], preferred_element_type=F32)
    u = jnp.dot(hf, wu_ref[...], preferred_element_type=F32)
    tf = g.shape[1]
    if seq_len >= tm:
        gp = gprev[f]
        gp = jnp.where(i % tiles_per_seq == 0, 0.0, gp)
        row8 = lax.broadcasted_iota(jnp.int32, (SUBLANES, tf), 0)
        shifted = []
        for k in range(1, FFN_CONV):
            sh = pltpu.roll(g, k, axis=0)
            top = jnp.where(row8 < k, pltpu.roll(gp, k, axis=0), sh[:SUBLANES])
            shifted.append(jnp.concatenate([top, sh[SUBLANES:]], axis=0))
        g1, g2 = shifted
        gprev[f] = g[tm - SUBLANES:]
        gt_ref[0] = g[tm - SUBLANES:]
    else:
        ns = tm // seq_len

        @pl.when((i == 0) & (f == 0))
        def _():
            p1_scr[...] = jnp.zeros_like(p1_scr)
            p2_scr[...] = jnp.zeros_like(p2_scr)

        for c in range(tf // LANES):
            cs = slice(c * LANES, (c + 1) * LANES)
            g_scr[c] = g[:, cs]
            p1_scr[c, pl.ds(0, ns, stride=seq_len), :] = st_ref[1, :, cs]
            p2_scr[c, pl.ds(0, ns, stride=seq_len), :] = st_ref[0, :, cs]
            p2_scr[c, pl.ds(1, ns, stride=seq_len), :] = st_ref[1, :, cs]
            gt_ref[0, :, cs] = g_scr[c, pl.ds(seq_len - 2, ns, stride=seq_len), :]
            gt_ref[1, :, cs] = g_scr[c, pl.ds(seq_len - 1, ns, stride=seq_len), :]
        p1 = jnp.concatenate([p1_scr[c] for c in range(tf // LANES)], axis=1)
        p2 = jnp.concatenate([p2_scr[c] for c in range(tf // LANES)], axis=1)
        pos = lax.broadcasted_iota(jnp.int32, (tm, tf), 0) % seq_len
        g1 = jnp.where(pos >= 1, pltpu.roll(g, 1, axis=0), 0.0) + p1
        g2 = jnp.where(pos >= 2, pltpu.roll(g, 2, axis=0), 0.0) + p2
    gc = cb_ref[...] + cw_ref[0:1] * g2 + cw_ref[1:2] * g1 + cw_ref[2:3] * g
    act = (_silu(gc) * u).astype(BF16)
    y_ref[...] += jnp.dot(act, wd_ref[...], preferred_element_type=F32)

    @pl.when(f == nf - 1)
    def _():
        y_ref[...] = _rms_rows(x1_ref[...] + y_ref[...], nfw_ref[...])


def _ffn(x1, hf, wg, wu, wd, cw, cb, nfw, state, *, seq_len):
    m = x1.shape[0]
    tm = min(FFN_ROWS, m)
    prompt_mode = seq_len >= tm
    tf = FFN_TILE
    nf = D_FF // tf
    tiles_per_seq = max(seq_len // tm, 1)
    kern = functools.partial(_ffn_kernel, tm=tm, seq_len=seq_len, tiles_per_seq=tiles_per_seq)
    row = lambda i, f: (i, 0)
    in_specs = [pl.BlockSpec((tm, D_MODEL), row), pl.BlockSpec((tm, D_MODEL), row),
                pl.BlockSpec((D_MODEL, tf), lambda i, f: (0, f)), pl.BlockSpec((D_MODEL, tf), lambda i, f: (0, f)),
                pl.BlockSpec((tf, D_MODEL), lambda i, f: (f, 0)),
                pl.BlockSpec((FFN_CONV, tf), lambda i, f: (0, f)), pl.BlockSpec((1, tf), lambda i, f: (0, f)),
                pl.BlockSpec((1, D_MODEL), lambda i, f: (0, 0))]
    args = [x1, hf, wg, wu, wd, cw, cb, nfw]
    scratch = []
    if prompt_mode:
        out_specs = [pl.BlockSpec((tm, D_MODEL), row), pl.BlockSpec((1, SUBLANES, tf), lambda i, f: (i, 0, f))]
        out_shape = [jax.ShapeDtypeStruct((m, D_MODEL), F32), jax.ShapeDtypeStruct((m // tm, SUBLANES, D_FF), F32)]
        scratch.append(pltpu.VMEM((nf, SUBLANES, tf), F32))
    else:
        ns = tm // seq_len
        state_spec = pl.BlockSpec((FFN_CONV - 1, ns, tf), lambda i, f: (0, i, f))
        in_specs.append(state_spec)
        args.append(state)
        out_specs = [pl.BlockSpec((tm, D_MODEL), row), state_spec]
        out_shape = [jax.ShapeDtypeStruct((m, D_MODEL), F32), jax.ShapeDtypeStruct(state.shape, F32)]
        scratch += [pltpu.VMEM((tf // LANES, tm, LANES), F32)] * 3
    return pl.pallas_call(
        kern,
        grid=(m // tm, nf),
        in_specs=in_specs,
        out_specs=out_specs,
        out_shape=out_shape,
        scratch_shapes=scratch,
        compiler_params=_cparams(("arbitrary", "arbitrary")),
        name="ffn",
    )(*args)


def _layer(xp, xs, past, lam_init, p):
    (cache_k, cache_v, state_ssm, state_conv_ssd, state_conv_ffn, page_table) = past
    bp, sp, _ = xp.shape
    db, ds, _ = xs.shape
    n_pages = page_table.shape[1]
    past_len = n_pages * PAGE_SIZE

    w_in = p["w_in"].astype(BF16)
    nmw = p["norm_mix_w"].reshape(1, D_MODEL)
    lam_vecs = jnp.stack([p["lambda_q1"], p["lambda_k1"], p["lambda_q2"], p["lambda_k2"]])
    subw = p["subln_w"].reshape(1, ATT_DV)
    cw_ssd = p["conv_ssd_w"]
    cb_ssd = p["conv_ssd_b"].reshape(1, CONV_DIM)
    dtb = p["dt_bias"].reshape(1, SSD_HEADS)
    alog = p["a_log"].reshape(1, SSD_HEADS)
    dskip = p["d_skip"].reshape(1, SSD_HEADS)
    nsw = p["norm_ssd_w"].reshape(1, D_SSD)
    nfw = p["norm_ffn_w"].reshape(1, D_MODEL)
    cw_ffn = p["conv_ffn_w"]
    cb_ffn = p["conv_ffn_b"].reshape(1, D_FF)
    nfin = p["norm_final_w"].reshape(1, D_MODEL)

    q, k, v, z, xbc, dt, kt, wg, wu, w_out = _in_proj(
        xp.reshape(bp * sp, D_MODEL), nmw, w_in, *_rope_tables(jnp.arange(sp)), kt_seq_len=sp,
        cast=(p["w_gate"], p["w_up"], p["w_out"]))
    r3 = lambda t: t.reshape(bp, sp, t.shape[-1])
    pos_s = jnp.tile(past_len + jnp.arange(ds), _in_proj_tile(db * ds) // ds)
    q_s, k_s, v_s, z_s, xbc_s, dt_s = _in_proj(xs.reshape(db * ds, D_MODEL), nmw, w_in, *_rope_tables(pos_s))
    r3s = lambda t: t.reshape(db, ds, t.shape[-1])
    n_phys = cache_k.shape[0]
    cache_kt = jnp.transpose(cache_k, (0, 2, 3, 4, 1)).reshape(n_phys, Q_DIM, PAGE_SIZE)
    cache_vr = cache_v.reshape(n_phys, PAGE_SIZE * N_ATT_HEADS, ATT_DV)
    subw_col = subw.reshape(ATT_DV, 1)
    vn2 = v_s.reshape(db, ds * N_ATT_HEADS, ATT_DV)
    ssd_args = (r3(xbc), r3(dt), r3(z), jnp.zeros((bp, SSD_HEADS, SSD_HEADDIM, SSD_STATE), F32),
                cw_ssd, cb_ssd, dtb, alog, dskip, nsw)
    if _attn_fused_ok(db, bp, sp):
        att_s, att, y, ssm_p = _attn_fused(page_table, lam_vecs, subw, subw_col, r3s(q_s), r3s(k_s), vn2,
                                           cache_kt, cache_vr, r3(q), r3(k), r3(v), lam_init, ssd_args)
    else:
        att = _attn_prompt(lam_vecs, subw_col, r3(q), r3(k), r3(v), lam_init)
        att_s = _attn_sample(page_table, lam_vecs, subw, r3s(q_s), r3s(k_s), vn2, cache_kt, cache_vr, lam_init)
        rows = SSD_CHUNK if sp % SSD_CHUNK == 0 else sp
        y, ssm_p = _ssd(*ssd_args, seqs=1, seq_rows=rows, valid_rows=rows)

    x1, hf, wd = _out_proj(xp.reshape(bp * sp, D_MODEL), att.reshape(bp * sp, D_ATT), y.reshape(bp * sp, D_SSD),
                           w_out, nfw, cast=(p["w_down"],))
    yp, gtail = _ffn(x1, hf, wg, wu, wd, cw_ffn, cb_ffn, nfin, None, seq_len=sp)
    tiles_per_seq = gtail.shape[0] // bp
    conv_ffn_p = gtail.reshape(bp, tiles_per_seq, SUBLANES, D_FF)[:, -1, SUBLANES - (FFN_CONV - 1):]
    new_k_p = jnp.transpose(kt.reshape(bp, N_ATT_HEADS, 2, ATT_DK, sp), (0, 4, 1, 2, 3))
    prompt_out = (yp.reshape(bp, sp, D_MODEL), new_k_p, v.reshape(bp, sp, N_ATT_HEADS, ATT_DV),
                  ssm_p, r3(xbc)[:, sp - (SSD_CONV - 1):], conv_ffn_p)

    k, v, z, xbc, dt, att = k_s, v_s, z_s, xbc_s, dt_s, att_s
    gs = math.gcd(db, SSD_SAMPLE_GROUP)
    pad8 = lambda t: jnp.pad(t, ((0, 0), (0, SUBLANES - ds), (0, 0))).reshape(db // gs, gs * SUBLANES, t.shape[-1])
    nxt_state = jnp.roll(state_conv_ssd.reshape(db // gs, gs, SSD_CONV - 1, CONV_DIM), -1, axis=1)
    xe = jnp.concatenate([r3s(xbc), jnp.zeros((db, SUBLANES - ds - (SSD_CONV - 1), CONV_DIM), F32),
                          nxt_state.reshape(db, SSD_CONV - 1, CONV_DIM)], axis=1)
    y8, ssm_s = _ssd(xe.reshape(db // gs, gs * SUBLANES, CONV_DIM), pad8(r3s(dt)), pad8(r3s(z)), state_ssm,
                     cw_ssd, cb_ssd, dtb, alog, dskip, nsw, seqs=gs, seq_rows=SUBLANES, valid_rows=ds)
    y = y8.reshape(db, SUBLANES, D_SSD)[:, :ds]
    x1, hf = _out_proj(xs.reshape(db * ds, D_MODEL), att.reshape(db * ds, D_ATT), y.reshape(db * ds, D_SSD), w_out, nfw)
    ys, conv_ffn_s = _ffn(x1, hf, wg, wu, wd, cw_ffn, cb_ffn, nfin, jnp.transpose(state_conv_ffn, (1, 0, 2)),
                          seq_len=ds)
    sample_out = (ys.reshape(db, ds, D_MODEL),
                  k.reshape(db, ds, N_ATT_HEADS, 2, ATT_DK), v.reshape(db, ds, N_ATT_HEADS, ATT_DV),
                  ssm_s, r3s(xbc)[:, ds - (SSD_CONV - 1):], jnp.transpose(conv_ffn_s, (1, 0, 2)))
    return prompt_out, sample_out


def kernel(x_prompt, x_sample, cache_k, cache_v, state_ssm, state_conv_ssd, state_conv_ffn, page_table, norm_mix_w, w_in, lambda_q1, lambda_k1, lambda_q2, lambda_k2, subln_w, conv_ssd_w, conv_ssd_b, dt_bias, a_log, d_skip, norm_ssd_w, w_out, norm_ffn_w, w_gate, w_up, conv_ffn_w, conv_ffn_b, w_down, norm_final_w):
    depth = w_in.shape[0]
    assert depth == 1, "the final RMSNorm is fused into the (single) layer's FFN kernel"
    lam_init = 0.8 - 0.6 * math.exp(-0.3 * 0)
    params = dict(norm_mix_w=norm_mix_w[0], w_in=w_in[0], lambda_q1=lambda_q1[0], lambda_k1=lambda_k1[0],
                  lambda_q2=lambda_q2[0], lambda_k2=lambda_k2[0], subln_w=subln_w[0], conv_ssd_w=conv_ssd_w[0],
                  conv_ssd_b=conv_ssd_b[0], dt_bias=dt_bias[0], a_log=a_log[0], d_skip=d_skip[0],
                  norm_ssd_w=norm_ssd_w[0], w_out=w_out[0], norm_ffn_w=norm_ffn_w[0], w_gate=w_gate[0],
                  w_up=w_up[0], conv_ffn_w=conv_ffn_w[0], conv_ffn_b=conv_ffn_b[0], w_down=w_down[0],
                  norm_final_w=norm_final_w)
    past = (cache_k[0], cache_v[0], state_ssm[0], state_conv_ssd[0], state_conv_ffn[0], page_table)
    (yp, kp, vp, sp_, cp, fp), (ys, ks, vs, ss, cs, fs) = _layer(x_prompt, x_sample, past, lam_init, params)
    lead = lambda t: t[None]
    return (yp, ys, lead(kp), lead(vp), lead(sp_), lead(cp), lead(fp),
            lead(ks), lead(vs), lead(ss), lead(cs), lead(fs))
```

```python
import functools
import math

import jax
import jax.numpy as jnp
from jax import lax
from jax.experimental import pallas as pl
from jax.experimental.pallas import tpu as pltpu

F32 = jnp.float32
BF16 = jnp.bfloat16

D_MODEL = 2048
ATT_DK = 64
ATT_DV = 128
N_ATT_HEADS = 8
ROT_DIM = 16
ROPE_THETA = 500000.0
Q_DIM = N_ATT_HEADS * 2 * ATT_DK
D_ATT = N_ATT_HEADS * ATT_DV
D_SSD = 1024
SSD_HEADDIM = 64
SSD_HEADS = 16
SSD_GROUPS = 2
SSD_STATE = 128
SSD_CONV = 4
SSD_CHUNK = 128
CONV_DIM = D_SSD + 2 * SSD_GROUPS * SSD_STATE
D_FF = 5632
FFN_CONV = 3
EPS = 1e-6
PAGE_SIZE = 128

LANES = 128
SUBLANES = 8
VMEM_LIMIT_BYTES = 56 * 1024 * 1024
IN_PROJ_DIM = 2 * Q_DIM + D_ATT + D_SSD + CONV_DIM + SSD_HEADS
PROJ_CHUNK = 512
SSD_SAMPLE_GROUP = 16
SSD_GROUPS_PER_STEP = 4
PROMPT_Q_TILE = 256
FFN_TILE = 512
FFN_ROWS = 512
FUSED_ATTN_VMEM_LIMIT_BYTES = 60 * 1024 * 1024
LOG2_E = math.log2(math.e)
NEG_INF = float("-inf")


def _cparams(sem):
    return pltpu.CompilerParams(dimension_semantics=sem, vmem_limit_bytes=VMEM_LIMIT_BYTES)


def _silu(x):
    h = 0.5 * x
    return h + h * jnp.tanh(h)


def _rms_rows(x, w):
    return x * lax.rsqrt(jnp.mean(x * x, axis=-1, keepdims=True) + EPS) * w


def _cast_specs(arrays, n_steps):
    specs = [pl.BlockSpec((a.shape[0] // n_steps, a.shape[1]), lambda i, *_: (i, 0)) for a in arrays]
    shapes = [jax.ShapeDtypeStruct(a.shape, BF16) for a in arrays]
    return specs, shapes


def _inproj_kernel(x_ref, nw_ref, w_ref, cos_ref, sa_ref, sb_ref, *rest, n_cast, with_kt):
    cast_in, rest = rest[:n_cast], rest[n_cast:]
    q_ref, k_ref, v_ref, z_ref, xbc_ref, dt_ref = rest[:6]
    kt_ref = rest[6] if with_kt else None
    cast_out = rest[len(rest) - n_cast:]
    for src, dst in zip(cast_in, cast_out):
        dst[...] = src[...].astype(BF16)
    xn = _rms_rows(x_ref[...], nw_ref[...]).astype(BF16)
    cos, sa, sb = cos_ref[...], sa_ref[...], sb_ref[...]

    def rope(p):
        up = pltpu.roll(p, LANES - ROT_DIM // 2, axis=1)
        dn = pltpu.roll(p, ROT_DIM // 2, axis=1)
        return p * cos + up * sa + dn * sb

    col = 0
    for ref, width, rot in ((q_ref, Q_DIM, True), (k_ref, Q_DIM, True), (v_ref, D_ATT, False),
                            (z_ref, D_SSD, False), (xbc_ref, CONV_DIM, False), (dt_ref, SSD_HEADS, False)):
        for c0 in range(0, width, PROJ_CHUNK):
            cw = min(PROJ_CHUNK, width - c0)
            p = jnp.dot(xn, w_ref[:, col + c0:col + c0 + cw], preferred_element_type=F32)
            if rot:
                for s in range(0, cw, LANES):
                    roped = rope(p[:, s:s + LANES])
                    ref[:, c0 + s:c0 + s + LANES] = roped
                    if ref is k_ref and kt_ref is not None:
                        kt_ref[0, c0 + s:c0 + s + LANES, :] = roped.T
            else:
                ref[:, c0:c0 + cw] = p
        col += width


def _in_proj_tile(m):
    return min(256, m)


def _in_proj(x2d, nw, w_bf, cos_t, sa_t, sb_t, *, kt_seq_len=None, cast=()):
    m = x2d.shape[0]
    tm = _in_proj_tile(m)
    table_tiles = cos_t.shape[0] // tm
    row = lambda i: (i, 0)
    trow = lambda i: (i % table_tiles, 0)
    const = lambda i: (0, 0)
    widths = (Q_DIM, Q_DIM, D_ATT, D_SSD, CONV_DIM, SSD_HEADS)
    out_specs = [pl.BlockSpec((tm, w), row) for w in widths]
    out_shape = [jax.ShapeDtypeStruct((m, w), F32) for w in widths]
    if kt_seq_len is not None:
        tiles_per_seq = kt_seq_len // tm
        out_specs.append(pl.BlockSpec((1, Q_DIM, tm), lambda i: (i // tiles_per_seq, 0, i % tiles_per_seq)))
        out_shape.append(jax.ShapeDtypeStruct((m // kt_seq_len, Q_DIM, kt_seq_len), F32))
    cast_specs, cast_shapes = _cast_specs(cast, m // tm)
    kern = functools.partial(_inproj_kernel, n_cast=len(cast), with_kt=kt_seq_len is not None)
    return pl.pallas_call(
        kern,
        grid=(m // tm,),
        in_specs=[pl.BlockSpec((tm, D_MODEL), row),
                  pl.BlockSpec((1, D_MODEL), const),
                  pl.BlockSpec((D_MODEL, IN_PROJ_DIM), const, pipeline_mode=pl.Buffered(1)),
                  pl.BlockSpec((tm, LANES), trow), pl.BlockSpec((tm, LANES), trow), pl.BlockSpec((tm, LANES), trow)]
                 + cast_specs,
        out_specs=out_specs + cast_specs,
        out_shape=out_shape + cast_shapes,
        compiler_params=_cparams(("arbitrary",)),
        name="in_proj",
    )(x2d, nw, w_bf, cos_t, sa_t, sb_t, *cast)


def _rope_tables(pos):
    half = ROT_DIM // 2
    d = jnp.arange(LANES) % ATT_DK
    inv = ROPE_THETA ** (-(d % half).astype(F32) / half)
    ang = pos.astype(F32)[:, None] * inv[None, :]
    cos, sin = jnp.cos(ang), jnp.sin(ang)
    first, second = (d < half)[None, :], ((d >= half) & (d < ROT_DIM))[None, :]
    return (jnp.where(first | second, cos, 1.0), jnp.where(first, -sin, 0.0), jnp.where(second, sin, 0.0))


def _lambda_full(lam_ref, lam_init):
    l = lam_ref[...]
    s1 = jnp.sum(l[0:1] * l[1:2], axis=1, keepdims=True)
    s2 = jnp.sum(l[2:3] * l[3:4], axis=1, keepdims=True)
    return jnp.exp(s1) - jnp.exp(s2) + lam_init


def _prompt_prepare(k_ref, v_ref, kbf, vt, tq, n_tiles=None):
    n_tiles = k_ref.shape[1] // tq if n_tiles is None else n_tiles
    for c in range(n_tiles):
        kbf[c * tq:(c + 1) * tq, :] = k_ref[0, c * tq:(c + 1) * tq, :].astype(BF16)
        vt[:, c * tq:(c + 1) * tq] = v_ref[0, c * tq:(c + 1) * tq, :].T.astype(BF16)


def _prompt_qtile(qi, lam, subw_ref, q_ref, o_ref, kbf, vt, s_scr, p_scr, tq, lam_init, out_pos=None, scr_off=0):
    out_pos = qi if out_pos is None else out_pos
    lane = lax.broadcasted_iota(jnp.int32, (tq, LANES), 1)
    krow = lax.broadcasted_iota(jnp.int32, (tq, 2 * tq), 0)
    qcol = lax.broadcasted_iota(jnp.int32, (tq, 2 * tq), 1)
    diag = krow <= jnp.where(qcol >= tq, qcol - tq, qcol)
    contract_last = (((1,), (1,)), ((), ()))
    q = q_ref[0, qi * tq:(qi + 1) * tq, :] * (ATT_DK ** -0.5 * LOG2_E)
    qq = jnp.concatenate([jnp.where(lane < ATT_DK, q, 0.0), jnp.where(lane >= ATT_DK, q, 0.0)],
                         axis=0).astype(BF16)
    m = None
    for c in range(qi + 1):
        s = lax.dot_general(kbf[c * tq:(c + 1) * tq, :], qq, contract_last, preferred_element_type=F32)
        if c == qi:
            s = jnp.where(diag, s, NEG_INF)
        s_scr[scr_off + c * tq:scr_off + (c + 1) * tq, :] = s
        cm = jnp.max(s, axis=0, keepdims=True)
        m = cm if m is None else jnp.maximum(m, cm)
    l = jnp.zeros((1, 2 * tq), F32)
    for c in range(qi + 1):
        p = jnp.exp2(s_scr[scr_off + c * tq:scr_off + (c + 1) * tq, :] - m)
        l = l + jnp.sum(p, axis=0, keepdims=True)
        p_scr[scr_off + c * tq:scr_off + (c + 1) * tq, :] = p.astype(BF16)
    kv = (qi + 1) * tq
    o = jnp.dot(vt[:, :kv], p_scr[scr_off:scr_off + kv, :], preferred_element_type=F32) / l
    att = o[:, :tq] - lam * o[:, tq:]
    ms = jnp.mean(att * att, axis=0, keepdims=True)
    y = att * lax.rsqrt(ms + EPS) * subw_ref[...] * (1.0 - lam_init)
    o_ref[0, out_pos * tq:(out_pos + 1) * tq, :] = y.T


def _prompt_qtiles_staggered(tiles, lam, subw_ref, q_tile_refs, o_ref, kbf, vt, s_scr, p_scr, tq, lam_init):
    lane = lax.broadcasted_iota(jnp.int32, (tq, LANES), 1)
    krow = lax.broadcasted_iota(jnp.int32, (tq, 2 * tq), 0)
    qcol = lax.broadcasted_iota(jnp.int32, (tq, 2 * tq), 1)
    diag = krow <= jnp.where(qcol >= tq, qcol - tq, qcol)
    contract_last = (((1,), (1,)), ((), ()))
    stages = []
    scr_off = 0
    for slot, qi in enumerate(tiles):
        st = {}

        def load_q(slot=slot, st=st):
            q = q_tile_refs[slot][0] * (ATT_DK ** -0.5 * LOG2_E)
            st["qq"] = jnp.concatenate([jnp.where(lane < ATT_DK, q, 0.0), jnp.where(lane >= ATT_DK, q, 0.0)],
                                       axis=0).astype(BF16)
            st["m"] = None
            st["l"] = jnp.zeros((1, 2 * tq), F32)

        def score(c, qi=qi, st=st, off=scr_off):
            s = lax.dot_general(kbf[c * tq:(c + 1) * tq, :], st["qq"], contract_last, preferred_element_type=F32)
            if c == qi:
                s = jnp.where(diag, s, NEG_INF)
            s_scr[off + c * tq:off + (c + 1) * tq, :] = s
            cm = jnp.max(s, axis=0, keepdims=True)
            st["m"] = cm if st["m"] is None else jnp.maximum(st["m"], cm)

        def prob(c, st=st, off=scr_off):
            p = jnp.exp2(s_scr[off + c * tq:off + (c + 1) * tq, :] - st["m"])
            st["l"] = st["l"] + jnp.sum(p, axis=0, keepdims=True)
            p_scr[off + c * tq:off + (c + 1) * tq, :] = p.astype(BF16)

        def finish(qi=qi, st=st, off=scr_off):
            kv = (qi + 1) * tq
            o = jnp.dot(vt[:, :kv], p_scr[off:off + kv, :], preferred_element_type=F32) / st["l"]
            att = o[:, :tq] - lam * o[:, tq:]
            ms = jnp.mean(att * att, axis=0, keepdims=True)
            y = att * lax.rsqrt(ms + EPS) * subw_ref[...] * (1.0 - lam_init)
            o_ref[0, qi * tq:(qi + 1) * tq, :] = y.T

        stages.append([[load_q] + [functools.partial(score, c) for c in range(qi + 1)],
                       [functools.partial(prob, c) for c in range(qi + 1)],
                       [finish]])
        scr_off += (qi + 1) * tq
    n_stage = 3
    for rnd in range(len(tiles) + n_stage - 1):
        active = [stages[t][rnd - t] for t in range(len(tiles)) if 0 <= rnd - t < n_stage]
        for k in range(max(len(units) for units in active)):
            for units in active:
                if k < len(units):
                    units[k]()


def _attn_prompt_kernel(lam_ref, subw_ref, q_ref, k_ref, v_ref, o_ref, kbf, vt, s_scr, p_scr, *, tq, lam_init):
    lam = _lambda_full(lam_ref, lam_init)
    _prompt_prepare(k_ref, v_ref, kbf, vt, tq)
    for qi in range(q_ref.shape[1] // tq):
        _prompt_qtile(qi, lam, subw_ref, q_ref, o_ref, kbf, vt, s_scr, p_scr, tq, lam_init)


def _prompt_scratch(s, tq, score_rows=None):
    score_rows = s if score_rows is None else score_rows
    return [pltpu.VMEM((s, LANES), BF16), pltpu.VMEM((ATT_DV, s), BF16),
            pltpu.VMEM((score_rows, 2 * tq), F32), pltpu.VMEM((score_rows, 2 * tq), BF16)]


def _attn_prompt(lam_vecs, subw_col, q, k, v, lam_init):
    b, s, _ = q.shape
    tq = min(PROMPT_Q_TILE, s)
    kern = functools.partial(_attn_prompt_kernel, tq=tq, lam_init=lam_init)
    head = pl.BlockSpec((1, s, LANES), lambda bb, h: (bb, 0, h))
    return pl.pallas_call(
        kern,
        grid=(b, N_ATT_HEADS),
        in_specs=[pl.BlockSpec((4, ATT_DK), lambda bb, h: (0, 0)),
                  pl.BlockSpec((ATT_DV, 1), lambda bb, h: (0, 0)),
                  head, head, head],
        out_specs=head,
        out_shape=jax.ShapeDtypeStruct((b, s, D_ATT), F32),
        scratch_shapes=_prompt_scratch(s, tq),
        compiler_params=_cparams(("arbitrary", "arbitrary")),
        name="attn_prompt",
    )(lam_vecs, subw_col, q, k, v)


def _expansion_matrix():
    tok = lax.broadcasted_iota(jnp.int32, (PAGE_SIZE, PAGE_SIZE * N_ATT_HEADS), 0)
    row = lax.broadcasted_iota(jnp.int32, (PAGE_SIZE, PAGE_SIZE * N_ATT_HEADS), 1)
    return (row // N_ATT_HEADS == tok).astype(BF16)


def _sample_attend(lam_ref, subw_ref, rexp_ref, q_ref, kn_ref, vn_ref, kpage, vpage, n_pages, o_ref, kbf, t_new,
                   lam_init):
    past = n_pages * PAGE_SIZE
    n_keys = past + PAGE_SIZE
    for j in range(n_pages):
        kbf[:, j * PAGE_SIZE:(j + 1) * PAGE_SIZE] = kpage(j).astype(BF16)
    zpad = jnp.zeros((PAGE_SIZE - t_new, Q_DIM), F32)
    k_tail = jnp.concatenate([kn_ref[0], zpad], axis=0).astype(BF16)

    lam = _lambda_full(lam_ref, lam_init)
    q = q_ref[0] * (ATT_DK ** -0.5)
    nr = t_new * N_ATT_HEADS
    qrep = jnp.concatenate([jnp.broadcast_to(q[t:t + 1], (N_ATT_HEADS, Q_DIM)) for t in range(t_new)], axis=0)
    row = lax.broadcasted_iota(jnp.int32, (nr, Q_DIM), 0)
    lane = lax.broadcasted_iota(jnp.int32, (nr, Q_DIM), 1)
    head = row % N_ATT_HEADS
    grp = lane // ATT_DK
    qq = jnp.concatenate([jnp.where(grp == 2 * head, qrep, 0.0), jnp.where(grp == 2 * head + 1, qrep, 0.0)],
                         axis=0).astype(BF16)
    s_past = jnp.dot(qq, kbf[...], preferred_element_type=F32)
    s_tail = lax.dot_general(qq, k_tail, (((1,), (1,)), ((), ())), preferred_element_type=F32)
    s = jnp.concatenate([s_past, s_tail], axis=1)
    srow = lax.broadcasted_iota(jnp.int32, (2 * nr, n_keys), 0)
    scol = lax.broadcasted_iota(jnp.int32, (2 * nr, n_keys), 1)
    tq = (srow // N_ATT_HEADS) % t_new
    s = jnp.where(scol <= past + tq, s, NEG_INF)
    m = jnp.max(s, axis=1, keepdims=True)
    p = jnp.exp(s - m)
    pn = p / jnp.sum(p, axis=1, keepdims=True)
    a = (pn[:nr] - lam * pn[nr:]).astype(BF16)
    a_pages = jnp.concatenate([a[:, j * PAGE_SIZE:(j + 1) * PAGE_SIZE] for j in range(n_pages + 1)], axis=0)
    a3 = jnp.dot(a_pages, rexp_ref[...], preferred_element_type=F32)
    hrow = lax.broadcasted_iota(jnp.int32, (N_ATT_HEADS, PAGE_SIZE * N_ATT_HEADS), 0)
    hcol = lax.broadcasted_iota(jnp.int32, (N_ATT_HEADS, PAGE_SIZE * N_ATT_HEADS), 1) % N_ATT_HEADS
    own = (hrow == hcol).astype(F32)
    a3 = (a3.reshape((n_pages + 1) * t_new, N_ATT_HEADS, PAGE_SIZE * N_ATT_HEADS) * own[None]).astype(BF16)
    a3 = a3.reshape((n_pages + 1) * nr, PAGE_SIZE * N_ATT_HEADS)
    v_tail = jnp.concatenate([vn_ref[0], jnp.zeros((PAGE_SIZE - nr, ATT_DV), F32)], axis=0)
    o = jnp.dot(a3[n_pages * nr:, :PAGE_SIZE], v_tail.astype(BF16), preferred_element_type=F32)
    for j in range(n_pages):
        o = o + jnp.dot(a3[j * nr:(j + 1) * nr], vpage(j).astype(BF16), preferred_element_type=F32)
    o_ref[0] = _rms_rows(o, subw_ref[...]) * (1.0 - lam_init)


def _attn_sample_kernel(pt_ref, lam_ref, subw_ref, q_ref, kn_ref, vn_ref, *rest, n_pages, t_new, lam_init):
    kpages, vpages = rest[:n_pages], rest[n_pages:2 * n_pages]
    rexp_ref, o_ref, kbf = rest[2 * n_pages:]
    _sample_attend(lam_ref, subw_ref, rexp_ref, q_ref, kn_ref, vn_ref, lambda j: kpages[j][0], lambda j: vpages[j][0],
                   n_pages, o_ref, kbf, t_new, lam_init)


def _sample_specs(page_table, t_new, seq_of_step=lambda s: s):
    n_pages = page_table.shape[1]
    tok = pl.BlockSpec((1, t_new, Q_DIM), lambda s, pt: (seq_of_step(s), 0, 0))
    rows = pl.BlockSpec((1, t_new * N_ATT_HEADS, ATT_DV), lambda s, pt: (seq_of_step(s), 0, 0))
    page_maps = [functools.partial(lambda s, pt, j: (pt[seq_of_step(s), j], 0, 0), j=j) for j in range(n_pages)]
    kpage_specs = [pl.BlockSpec((1, Q_DIM, PAGE_SIZE), pm) for pm in page_maps]
    vpage_specs = [pl.BlockSpec((1, PAGE_SIZE * N_ATT_HEADS, ATT_DV), pm) for pm in page_maps]
    in_specs = [pl.BlockSpec((4, ATT_DK), lambda b, pt: (0, 0)),
                pl.BlockSpec((1, ATT_DV), lambda b, pt: (0, 0)),
                tok, tok, rows] + kpage_specs + vpage_specs + [
                pl.BlockSpec((PAGE_SIZE, PAGE_SIZE * N_ATT_HEADS), lambda b, pt: (0, 0))]
    scratch = [pltpu.VMEM((Q_DIM, n_pages * PAGE_SIZE), BF16)]
    return in_specs, rows, scratch


def _attn_sample(page_table, lam_vecs, subw, q, kn, vn2, cache_k, cache_v, lam_init):
    db, t_new, _ = q.shape
    n_pages = page_table.shape[1]
    kern = functools.partial(_attn_sample_kernel, n_pages=n_pages, t_new=t_new, lam_init=lam_init)
    in_specs, out_spec, scratch = _sample_specs(page_table, t_new)
    grid_spec = pltpu.PrefetchScalarGridSpec(num_scalar_prefetch=1, grid=(db,), in_specs=in_specs,
                                             out_specs=out_spec, scratch_shapes=scratch)
    att = pl.pallas_call(
        kern,
        grid_spec=grid_spec,
        out_shape=jax.ShapeDtypeStruct((db, t_new * N_ATT_HEADS, ATT_DV), F32),
        compiler_params=_cparams(("arbitrary",)),
        name="attn_sample",
    )(page_table, lam_vecs, subw, q, kn, vn2, *([cache_k] * n_pages), *([cache_v] * n_pages), _expansion_matrix())
    return att.reshape(db, t_new, D_ATT)


def _fused_tile_plan(n_qtiles, steps_per_head):
    order = []
    lo, hi = 0, n_qtiles - 1
    while lo <= hi:
        order.append(lo)
        if hi != lo:
            order.append(hi)
        lo, hi = lo + 1, hi - 1
    per_step = n_qtiles // steps_per_head
    return [order[u * per_step:(u + 1) * per_step] for u in range(steps_per_head)]


def _page_copies(pt_ref, seq, slot, ck_hbm, cv_hbm, kpg, vpg, sems, n_pages):
    copies = []
    for j in range(n_pages):
        page = pt_ref[seq, j]
        copies.append(pltpu.make_async_copy(ck_hbm.at[page], kpg.at[slot, j], sems.at[slot, 0]))
        copies.append(pltpu.make_async_copy(cv_hbm.at[page], vpg.at[slot, j], sems.at[slot, 1]))
    return copies


def _attn_fused_kernel(pt_ref, lam_ref, subw_ref, q_ref, kn_ref, vn_ref, ck_hbm, cv_hbm, rexp_ref, subw_col_ref,
                       kp_ref, vp_ref, xbc_ref, dt_ref, z_ref, cw_ref, cb_ref, dtb_ref, alog_ref,
                       dskip_ref, nw_ref, *rest, n_pages, t_new, lam_init, tq, plan, ssd_rows, ssd_chunks):
    per = len(plan[0])
    q_tiles = rest[:per]
    o_ref, op_ref, y_ref, hout_ref = rest[per:per + 4]
    kpg, vpg, sems, kbf, kbf_p, vt_p, s_scr, p_scr, xprev = rest[per + 4:]
    b = pl.program_id(0)
    slot = b % 2

    @pl.when(b == 0)
    def _():
        for c in _page_copies(pt_ref, 0, 0, ck_hbm, cv_hbm, kpg, vpg, sems, n_pages):
            c.start()

    @pl.when(b + 1 < pl.num_programs(0))
    def _():
        for c in _page_copies(pt_ref, b + 1, 1 - slot, ck_hbm, cv_hbm, kpg, vpg, sems, n_pages):
            c.start()

    for c in _page_copies(pt_ref, b, slot, ck_hbm, cv_hbm, kpg, vpg, sems, n_pages):
        c.wait()
    _sample_attend(lam_ref, subw_ref, rexp_ref, q_ref, kn_ref, vn_ref, lambda j: kpg[slot, j], lambda j: vpg[slot, j],
                   n_pages, o_ref, kbf, t_new, lam_init)

    sub = b % len(plan)
    lam = _lambda_full(lam_ref, lam_init)

    @pl.when(sub == 0)
    def _():
        _prompt_prepare(kp_ref, vp_ref, kbf_p, vt_p, tq)

    for u, tiles in enumerate(plan):
        @pl.when(sub == u)
        def _(tiles=tiles):
            _prompt_qtiles_staggered(tiles, lam, subw_col_ref, q_tiles, op_ref, kbf_p, vt_p, s_scr, p_scr, tq, lam_init)

    @pl.when(b % ssd_chunks == 0)
    def _():
        hout_ref[...] = jnp.zeros_like(hout_ref)
        xprev[...] = jnp.zeros_like(xprev)

    _ssd_group(0, xbc_ref, dt_ref, z_ref, cw_ref, cb_ref, dtb_ref, alog_ref, dskip_ref, nw_ref, y_ref, hout_ref, xprev,
               seqs=1, seq_rows=ssd_rows, valid_rows=ssd_rows, carry=True)


def _fused_ssd_rows(db, bp, sp):
    if db % bp or sp % (db // bp):
        return 0
    rows = sp // (db // bp)
    return rows if rows % SUBLANES == 0 and rows <= 2 * SSD_CHUNK else 0


def _attn_fused_ok(db, bp, sp):
    heads = bp * N_ATT_HEADS
    tq = min(PROMPT_Q_TILE, sp)
    return db % heads == 0 and (sp // tq) % (db // heads) == 0 and _fused_ssd_rows(db, bp, sp) > 0


def _attn_fused(page_table, lam_vecs, subw, subw_col, q, kn, vn2, cache_k, cache_v, qp, kp, vp, lam_init, ssd_args):
    db, t_new, _ = q.shape
    bp, sp, _ = qp.shape
    n_pages = page_table.shape[1]
    tq = min(PROMPT_Q_TILE, sp)
    steps_per_head = db // (bp * N_ATT_HEADS)
    plan = _fused_tile_plan(sp // tq, steps_per_head)
    ssd_rows = _fused_ssd_rows(db, bp, sp)
    ssd_chunks = sp // ssd_rows
    kern = functools.partial(_attn_fused_kernel, n_pages=n_pages, t_new=t_new, lam_init=lam_init, tq=tq, plan=plan,
                             ssd_rows=ssd_rows, ssd_chunks=ssd_chunks)
    chunk = lambda width: pl.BlockSpec((1, ssd_rows, width), lambda b, pt: (b // ssd_chunks, b % ssd_chunks, 0))
    state_spec = pl.BlockSpec((1, SSD_HEADS, SSD_HEADDIM, SSD_STATE), lambda b, pt: (b // ssd_chunks, 0, 0, 0))
    tok = pl.BlockSpec((1, t_new, Q_DIM), lambda b, pt: (b, 0, 0))
    rows = pl.BlockSpec((1, t_new * N_ATT_HEADS, ATT_DV), lambda b, pt: (b, 0, 0))
    const = lambda b, pt: (0, 0)
    head = pl.BlockSpec((1, sp, LANES), lambda b, pt: (b // steps_per_head // N_ATT_HEADS, 0,
                                                       b // steps_per_head % N_ATT_HEADS))
    score_rows = max(sum(qi + 1 for qi in tiles) for tiles in plan) * tq
    n_qt, per = sp // tq, len(plan[0])

    def q_tile_spec(k):
        def index(b, pt):
            i = (b % steps_per_head) * per + k
            tile = jnp.where(i % 2 == 0, i // 2, n_qt - 1 - i // 2)
            return (b // steps_per_head // N_ATT_HEADS, tile, b // steps_per_head % N_ATT_HEADS)
        return pl.BlockSpec((1, tq, LANES), index)

    grid_spec = pltpu.PrefetchScalarGridSpec(
        num_scalar_prefetch=1,
        grid=(db,),
        in_specs=[pl.BlockSpec((4, ATT_DK), const), pl.BlockSpec((1, ATT_DV), const), tok, tok, rows,
                  pl.BlockSpec(memory_space=pl.ANY), pl.BlockSpec(memory_space=pl.ANY),
                  pl.BlockSpec((PAGE_SIZE, PAGE_SIZE * N_ATT_HEADS), const, pipeline_mode=pl.Buffered(1)),
                  pl.BlockSpec((ATT_DV, 1), const), head, head,
                  chunk(CONV_DIM), chunk(SSD_HEADS), chunk(D_SSD),
                  pl.BlockSpec((SSD_CONV, CONV_DIM), const), pl.BlockSpec((1, CONV_DIM), const),
                  pl.BlockSpec((1, SSD_HEADS), const), pl.BlockSpec((1, SSD_HEADS), const),
                  pl.BlockSpec((1, SSD_HEADS), const), pl.BlockSpec((1, D_SSD), const)]
                 + [q_tile_spec(k) for k in range(per)],
        out_specs=[rows, head, chunk(D_SSD), state_spec],
        scratch_shapes=[pltpu.VMEM((2, n_pages, Q_DIM, PAGE_SIZE), F32),
                        pltpu.VMEM((2, n_pages, PAGE_SIZE * N_ATT_HEADS, ATT_DV), F32),
                        pltpu.SemaphoreType.DMA((2, 2)),
                        pltpu.VMEM((Q_DIM, n_pages * PAGE_SIZE), BF16)]
                       + _prompt_scratch(sp, tq, score_rows=score_rows)
                       + [pltpu.VMEM((1, SUBLANES, CONV_DIM), F32)],
    )
    att_s, att_p, y_p, ssm_p = pl.pallas_call(
        kern,
        grid_spec=grid_spec,
        out_shape=[jax.ShapeDtypeStruct((db, t_new * N_ATT_HEADS, ATT_DV), F32),
                   jax.ShapeDtypeStruct((bp, sp, D_ATT), F32),
                   jax.ShapeDtypeStruct((bp, sp, D_SSD), F32),
                   jax.ShapeDtypeStruct((bp, SSD_HEADS, SSD_HEADDIM, SSD_STATE), F32)],
        compiler_params=pltpu.CompilerParams(dimension_semantics=("arbitrary",),
                                             vmem_limit_bytes=FUSED_ATTN_VMEM_LIMIT_BYTES),
        name="attn_fused",
    )(page_table, lam_vecs, subw, q, kn, vn2, cache_k, cache_v, _expansion_matrix(), subw_col, kp, vp,
      *ssd_args[:3], *ssd_args[4:], *([qp] * per))
    return att_s.reshape(db, t_new, D_ATT), att_p, y_p, ssm_p


def _expand_heads(v, rows):
    return jnp.concatenate([jnp.broadcast_to(v[:, h:h + 1], (rows, SSD_HEADDIM)) for h in range(SSD_HEADS)], axis=1)


def _expand_heads_mxu(vals):
    rows = vals[0].shape[0]
    parts = []
    for v in vals:
        hi = v.astype(BF16).astype(F32)
        mid = (v - hi).astype(BF16).astype(F32)
        lo = v - hi - mid
        parts.append(jnp.concatenate([hi, mid, lo], axis=1))
    lhs = jnp.concatenate(parts, axis=0).astype(BF16)
    head = lax.broadcasted_iota(jnp.int32, (3 * SSD_HEADS, D_SSD), 0) % SSD_HEADS
    col = lax.broadcasted_iota(jnp.int32, (3 * SSD_HEADS, D_SSD), 1) // SSD_HEADDIM
    out = jnp.dot(lhs, (head == col).astype(BF16), preferred_element_type=F32)
    return [out[i * rows:(i + 1) * rows] for i in range(len(vals))]


def _ssd_kernel(xbc_ref, dt_ref, z_ref, h0_ref, cw_ref, cb_ref, dtb_ref, alog_ref, dskip_ref, nw_ref,
                y_ref, hout_ref, xprev, *, groups, seqs, seq_rows, valid_rows, carry):
    c = pl.program_id(1)

    @pl.when(c == 0)
    def _():
        hout_ref[...] = h0_ref[...]
        if carry:
            xprev[...] = jnp.zeros_like(xprev)

    for gi in range(groups):
        _ssd_group(gi, xbc_ref, dt_ref, z_ref, cw_ref, cb_ref, dtb_ref, alog_ref, dskip_ref, nw_ref,
                   y_ref, hout_ref, xprev, seqs=seqs, seq_rows=seq_rows, valid_rows=valid_rows, carry=carry)


def _ssd_group(gi, xbc_ref, dt_ref, z_ref, cw_ref, cb_ref, dtb_ref, alog_ref, dskip_ref, nw_ref,
               y_ref, hout_ref, xprev, *, seqs, seq_rows, valid_rows, carry):
    rows = seqs * seq_rows
    xr = xbc_ref[gi]
    conv = cb_ref[...] + cw_ref[SSD_CONV - 1:SSD_CONV] * xr
    if carry:
        pv = xprev[gi]
        row8 = lax.broadcasted_iota(jnp.int32, (SUBLANES, CONV_DIM), 0)
    for k in range(1, SSD_CONV):
        sh = pltpu.roll(xr, k, axis=0)
        if carry:
            top = jnp.where(row8 < k, pltpu.roll(pv, k, axis=0), sh[:SUBLANES])
            sh = jnp.concatenate([top, sh[SUBLANES:]], axis=0)
        conv = conv + cw_ref[SSD_CONV - 1 - k:SSD_CONV - k] * sh
    if carry:
        xprev[gi] = xr[rows - SUBLANES:]
    xc = _silu(conv)
    xs = xc[:, :D_SSD]

    dtv = dt_ref[gi] + dtb_ref[...]
    dtv = jnp.maximum(dtv, 0.0) + jnp.log1p(jnp.exp(-jnp.abs(dtv)))
    if valid_rows < seq_rows:
        rvalid = lax.broadcasted_iota(jnp.int32, (rows, SSD_HEADS), 0) % seq_rows < valid_rows
        dtv = jnp.where(rvalid, dtv, 0.0)
    a = dtv * (-jnp.exp(alog_ref[...]))
    ri = lax.broadcasted_iota(jnp.int32, (rows, rows), 0)
    ci = lax.broadcasted_iota(jnp.int32, (rows, rows), 1)
    tril = ri >= ci
    if seqs > 1:
        tril = tril & (ri // seq_rows == ci // seq_rows)
    a_cs = jnp.dot(tril.astype(F32), a, preferred_element_type=F32, precision=lax.Precision.HIGHEST)
    totals = [a_cs[(s + 1) * seq_rows - 1:(s + 1) * seq_rows] for s in range(seqs)]
    e_tots = [jnp.exp(t) for t in totals]
    total = totals[0] if seqs == 1 else jnp.concatenate(
        [jnp.broadcast_to(t, (seq_rows, SSD_HEADS)) for t in totals], axis=0)
    dt_exp, decay_exp, e_exp = _expand_heads_mxu([dtv, jnp.exp(total - a_cs), jnp.exp(a_cs)])
    xdt = xs * dt_exp
    xdtd = (xdt * decay_exp).astype(BF16)
    eye = ri == ci
    s0 = gi * seqs

    y_diag_parts, y_off_parts = [], []
    heads_per_group = SSD_HEADS // SSD_GROUPS
    gw = heads_per_group * SSD_HEADDIM
    contract_last = (((1,), (1,)), ((), ()))
    contract_first = (((0,), (0,)), ((), ()))
    for g in range(SSD_GROUPS):
        bg = xc[:, D_SSD + g * SSD_STATE:D_SSD + (g + 1) * SSD_STATE].astype(BF16)
        cg = xc[:, D_SSD + (SSD_GROUPS + g) * SSD_STATE:D_SSD + (SSD_GROUPS + g + 1) * SSD_STATE].astype(BF16)
        y_off_seq, st_seq = [], []
        for s in range(seqs):
            rs = slice(s * seq_rows, (s + 1) * seq_rows)
            hg = hout_ref[s0 + s, g * heads_per_group:(g + 1) * heads_per_group].reshape(gw, SSD_STATE)
            y_off_seq.append(lax.dot_general(cg[rs], hg.astype(BF16), contract_last, preferred_element_type=F32))
            st_seq.append(lax.dot_general(xdtd[rs, g * gw:(g + 1) * gw], bg[rs], contract_first,
                                          preferred_element_type=F32))
        y_off = y_off_seq[0] if seqs == 1 else jnp.concatenate(y_off_seq, axis=0)
        cbm = lax.dot_general(cg, bg, contract_last, preferred_element_type=F32)
        for r in range(heads_per_group):
            h = g * heads_per_group + r
            col = a_cs[:, h:h + 1]
            rowv = jnp.sum(jnp.where(eye, col, 0.0), axis=0, keepdims=True)
            lmat = jnp.where(tril, jnp.exp(col - rowv), 0.0)
            mm = (cbm * lmat).astype(BF16)
            y_diag_parts.append(jnp.dot(mm, xdt[:, h * SSD_HEADDIM:(h + 1) * SSD_HEADDIM].astype(BF16),
                                        preferred_element_type=F32))
            for s in range(seqs):
                hout_ref[s0 + s, h] = (hout_ref[s0 + s, h] * e_tots[s][:, h:h + 1]
                                       + st_seq[s][r * SSD_HEADDIM:(r + 1) * SSD_HEADDIM])
        y_off_parts.append(y_off)
    y = (jnp.concatenate(y_diag_parts, axis=1) + jnp.concatenate(y_off_parts, axis=1) * e_exp
         + _expand_heads(dskip_ref[...], 1) * xs)
    zz = z_ref[gi]
    y = y * _silu(zz)
    half = D_SSD // SSD_GROUPS
    nw = nw_ref[...]
    y_ref[gi] = jnp.concatenate([_rms_rows(y[:, g * half:(g + 1) * half], nw[:, g * half:(g + 1) * half])
                                for g in range(SSD_GROUPS)], axis=1)


def _ssd(xbc, dt, z, h0, cw, cb, dtb, alog, dskip, nw, *, seqs, seq_rows, valid_rows):
    b, s, _ = xbc.shape
    rows = seqs * seq_rows
    n_chunks = s // rows
    assert seqs == 1 or n_chunks == 1
    groups = math.gcd(b, SSD_GROUPS_PER_STEP) if seqs == 1 else 1
    kern = functools.partial(_ssd_kernel, groups=groups, seqs=seqs, seq_rows=seq_rows, valid_rows=valid_rows,
                             carry=seqs == 1)
    tokmap = lambda bb, c: (bb, c, 0)
    seqmap4 = lambda bb, c: (bb, 0, 0, 0)
    const = lambda bb, c: (0, 0)
    state_spec = pl.BlockSpec((groups * seqs, SSD_HEADS, SSD_HEADDIM, SSD_STATE), seqmap4)
    return pl.pallas_call(
        kern,
        grid=(b // groups, n_chunks),
        in_specs=[pl.BlockSpec((groups, rows, CONV_DIM), tokmap),
                  pl.BlockSpec((groups, rows, SSD_HEADS), tokmap),
                  pl.BlockSpec((groups, rows, D_SSD), tokmap),
                  state_spec,
                  pl.BlockSpec((SSD_CONV, CONV_DIM), const),
                  pl.BlockSpec((1, CONV_DIM), const),
                  pl.BlockSpec((1, SSD_HEADS), const),
                  pl.BlockSpec((1, SSD_HEADS), const),
                  pl.BlockSpec((1, SSD_HEADS), const),
                  pl.BlockSpec((1, D_SSD), const)],
        out_specs=[pl.BlockSpec((groups, rows, D_SSD), tokmap), state_spec],
        out_shape=[jax.ShapeDtypeStruct((b, s, D_SSD), F32),
                   jax.ShapeDtypeStruct((b * seqs, SSD_HEADS, SSD_HEADDIM, SSD_STATE), F32)],
        scratch_shapes=[pltpu.VMEM((groups, SUBLANES, CONV_DIM), F32)],
        compiler_params=_cparams(("arbitrary", "arbitrary")),
        name="ssd",
    )(xbc, dt, z, h0, cw, cb, dtb, alog, dskip, nw)


def _outproj_kernel(x_ref, att_ref, y_ref, wa_ref, wy_ref, nw_ref, *rest, n_cast):
    cast_in, (x1_ref, hf_ref), cast_out = rest[:n_cast], rest[n_cast:n_cast + 2], rest[n_cast + 2:]
    for src, dst in zip(cast_in, cast_out):
        dst[...] = src[...].astype(BF16)
    mix = jnp.dot(att_ref[...].astype(BF16), wa_ref[...], preferred_element_type=F32)
    mix = mix + jnp.dot(y_ref[...].astype(BF16), wy_ref[...], preferred_element_type=F32)
    x1 = x_ref[...] + mix
    x1_ref[...] = x1
    hf_ref[...] = _rms_rows(x1, nw_ref[...]).astype(BF16)


def _out_proj(x2d, att, y, w_out, nw, *, cast=()):
    assert D_ATT == D_SSD
    m = x2d.shape[0]
    tm = min(512, m)
    row = lambda i: (i, 0)
    const = lambda i: (0, 0)
    cast_specs, cast_shapes = _cast_specs(cast, m // tm)
    return pl.pallas_call(
        functools.partial(_outproj_kernel, n_cast=len(cast)),
        grid=(m // tm,),
        in_specs=[pl.BlockSpec((tm, D_MODEL), row), pl.BlockSpec((tm, D_ATT), row), pl.BlockSpec((tm, D_SSD), row),
                  pl.BlockSpec((D_ATT, D_MODEL), const, pipeline_mode=pl.Buffered(1)),
                  pl.BlockSpec((D_SSD, D_MODEL), lambda i: (1, 0), pipeline_mode=pl.Buffered(1)),
                  pl.BlockSpec((1, D_MODEL), const)] + cast_specs,
        out_specs=[pl.BlockSpec((tm, D_MODEL), row), pl.BlockSpec((tm, D_MODEL), row)] + cast_specs,
        out_shape=[jax.ShapeDtypeStruct((m, D_MODEL), F32), jax.ShapeDtypeStruct((m, D_MODEL), BF16)] + cast_shapes,
        compiler_params=_cparams(("arbitrary",)),
        name="out_proj",
    )(x2d, att, y, w_out, w_out, nw, *cast)


def _ffn_kernel(x1_ref, hf_ref, wg_ref, wu_ref, wd_ref, cw_ref, cb_ref, nfw_ref, *rest,
                tm, seq_len, tiles_per_seq):
    if seq_len >= tm:
        y_ref, gt_ref, gprev = rest
    else:
        st_ref, y_ref, gt_ref, p1_scr, p2_scr, g_scr = rest
    i = pl.program_id(0)
    f = pl.program_id(1)
    nf = pl.num_programs(1)

    @pl.when(f == 0)
    def _():
        y_ref[...] = jnp.zeros_like(y_ref)

    hf = hf_ref[...]
    g = jnp.dot(hf, wg_ref[...], preferred_element_type=F32)
    u = jnp.dot(hf, wu_ref[...], preferred_element_type=F32)
    tf = g.shape[1]
    if seq_len >= tm:
        gp = gprev[f]
        gp = jnp.where(i % tiles_per_seq == 0, 0.0, gp)
        row8 = lax.broadcasted_iota(jnp.int32, (SUBLANES, tf), 0)
        shifted = []
        for k in range(1, FFN_CONV):
            sh = pltpu.roll(g, k, axis=0)
            top = jnp.where(row8 < k, pltpu.roll(gp, k, axis=0), sh[:SUBLANES])
            shifted.append(jnp.concatenate([top, sh[SUBLANES:]], axis=0))
        g1, g2 = shifted
        gprev[f] = g[tm - SUBLANES:]
        gt_ref[0] = g[tm - SUBLANES:]
    else:
        ns = tm // seq_len

        @pl.when((i == 0) & (f == 0))
        def _():
            p1_scr[...] = jnp.zeros_like(p1_scr)
            p2_scr[...] = jnp.zeros_like(p2_scr)

        for c in range(tf // LANES):
            cs = slice(c * LANES, (c + 1) * LANES)
            g_scr[c] = g[:, cs]
            p1_scr[c, pl.ds(0, ns, stride=seq_len), :] = st_ref[1, :, cs]
            p2_scr[c, pl.ds(0, ns, stride=seq_len), :] = st_ref[0, :, cs]
            p2_scr[c, pl.ds(1, ns, stride=seq_len), :] = st_ref[1, :, cs]
            gt_ref[0, :, cs] = g_scr[c, pl.ds(seq_len - 2, ns, stride=seq_len), :]
            gt_ref[1, :, cs] = g_scr[c, pl.ds(seq_len - 1, ns, stride=seq_len), :]
        p1 = jnp.concatenate([p1_scr[c] for c in range(tf // LANES)], axis=1)
        p2 = jnp.concatenate([p2_scr[c] for c in range(tf // LANES)], axis=1)
        pos = lax.broadcasted_iota(jnp.int32, (tm, tf), 0) % seq_len
        g1 = jnp.where(pos >= 1, pltpu.roll(g, 1, axis=0), 0.0) + p1
        g2 = jnp.where(pos >= 2, pltpu.roll(g, 2, axis=0), 0.0) + p2
    gc = cb_ref[...] + cw_ref[0:1] * g2 + cw_ref[1:2] * g1 + cw_ref[2:3] * g
    act = (_silu(gc) * u).astype(BF16)
    y_ref[...] += jnp.dot(act, wd_ref[...], preferred_element_type=F32)

    @pl.when(f == nf - 1)
    def _():
        y_ref[...] = _rms_rows(x1_ref[...] + y_ref[...], nfw_ref[...])


def _ffn(x1, hf, wg, wu, wd, cw, cb, nfw, state, *, seq_len):
    m = x1.shape[0]
    tm = min(FFN_ROWS, m)
    prompt_mode = seq_len >= tm
    tf = FFN_TILE
    nf = D_FF // tf
    tiles_per_seq = max(seq_len // tm, 1)
    kern = functools.partial(_ffn_kernel, tm=tm, seq_len=seq_len, tiles_per_seq=tiles_per_seq)
    row = lambda i, f: (i, 0)
    in_specs = [pl.BlockSpec((tm, D_MODEL), row), pl.BlockSpec((tm, D_MODEL), row),
                pl.BlockSpec((D_MODEL, tf), lambda i, f: (0, f)), pl.BlockSpec((D_MODEL, tf), lambda i, f: (0, f)),
                pl.BlockSpec((tf, D_MODEL), lambda i, f: (f, 0)),
                pl.BlockSpec((FFN_CONV, tf), lambda i, f: (0, f)), pl.BlockSpec((1, tf), lambda i, f: (0, f)),
                pl.BlockSpec((1, D_MODEL), lambda i, f: (0, 0))]
    args = [x1, hf, wg, wu, wd, cw, cb, nfw]
    scratch = []
    if prompt_mode:
        out_specs = [pl.BlockSpec((tm, D_MODEL), row), pl.BlockSpec((1, SUBLANES, tf), lambda i, f: (i, 0, f))]
        out_shape = [jax.ShapeDtypeStruct((m, D_MODEL), F32), jax.ShapeDtypeStruct((m // tm, SUBLANES, D_FF), F32)]
        scratch.append(pltpu.VMEM((nf, SUBLANES, tf), F32))
    else:
        ns = tm // seq_len
        state_spec = pl.BlockSpec((FFN_CONV - 1, ns, tf), lambda i, f: (0, i, f))
        in_specs.append(state_spec)
        args.append(state)
        out_specs = [pl.BlockSpec((tm, D_MODEL), row), state_spec]
        out_shape = [jax.ShapeDtypeStruct((m, D_MODEL), F32), jax.ShapeDtypeStruct(state.shape, F32)]
        scratch += [pltpu.VMEM((tf // LANES, tm, LANES), F32)] * 3
    return pl.pallas_call(
        kern,
        grid=(m // tm, nf),
        in_specs=in_specs,
        out_specs=out_specs,
        out_shape=out_shape,
        scratch_shapes=scratch,
        compiler_params=_cparams(("arbitrary", "arbitrary")),
        name="ffn",
    )(*args)


def _layer(xp, xs, past, lam_init, p):
    (cache_k, cache_v, state_ssm, state_conv_ssd, state_conv_ffn, page_table) = past
    bp, sp, _ = xp.shape
    db, ds, _ = xs.shape
    n_pages = page_table.shape[1]
    past_len = n_pages * PAGE_SIZE

    w_in = p["w_in"].astype(BF16)
    nmw = p["norm_mix_w"].reshape(1, D_MODEL)
    lam_vecs = jnp.stack([p["lambda_q1"], p["lambda_k1"], p["lambda_q2"], p["lambda_k2"]])
    subw = p["subln_w"].reshape(1, ATT_DV)
    cw_ssd = p["conv_ssd_w"]
    cb_ssd = p["conv_ssd_b"].reshape(1, CONV_DIM)
    dtb = p["dt_bias"].reshape(1, SSD_HEADS)
    alog = p["a_log"].reshape(1, SSD_HEADS)
    dskip = p["d_skip"].reshape(1, SSD_HEADS)
    nsw = p["norm_ssd_w"].reshape(1, D_SSD)
    nfw = p["norm_ffn_w"].reshape(1, D_MODEL)
    cw_ffn = p["conv_ffn_w"]
    cb_ffn = p["conv_ffn_b"].reshape(1, D_FF)
    nfin = p["norm_final_w"].reshape(1, D_MODEL)

    q, k, v, z, xbc, dt, kt, wg, wu, w_out = _in_proj(
        xp.reshape(bp * sp, D_MODEL), nmw, w_in, *_rope_tables(jnp.arange(sp)), kt_seq_len=sp,
        cast=(p["w_gate"], p["w_up"], p["w_out"]))
    r3 = lambda t: t.reshape(bp, sp, t.shape[-1])
    pos_s = jnp.tile(past_len + jnp.arange(ds), _in_proj_tile(db * ds) // ds)
    q_s, k_s, v_s, z_s, xbc_s, dt_s = _in_proj(xs.reshape(db * ds, D_MODEL), nmw, w_in, *_rope_tables(pos_s))
    r3s = lambda t: t.reshape(db, ds, t.shape[-1])
    n_phys = cache_k.shape[0]
    cache_kt = jnp.transpose(cache_k, (0, 2, 3, 4, 1)).reshape(n_phys, Q_DIM, PAGE_SIZE)
    cache_vr = cache_v.reshape(n_phys, PAGE_SIZE * N_ATT_HEADS, ATT_DV)
    subw_col = subw.reshape(ATT_DV, 1)
    vn2 = v_s.reshape(db, ds * N_ATT_HEADS, ATT_DV)
    ssd_args = (r3(xbc), r3(dt), r3(z), jnp.zeros((bp, SSD_HEADS, SSD_HEADDIM, SSD_STATE), F32),
                cw_ssd, cb_ssd, dtb, alog, dskip, nsw)
    if _attn_fused_ok(db, bp, sp):
        att_s, att, y, ssm_p = _attn_fused(page_table, lam_vecs, subw, subw_col, r3s(q_s), r3s(k_s), vn2,
                                           cache_kt, cache_vr, r3(q), r3(k), r3(v), lam_init, ssd_args)
    else:
        att = _attn_prompt(lam_vecs, subw_col, r3(q), r3(k), r3(v), lam_init)
        att_s = _attn_sample(page_table, lam_vecs, subw, r3s(q_s), r3s(k_s), vn2, cache_kt, cache_vr, lam_init)
        rows = SSD_CHUNK if sp % SSD_CHUNK == 0 else sp
        y, ssm_p = _ssd(*ssd_args, seqs=1, seq_rows=rows, valid_rows=rows)

    x1, hf, wd = _out_proj(xp.reshape(bp * sp, D_MODEL), att.reshape(bp * sp, D_ATT), y.reshape(bp * sp, D_SSD),
                           w_out, nfw, cast=(p["w_down"],))
    yp, gtail = _ffn(x1, hf, wg, wu, wd, cw_ffn, cb_ffn, nfin, None, seq_len=sp)
    tiles_per_seq = gtail.shape[0] // bp
    conv_ffn_p = gtail.reshape(bp, tiles_per_seq, SUBLANES, D_FF)[:, -1, SUBLANES - (FFN_CONV - 1):]
    new_k_p = jnp.transpose(kt.reshape(bp, N_ATT_HEADS, 2, ATT_DK, sp), (0, 4, 1, 2, 3))
    prompt_out = (yp.reshape(bp, sp, D_MODEL), new_k_p, v.reshape(bp, sp, N_ATT_HEADS, ATT_DV),
                  ssm_p, r3(xbc)[:, sp - (SSD_CONV - 1):], conv_ffn_p)

    k, v, z, xbc, dt, att = k_s, v_s, z_s, xbc_s, dt_s, att_s
    gs = math.gcd(db, SSD_SAMPLE_GROUP)
    pad8 = lambda t: jnp.pad(t, ((0, 0), (0, SUBLANES - ds), (0, 0))).reshape(db // gs, gs * SUBLANES, t.shape[-1])
    nxt_state = jnp.roll(state_conv_ssd.reshape(db // gs, gs, SSD_CONV - 1, CONV_DIM), -1, axis=1)
    xe = jnp.concatenate([r3s(xbc), jnp.zeros((db, SUBLANES - ds - (SSD_CONV - 1), CONV_DIM), F32),
                          nxt_state.reshape(db, SSD_CONV - 1, CONV_DIM)], axis=1)
    y8, ssm_s = _ssd(xe.reshape(db // gs, gs * SUBLANES, CONV_DIM), pad8(r3s(dt)), pad8(r3s(z)), state_ssm,
                     cw_ssd, cb_ssd, dtb, alog, dskip, nsw, seqs=gs, seq_rows=SUBLANES, valid_rows=ds)
    y = y8.reshape(db, SUBLANES, D_SSD)[:, :ds]
    x1, hf = _out_proj(xs.reshape(db * ds, D_MODEL), att.reshape(db * ds, D_ATT), y.reshape(db * ds, D_SSD), w_out, nfw)
    ys, conv_ffn_s = _ffn(x1, hf, wg, wu, wd, cw_ffn, cb_ffn, nfin, jnp.transpose(state_conv_ffn, (1, 0, 2)),
                          seq_len=ds)
    sample_out = (ys.reshape(db, ds, D_MODEL),
                  k.reshape(db, ds, N_ATT_HEADS, 2, ATT_DK), v.reshape(db, ds, N_ATT_HEADS, ATT_DV),
                  ssm_s, r3s(xbc)[:, ds - (SSD_CONV - 1):], jnp.transpose(conv_ffn_s, (1, 0, 2)))
    return prompt_out, sample_out


def kernel(x_prompt, x_sample, cache_k, cache_v, state_ssm, state_conv_ssd, state_conv_ffn, page_table, norm_mix_w, w_in, lambda_q1, lambda_k1, lambda_q2, lambda_k2, subln_w, conv_ssd_w, conv_ssd_b, dt_bias, a_log, d_skip, norm_ssd_w, w_out, norm_ffn_w, w_gate, w_up, conv_ffn_w, conv_ffn_b, w_down, norm_final_w):
    depth = w_in.shape[0]
    assert depth == 1, "the final RMSNorm is fused into the (single) layer's FFN kernel"
    lam_init = 0.8 - 0.6 * math.exp(-0.3 * 0)
    params = dict(norm_mix_w=norm_mix_w[0], w_in=w_in[0], lambda_q1=lambda_q1[0], lambda_k1=lambda_k1[0],
                  lambda_q2=lambda_q2[0], lambda_k2=lambda_k2[0], subln_w=subln_w[0], conv_ssd_w=conv_ssd_w[0],
                  conv_ssd_b=conv_ssd_b[0], dt_bias=dt_bias[0], a_log=a_log[0], d_skip=d_skip[0],
                  norm_ssd_w=norm_ssd_w[0], w_out=w_out[0], norm_ffn_w=norm_ffn_w[0], w_gate=w_gate[0],
                  w_up=w_up[0], conv_ffn_w=conv_ffn_w[0], conv_ffn_b=conv_ffn_b[0], w_down=w_down[0],
                  norm_final_w=norm_final_w)
    past = (cache_k[0], cache_v[0], state_ssm[0], state_conv_ssd[0], state_conv_ffn[0], page_table)
    (yp, kp, vp, sp_, cp, fp), (ys, ks, vs, ss, cs, fs) = _layer(x_prompt, x_sample, past, lam_init, params)
    lead = lambda t: t[None]
    return (yp, ys, lead(kp), lead(vp), lead(sp_), lead(cp), lead(fp),
            lead(ks), lead(vs), lead(ss), lead(cs), lead(fs))
```
